```python
import math
import jax, jax.numpy as jnp
from jax import lax
import numpy as np

D_MODEL = 4096
BATCH = 2
SEQ = 4096
DEPTH = 1

HEAD_DIM = 128
BLOCK = 128
GM_GROUPS = 12
GM_GROUP_DIM = 128
GM_WIDTH = GM_GROUPS * GM_GROUP_DIM
DIL_PAIRS = ((128, 1), (512, 4), (2048, 16))
DIL_HEADS = 4
DIL_GROUP_WIDTH = DIL_HEADS * HEAD_DIM
DIL_WIDTH = len(DIL_PAIRS) * DIL_GROUP_WIDTH
MEM_LEN = 256
MEM_HEADS = 4
MEM_HEAD_DIM = 256
MEM_WIDTH = MEM_HEADS * MEM_HEAD_DIM
N_BRANCHES = 3
IN_WIDTH = 2 * GM_WIDTH + 3 * DIL_WIDTH + MEM_WIDTH
PEER_HEADS = 8
PEER_KEYS = 128
PEER_EXPERTS = PEER_KEYS * PEER_KEYS
PEER_TOPK = 16
PEER_QUERY_DIM = 256
PEER_TOKEN_BLOCK = 128
ROPE_THETA = 10000.0
LN_EPS = 1e-5
DEEPNORM_ALPHA = (2 * DEPTH) ** 0.25
DEEPNORM_BETA = (8 * DEPTH) ** -0.25
NEG_INF = -1e30

kernel_name = "hybrid_gmlp_dilated_mem_peer_layer"


def layer_norm(x, g, b):
    xf = x.astype(jnp.float32)
    mu = jnp.mean(xf, axis=-1, keepdims=True)
    var = jnp.mean(jnp.square(xf - mu), axis=-1, keepdims=True)
    y = (xf - mu) * lax.rsqrt(var + LN_EPS)
    return (y * g.astype(jnp.float32) + b.astype(jnp.float32)).astype(x.dtype)


def rotary(t, positions):
    half = t.shape[-1] // 2
    inv_freq = ROPE_THETA ** (-jnp.arange(half, dtype=jnp.float32) / half)
    ang = positions.astype(jnp.float32)[..., None] * inv_freq
    cos = jnp.cos(ang)[:, :, None, :]
    sin = jnp.sin(ang)[:, :, None, :]
    tf = t.astype(jnp.float32)
    t1, t2 = tf[..., :half], tf[..., half:]
    return jnp.concatenate([t1 * cos - t2 * sin, t2 * cos + t1 * sin], axis=-1).astype(t.dtype)


def chunked_gmlp(u, v, ln_g, ln_b, w_s, b_s):
    B, S, _ = v.shape
    v = layer_norm(v, ln_g, ln_b)
    vb = v.reshape(B, S // BLOCK, BLOCK, GM_GROUPS, GM_GROUP_DIM)
    causal = jnp.tril(jnp.ones((BLOCK, BLOCK), dtype=bool))
    w = jnp.where(causal[None], w_s, jnp.zeros_like(w_s))
    mixed = jnp.einsum('gts,bnsgc->bntgc', w, vb) + b_s.T[None, None, :, :, None]
    return u * mixed.reshape(B, S, GM_WIDTH)


def dilated_window_attention(q, k, v, window, dilation):
    B, S, H, C = q.shape
    L = S // dilation
    nb = -(-L // BLOCK)
    Lp = nb * BLOCK
    span = window // dilation

    def to_sub(t):
        t = t.reshape(B, L, dilation, H, C).transpose(0, 2, 1, 3, 4)
        t = jnp.pad(t, ((0, 0), (0, 0), (0, Lp - L), (0, 0), (0, 0)))
        return t.reshape(B, dilation, nb, BLOCK, H, C)

    def with_prev(t):
        prev = jnp.pad(t[:, :, :-1], ((0, 0), (0, 0), (1, 0), (0, 0), (0, 0), (0, 0)))
        return jnp.concatenate([prev, t], axis=3)

    qs = to_sub(q)
    kk = with_prev(to_sub(k))
    vv = with_prev(to_sub(v))
    scores = jnp.einsum('brnqhc,brnkhc->brnhqk', qs, kk).astype(jnp.float32) * (C ** -0.5)
    qi = BLOCK + jnp.arange(BLOCK)[:, None]
    ki = jnp.arange(2 * BLOCK)[None, :]
    dist = qi - ki
    local = (dist >= 0) & (dist <= span)
    has_prev = (jnp.arange(nb)[:, None, None] > 0) | (ki[None] >= BLOCK)
    mask = local[None] & has_prev
    scores = jnp.where(mask[None, None, :, None], scores, NEG_INF)
    lse = jax.nn.logsumexp(scores, axis=-1)
    probs = jnp.exp(scores - lse[..., None])
    out = jnp.einsum('brnhqk,brnkhc->brnqhc', probs.astype(v.dtype), vv)
    out = out.reshape(B, dilation, Lp, H, C)[:, :, :L].transpose(0, 2, 1, 3, 4).reshape(B, S, H, C)
    lse = lse.transpose(0, 1, 2, 4, 3).reshape(B, dilation, Lp, H)[:, :, :L]
    lse = lse.transpose(0, 2, 1, 3).reshape(B, S, H)
    return out, lse


def mixer_block(h, mem, positions, w_in, gm_ln_g, gm_ln_b, gm_w_s, gm_b_s, w_mem_kv,
                w_gate, b_gate, w_br_gmlp, w_br_dil, w_br_mem, w_out):
    B, S, D = h.shape
    proj = h @ w_in
    splits = [int(c) for c in np.cumsum([GM_WIDTH, GM_WIDTH, DIL_WIDTH, DIL_WIDTH, DIL_WIDTH])]
    gu, gv, dq, dk, dv, mq = jnp.split(proj, splits, axis=-1)

    a_out = chunked_gmlp(jax.nn.gelu(gu), jax.nn.gelu(gv), gm_ln_g, gm_ln_b, gm_w_s, gm_b_s)

    n_dil_heads = len(DIL_PAIRS) * DIL_HEADS
    dq = rotary(dq.reshape(B, S, n_dil_heads, HEAD_DIM), positions)
    dk = rotary(dk.reshape(B, S, n_dil_heads, HEAD_DIM), positions)
    dv = dv.reshape(B, S, n_dil_heads, HEAD_DIM)
    outs, lses = [], []
    for g, (window, dilation) in enumerate(DIL_PAIRS):
        sl = slice(g * DIL_HEADS, (g + 1) * DIL_HEADS)
        o, l = dilated_window_attention(dq[:, :, sl], dk[:, :, sl], dv[:, :, sl], window, dilation)
        outs.append(o)
        lses.append(l)
    weights = jax.nn.softmax(jnp.stack(lses, axis=0), axis=0)
    b_out = jnp.einsum('gbsh,gbshc->bshc', weights.astype(dv.dtype), jnp.stack(outs, axis=0))
    b_out = b_out.reshape(B, S, DIL_GROUP_WIDTH)

    mkv = (mem @ w_mem_kv).reshape(B, MEM_LEN, 2, MEM_HEADS, MEM_HEAD_DIM)
    mq = mq.reshape(B, S, MEM_HEADS, MEM_HEAD_DIM)
    ms = jnp.einsum('bshc,bmhc->bhsm', mq, mkv[:, :, 0]).astype(jnp.float32) * (MEM_HEAD_DIM ** -0.5)
    mp = jax.nn.softmax(ms, axis=-1)
    m_out = jnp.einsum('bhsm,bmhc->bshc', mp.astype(mq.dtype), mkv[:, :, 1]).reshape(B, S, MEM_WIDTH)

    gates = jax.nn.sigmoid(h @ w_gate + b_gate).reshape(B, S, N_BRANCHES, D)
    merged = (gates[:, :, 0] * (a_out @ w_br_gmlp)
              + gates[:, :, 1] * (b_out @ w_br_dil)
              + gates[:, :, 2] * (m_out @ w_br_mem))
    return merged @ w_out


def peer_ffn(h, w_query, sub_keys, expert_u, expert_v):
    B, S, D = h.shape
    T = B * S
    xt = h.reshape(T, D)
    q = (xt @ w_query).reshape(T, PEER_HEADS, 2, PEER_QUERY_DIM // 2)
    s = jnp.einsum('thpc,hpkc->thpk', q, sub_keys).astype(jnp.float32)
    s_top, i_top = lax.top_k(s, PEER_TOPK)
    cand = s_top[:, :, 0, :, None] + s_top[:, :, 1, None, :]
    cand_idx = i_top[:, :, 0, :, None] * PEER_KEYS + i_top[:, :, 1, None, :]
    best, pos = lax.top_k(cand.reshape(T, PEER_HEADS, PEER_TOPK * PEER_TOPK), PEER_TOPK)
    experts = jnp.take_along_axis(cand_idx.reshape(T, PEER_HEADS, PEER_TOPK * PEER_TOPK), pos, axis=-1)
    gate = jax.nn.softmax(best, axis=-1).astype(h.dtype)

    n_sel = PEER_HEADS * PEER_TOPK
    nblk = T // PEER_TOKEN_BLOCK

    def block_fn(args):
        xb, eb, gb = args
        u = expert_u[eb]
        act = jax.nn.gelu(jnp.einsum('td,ted->te', xb, u)) * gb
        return jnp.einsum('te,ted->td', act, expert_v[eb])

    y = lax.map(block_fn, (xt.reshape(nblk, PEER_TOKEN_BLOCK, D),
                           experts.reshape(nblk, PEER_TOKEN_BLOCK, n_sel),
                           gate.reshape(nblk, PEER_TOKEN_BLOCK, n_sel)))
    return y.reshape(B, S, D)


def setup_inputs(seed: int = 0) -> dict:
    key = jax.random.key(seed)
    ks = jax.random.split(key, 24)

    def nrm(k, shape, scale):
        return jax.random.normal(k, shape, dtype=jnp.float32) * scale

    D = D_MODEL
    x = nrm(ks[0], (BATCH, SEQ, D), 1.0)
    mem = nrm(ks[1], (BATCH, MEM_LEN, D), 1.0)
    positions = (jax.random.randint(ks[2], (BATCH, 1), 0, 1024, dtype=jnp.int32)
                 + jnp.arange(SEQ, dtype=jnp.int32)[None, :])
    col_scale = jnp.concatenate([
        jnp.ones((2 * GM_WIDTH + 2 * DIL_WIDTH,), jnp.float32),
        jnp.full((DIL_WIDTH,), DEEPNORM_BETA, jnp.float32),
        jnp.ones((MEM_WIDTH,), jnp.float32)])
    w_in = nrm(ks[3], (DEPTH, D, IN_WIDTH), D ** -0.5) * col_scale
    gm_ln_g = 1.0 + nrm(ks[4], (DEPTH, GM_WIDTH), 0.01)
    gm_ln_b = nrm(ks[5], (DEPTH, GM_WIDTH), 0.01)
    gm_w_s = nrm(ks[6], (DEPTH, GM_GROUPS, BLOCK, BLOCK), BLOCK ** -0.5)
    gm_b_s = 1.0 + nrm(ks[7], (DEPTH, GM_GROUPS, BLOCK), 0.01)
    kv_scale = jnp.concatenate([jnp.ones((MEM_WIDTH,), jnp.float32),
                                jnp.full((MEM_WIDTH,), DEEPNORM_BETA, jnp.float32)])
    w_mem_kv = nrm(ks[8], (DEPTH, D, 2 * MEM_WIDTH), D ** -0.5) * kv_scale
    w_gate = nrm(ks[9], (DEPTH, D, N_BRANCHES * D), D ** -0.5)
    b_gate = nrm(ks[10], (DEPTH, N_BRANCHES * D), 0.01)
    w_br_gmlp = nrm(ks[11], (DEPTH, GM_WIDTH, D), GM_WIDTH ** -0.5 * DEEPNORM_BETA)
    w_br_dil = nrm(ks[12], (DEPTH, DIL_GROUP_WIDTH, D), DIL_GROUP_WIDTH ** -0.5 * DEEPNORM_BETA)
    w_br_mem = nrm(ks[13], (DEPTH, MEM_WIDTH, D), MEM_WIDTH ** -0.5 * DEEPNORM_BETA)
    w_out = nrm(ks[14], (DEPTH, D, D), D ** -0.5 * DEEPNORM_BETA)
    ln_mix_g = 1.0 + nrm(ks[15], (DEPTH, D), 0.01)
    ln_mix_b = nrm(ks[16], (DEPTH, D), 0.01)
    peer_w_query = nrm(ks[17], (DEPTH, D, PEER_HEADS * PEER_QUERY_DIM), D ** -0.5)
    peer_sub_keys = nrm(ks[18], (DEPTH, PEER_HEADS, 2, PEER_KEYS, PEER_QUERY_DIM // 2),
                        (PEER_QUERY_DIM // 2) ** -0.5)
    peer_u = nrm(ks[19], (DEPTH, PEER_EXPERTS, D), D ** -0.5)
    peer_v = nrm(ks[20], (DEPTH, PEER_EXPERTS, D), DEEPNORM_BETA)
    ln_ffn_g = 1.0 + nrm(ks[21], (DEPTH, D), 0.01)
    ln_ffn_b = nrm(ks[22], (DEPTH, D), 0.01)
    return {"x": x, "mem": mem, "positions": positions, "w_in": w_in,
            "gm_ln_g": gm_ln_g, "gm_ln_b": gm_ln_b, "gm_w_s": gm_w_s, "gm_b_s": gm_b_s,
            "w_mem_kv": w_mem_kv, "w_gate": w_gate, "b_gate": b_gate,
            "w_br_gmlp": w_br_gmlp, "w_br_dil": w_br_dil, "w_br_mem": w_br_mem,
            "w_out": w_out, "ln_mix_g": ln_mix_g, "ln_mix_b": ln_mix_b,
            "peer_w_query": peer_w_query, "peer_sub_keys": peer_sub_keys,
            "peer_u": peer_u, "peer_v": peer_v, "ln_ffn_g": ln_ffn_g, "ln_ffn_b": ln_ffn_b}


def reference(x, mem, positions, w_in, gm_ln_g, gm_ln_b, gm_w_s, gm_b_s, w_mem_kv,
              w_gate, b_gate, w_br_gmlp, w_br_dil, w_br_mem, w_out, ln_mix_g, ln_mix_b,
              peer_w_query, peer_sub_keys, peer_u, peer_v, ln_ffn_g, ln_ffn_b):
    h = x
    for l in range(DEPTH):
        y = mixer_block(h, mem, positions, w_in[l], gm_ln_g[l], gm_ln_b[l], gm_w_s[l], gm_b_s[l],
                        w_mem_kv[l], w_gate[l], b_gate[l], w_br_gmlp[l], w_br_dil[l],
                        w_br_mem[l], w_out[l])
        h = layer_norm(DEEPNORM_ALPHA * h + y, ln_mix_g[l], ln_mix_b[l])
        y = peer_ffn(h, peer_w_query[l], peer_sub_keys[l], peer_u[l], peer_v[l])
        h = layer_norm(DEEPNORM_ALPHA * h + y, ln_ffn_g[l], ln_ffn_b[l])
    return h
```

```python
import functools

import jax
import jax.numpy as jnp
from jax import lax
from jax.experimental import pallas as pl
from jax.experimental.pallas import tpu as pltpu

F32 = jnp.float32
BF16 = jnp.bfloat16

HEAD_DIM = 128
BLOCK = 128
GM_GROUPS = 12
GM_WIDTH = GM_GROUPS * 128
DILATIONS = (1, 4, 16)
DIL_HEADS = 4
DIL_GROUP_WIDTH = DIL_HEADS * HEAD_DIM
DIL_WIDTH = len(DILATIONS) * DIL_GROUP_WIDTH
MEM_HEADS = 4
MEM_HEAD_DIM = 256
MEM_WIDTH = MEM_HEADS * MEM_HEAD_DIM
PEER_HEADS = 8
PEER_KEYS = 128
PEER_TOPK = 16
PEER_HALF = 128
ROPE_THETA = 10000.0
LN_EPS = 1e-5
DEPTH = 1
ALPHA = (2 * DEPTH) ** 0.25
NEG_INF = -1e30

OFF_GU, OFF_GV = 0, GM_WIDTH
OFF_DQ = 2 * GM_WIDTH
OFF_DK = OFF_DQ + DIL_WIDTH
OFF_DV = OFF_DK + DIL_WIDTH
OFF_MQ = OFF_DV + DIL_WIDTH
IN_WIDTH = OFF_MQ + MEM_WIDTH

V7X_VMEM_BYTES = 64 * 1024 * 1024
VMEM_LIMIT = V7X_VMEM_BYTES - 8 * 1024 * 1024

NT_DIMS = (((1,), (1,)), ((), ()))


def _params(*sem):
    return pltpu.CompilerParams(dimension_semantics=sem, vmem_limit_bytes=VMEM_LIMIT)


def _mm_body(*refs, act, alpha, has_bias, has_res):
    x_ref, w_ref = refs[0], refs[1]
    k = 2
    b_ref = r_ref = None
    if has_bias:
        b_ref = refs[k]
        k += 1
    if has_res:
        r_ref = refs[k]
        k += 1
    o_ref, wb_ref = refs[k], refs[k + 1]

    @pl.when(pl.program_id(1) == 0)
    def _():
        wb_ref[...] = w_ref[...].astype(BF16)

    acc = jnp.dot(x_ref[...], wb_ref[...], preferred_element_type=F32)
    if has_bias:
        acc = acc + b_ref[...]
    if act == "sigmoid":
        acc = jax.nn.sigmoid(acc)
    if has_res:
        acc = alpha * r_ref[...] + acc
    o_ref[...] = acc.astype(o_ref.dtype)


def _matmul(x, w, *, out_dtype, tm, tn, bias=None, res=None, alpha=1.0, act=None, name):
    m, k = x.shape
    n = w.shape[1]
    tm, tn = min(tm, m), min(tn, n)
    assert m % tm == 0 and n % tn == 0
    in_specs = [pl.BlockSpec((tm, k), lambda j, i: (i, 0)),
                pl.BlockSpec((k, tn), lambda j, i: (0, j))]
    args = [x, w]
    if bias is not None:
        in_specs.append(pl.BlockSpec((1, tn), lambda j, i: (0, j)))
        args.append(bias.reshape(1, n))
    if res is not None:
        in_specs.append(pl.BlockSpec((tm, tn), lambda j, i: (i, j)))
        args.append(res)
    body = functools.partial(_mm_body, act=act, alpha=alpha, has_bias=bias is not None,
                             has_res=res is not None)
    return pl.pallas_call(
        body,
        grid=(n // tn, m // tm),
        in_specs=in_specs,
        out_specs=pl.BlockSpec((tm, tn), lambda j, i: (i, j)),
        out_shape=jax.ShapeDtypeStruct((m, n), out_dtype),
        scratch_shapes=[pltpu.VMEM((k, tn), BF16)],
        compiler_params=_params("arbitrary", "arbitrary"),
        name=name,
    )(*args)


def _ln_rows(z, g, b):
    mu = jnp.mean(z, axis=-1, keepdims=True)
    zc = z - mu
    var = jnp.mean(zc * zc, axis=-1, keepdims=True)
    return zc * lax.rsqrt(var + LN_EPS) * g + b


def _ln_body(*refs, alpha, has_add, want_t):
    k = 0
    z = refs[k][...]
    k += 1
    if has_add:
        z = alpha * z + refs[k][...]
        k += 1
    g_ref, b_ref = refs[k], refs[k + 1]
    k += 2
    h = _ln_rows(z, g_ref[...], b_ref[...])
    refs[k][...] = h
    if want_t:
        refs[k + 1][...] = h.astype(BF16)
        refs[k + 2][...] = h.T.astype(BF16)


def _layer_norm(z, g, b, *, add=None, alpha=1.0, want_t=False, tm, name):
    t, d = z.shape
    tm = min(tm, t)
    assert t % tm == 0
    row = pl.BlockSpec((tm, d), lambda i: (i, 0))
    vec = pl.BlockSpec((1, d), lambda i: (0, 0))
    in_specs, args = [row], [z]
    if add is not None:
        in_specs.append(row)
        args.append(add)
    in_specs += [vec, vec]
    args += [g.reshape(1, d), b.reshape(1, d)]
    out_specs, out_shape = [row], [jax.ShapeDtypeStruct((t, d), F32)]
    if want_t:
        out_specs += [row, pl.BlockSpec((d, tm), lambda i: (0, i))]
        out_shape += [jax.ShapeDtypeStruct((t, d), BF16), jax.ShapeDtypeStruct((d, t), BF16)]
    return pl.pallas_call(
        functools.partial(_ln_body, alpha=alpha, has_add=add is not None, want_t=want_t),
        grid=(t // tm,),
        in_specs=in_specs,
        out_specs=out_specs,
        out_shape=out_shape,
        compiler_params=_params("arbitrary"),
        name=name,
    )(*args)


def _gmlp_body(gu_ref, gv_ref, g_ref, b_ref, ws_ref, bs_ref, o_ref):
    tm = gu_ref.shape[0]
    u = jax.nn.gelu(gu_ref[...].astype(F32))
    v = _ln_rows(jax.nn.gelu(gv_ref[...].astype(F32)), g_ref[...], b_ref[...]).astype(BF16)
    causal = (lax.broadcasted_iota(jnp.int32, (BLOCK, BLOCK), 0)
              >= lax.broadcasted_iota(jnp.int32, (BLOCK, BLOCK), 1))
    for g in range(GM_GROUPS):
        cols = slice(g * 128, (g + 1) * 128)
        w = jnp.where(causal, ws_ref[g], 0.0).astype(BF16)
        for c in range(tm // BLOCK):
            rows = slice(c * BLOCK, (c + 1) * BLOCK)
            mixed = jnp.dot(w, v[rows, cols], preferred_element_type=F32) + bs_ref[g]
            o_ref[rows, cols] = (u[rows, cols] * mixed).astype(o_ref.dtype)


def _gmlp(proj, ln_g, ln_b, w_s, b_s, *, tm):
    t = proj.shape[0]
    assert t % tm == 0 and tm % BLOCK == 0
    bs_full = jnp.broadcast_to(b_s[:, :, None], (GM_GROUPS, BLOCK, 128))
    return pl.pallas_call(
        _gmlp_body,
        grid=(t // tm,),
        in_specs=[pl.BlockSpec((tm, GM_WIDTH), lambda i: (i, OFF_GU // GM_WIDTH)),
                  pl.BlockSpec((tm, GM_WIDTH), lambda i: (i, OFF_GV // GM_WIDTH)),
                  pl.BlockSpec((1, GM_WIDTH), lambda i: (0, 0)),
                  pl.BlockSpec((1, GM_WIDTH), lambda i: (0, 0)),
                  pl.BlockSpec((GM_GROUPS, BLOCK, BLOCK), lambda i: (0, 0, 0)),
                  pl.BlockSpec((GM_GROUPS, BLOCK, 128), lambda i: (0, 0, 0))],
        out_specs=pl.BlockSpec((tm, GM_WIDTH), lambda i: (i, 0)),
        out_shape=jax.ShapeDtypeStruct((t, GM_WIDTH), BF16),
        compiler_params=_params("arbitrary"),
        name="gmlp",
    )(proj, proj, ln_g.reshape(1, GM_WIDTH), ln_b.reshape(1, GM_WIDTH), w_s, bs_full)


def _rotary_body(q_ref, k_ref, pos_ref, inv_ref, qo_ref, ko_ref):
    half = HEAD_DIM // 2
    ang = pos_ref[...] * inv_ref[...]
    cos = jnp.cos(ang)
    lane = lax.broadcasted_iota(jnp.int32, ang.shape, 1)
    sin = jnp.where(lane < half, -jnp.sin(ang), jnp.sin(ang))
    for src, dst in ((q_ref, qo_ref), (k_ref, ko_ref)):
        for h in range(DIL_WIDTH // HEAD_DIM):
            cols = slice(h * HEAD_DIM, (h + 1) * HEAD_DIM)
            t = src[:, cols].astype(F32)
            swapped = jnp.concatenate([t[:, half:], t[:, :half]], axis=1)
            dst[:, cols] = (t * cos + swapped * sin).astype(dst.dtype)


def _rotary(proj, positions, *, tm):
    t = proj.shape[0]
    assert t % tm == 0
    half = HEAD_DIM // 2
    inv_freq = ROPE_THETA ** (-jnp.arange(half, dtype=F32) / half)
    inv = jnp.concatenate([inv_freq, inv_freq]).reshape(1, HEAD_DIM)
    pos = positions.astype(F32).reshape(t, 1)
    out = jax.ShapeDtypeStruct((t, DIL_WIDTH), BF16)
    return pl.pallas_call(
        _rotary_body,
        grid=(t // tm,),
        in_specs=[pl.BlockSpec((tm, DIL_WIDTH), lambda i: (i, OFF_DQ // DIL_WIDTH)),
                  pl.BlockSpec((tm, DIL_WIDTH), lambda i: (i, OFF_DK // DIL_WIDTH)),
                  pl.BlockSpec((tm, 1), lambda i: (i, 0)),
                  pl.BlockSpec((1, HEAD_DIM), lambda i: (0, 0))],
        out_specs=[pl.BlockSpec((tm, DIL_WIDTH), lambda i: (i, 0))] * 2,
        out_shape=[out, out],
        compiler_params=_params("arbitrary"),
        name="rotary",
    )(proj, proj, pos, inv)


def _dil_body(q_ref, kp_ref, kc_ref, vp_ref, vc_ref, o_ref, lse_ref):
    n = pl.program_id(2)
    qi = lax.broadcasted_iota(jnp.int32, (BLOCK, BLOCK), 0)
    ki = lax.broadcasted_iota(jnp.int32, (BLOCK, BLOCK), 1)
    mask_prev = ki >= qi + jnp.where(n > 0, 0, BLOCK)
    mask_cur = ki <= qi
    scale = HEAD_DIM ** -0.5
    for h in range(DIL_HEADS):
        cols = slice(h * HEAD_DIM, (h + 1) * HEAD_DIM)
        q = q_ref[:, cols]
        sp = lax.dot_general(q, kp_ref[:, cols], NT_DIMS, preferred_element_type=F32) * scale
        sc = lax.dot_general(q, kc_ref[:, cols], NT_DIMS, preferred_element_type=F32) * scale
        sp = jnp.where(mask_prev, sp, NEG_INF)
        sc = jnp.where(mask_cur, sc, NEG_INF)
        m = jnp.maximum(jnp.max(sp, axis=-1, keepdims=True), jnp.max(sc, axis=-1, keepdims=True))
        pp = jnp.exp(sp - m)
        pc = jnp.exp(sc - m)
        den = jnp.sum(pp, axis=-1, keepdims=True) + jnp.sum(pc, axis=-1, keepdims=True)
        o = (jnp.dot(pp.astype(BF16), vp_ref[:, cols], preferred_element_type=F32)
             + jnp.dot(pc.astype(BF16), vc_ref[:, cols], preferred_element_type=F32))
        o_ref[:, cols] = o / den
        lse_ref[:, cols] = jnp.broadcast_to(m + jnp.log(den), (BLOCK, HEAD_DIM))


def _dilated_attention(q_rot, k_rot, proj, group, dilation, batch, seq):
    d = dilation
    sub = seq // d
    nb = sub // BLOCK
    assert seq % d == 0 and sub % BLOCK == 0
    qv = q_rot.reshape(batch, sub, d * DIL_WIDTH)
    kv = k_rot.reshape(batch, sub, d * DIL_WIDTH)
    pv = proj.reshape(batch, sub, d * IN_WIDTH)
    gw = DIL_GROUP_WIDTH
    qk_per_tok, proj_per_tok = DIL_WIDTH // gw, IN_WIDTH // gw
    v_off = OFF_DV // gw + group
    blk = (None, BLOCK, gw)
    cur = lambda b, r, n: (b, n, r * qk_per_tok + group)
    prev = lambda b, r, n: (b, jnp.maximum(n - 1, 0), r * qk_per_tok + group)
    vcur = lambda b, r, n: (b, n, r * proj_per_tok + v_off)
    vprev = lambda b, r, n: (b, jnp.maximum(n - 1, 0), r * proj_per_tok + v_off)
    out = jax.ShapeDtypeStruct((batch, sub, d * gw), F32)
    o, lse = pl.pallas_call(
        _dil_body,
        grid=(batch, d, nb),
        in_specs=[pl.BlockSpec(blk, cur), pl.BlockSpec(blk, prev), pl.BlockSpec(blk, cur),
                  pl.BlockSpec(blk, vprev), pl.BlockSpec(blk, vcur)],
        out_specs=[pl.BlockSpec(blk, lambda b, r, n: (b, n, r))] * 2,
        out_shape=[out, out],
        compiler_params=_params("arbitrary", "arbitrary", "arbitrary"),
        name=f"dil_attn_d{d}",
    )(qv, kv, kv, pv, pv)
    return o.reshape(batch * seq, gw), lse.reshape(batch * seq, gw)


def _dil_merge_body(o0, o1, o2, l0, l1, l2, out_ref):
    m = jnp.maximum(jnp.maximum(l0[...], l1[...]), l2[...])
    e0, e1, e2 = jnp.exp(l0[...] - m), jnp.exp(l1[...] - m), jnp.exp(l2[...] - m)
    den = e0 + e1 + e2
    out_ref[...] = ((e0 * o0[...] + e1 * o1[...] + e2 * o2[...]) / den).astype(out_ref.dtype)


def _dil_merge(outs, lses, *, tm):
    t, w = outs[0].shape
    spec = pl.BlockSpec((tm, w), lambda i: (i, 0))
    return pl.pallas_call(
        _dil_merge_body,
        grid=(t // tm,),
        in_specs=[spec] * 6,
        out_specs=spec,
        out_shape=jax.ShapeDtypeStruct((t, w), BF16),
        compiler_params=_params("arbitrary"),
        name="dil_merge",
    )(*outs, *lses)


def _mem_body(q0, q1, q2, q3, kv_ref, o_ref):
    scale = MEM_HEAD_DIM ** -0.5
    for h, q_ref in enumerate((q0, q1, q2, q3)):
        cols = slice(h * MEM_HEAD_DIM, (h + 1) * MEM_HEAD_DIM)
        k = kv_ref[:, cols]
        v = kv_ref[:, MEM_WIDTH + h * MEM_HEAD_DIM:MEM_WIDTH + (h + 1) * MEM_HEAD_DIM]
        s = lax.dot_general(q_ref[...], k, NT_DIMS, preferred_element_type=F32) * scale
        m = jnp.max(s, axis=-1, keepdims=True)
        p = jnp.exp(s - m)
        den = jnp.sum(p, axis=-1, keepdims=True)
        o = jnp.dot(p.astype(BF16), v, preferred_element_type=F32) / den
        o_ref[:, cols] = o.astype(o_ref.dtype)


def _mem_attention(proj, mkv, batch, seq, *, tm):
    mem_len = mkv.shape[0] // batch
    assert seq % tm == 0
    steps = seq // tm
    q_specs = [pl.BlockSpec((tm, MEM_HEAD_DIM),
                            lambda b, i, h=h: (b * steps + i, OFF_MQ // MEM_HEAD_DIM + h))
               for h in range(MEM_HEADS)]
    return pl.pallas_call(
        _mem_body,
        grid=(batch, steps),
        in_specs=q_specs + [pl.BlockSpec((mem_len, 2 * MEM_WIDTH), lambda b, i: (b, 0))],
        out_specs=pl.BlockSpec((tm, MEM_WIDTH), lambda b, i: (b * steps + i, 0)),
        out_shape=jax.ShapeDtypeStruct((batch * seq, MEM_WIDTH), BF16),
        compiler_params=_params("arbitrary", "arbitrary"),
        name="mem_attn",
    )(proj, proj, proj, proj, mkv)


def _merge_body(a_ref, b_ref, m_ref, wa_ref, wb_ref, wm_ref, g0_ref, g1_ref, g2_ref, o_ref,
                wa_bf, wb_bf, wm_bf):
    @pl.when(pl.program_id(1) == 0)
    def _():
        wa_bf[...] = wa_ref[...].astype(BF16)
        wb_bf[...] = wb_ref[...].astype(BF16)
        wm_bf[...] = wm_ref[...].astype(BF16)

    ya = jnp.dot(a_ref[...], wa_bf[...], preferred_element_type=F32)
    yb = jnp.dot(b_ref[...], wb_bf[...], preferred_element_type=F32)
    ym = jnp.dot(m_ref[...], wm_bf[...], preferred_element_type=F32)
    o = (g0_ref[...].astype(F32) * ya + g1_ref[...].astype(F32) * yb
         + g2_ref[...].astype(F32) * ym)
    o_ref[...] = o.astype(o_ref.dtype)


def _merge(a_out, b_out, m_out, w_a, w_b, w_m, gates, *, tm, tn):
    t = a_out.shape[0]
    d = w_a.shape[1]
    assert t % tm == 0 and d % tn == 0
    nj = d // tn
    rows = lambda w: pl.BlockSpec((tm, w), lambda j, i: (i, 0))
    wcol = lambda k: pl.BlockSpec((k, tn), lambda j, i: (0, j))
    gate = lambda br: pl.BlockSpec((tm, tn), lambda j, i, br=br: (i, br * nj + j))
    return pl.pallas_call(
        _merge_body,
        grid=(nj, t // tm),
        in_specs=[rows(GM_WIDTH), rows(DIL_GROUP_WIDTH), rows(MEM_WIDTH),
                  wcol(GM_WIDTH), wcol(DIL_GROUP_WIDTH), wcol(MEM_WIDTH),
                  gate(0), gate(1), gate(2)],
        out_specs=pl.BlockSpec((tm, tn), lambda j, i: (i, j)),
        out_shape=jax.ShapeDtypeStruct((t, d), BF16),
        scratch_shapes=[pltpu.VMEM((GM_WIDTH, tn), BF16), pltpu.VMEM((DIL_GROUP_WIDTH, tn), BF16),
                        pltpu.VMEM((MEM_WIDTH, tn), BF16)],
        compiler_params=_params("arbitrary", "arbitrary"),
        name="branch_merge",
    )(a_out, b_out, m_out, w_a, w_b, w_m, gates, gates, gates)


def _peer_route_body(q_ref, keys_ref, st_ref, et_ref, tau_ref):
    tm = q_ref.shape[0]
    k = PEER_TOPK
    row = lax.broadcasted_iota(jnp.int32, (PEER_KEYS, tm), 0)
    neg = -jnp.inf
    for h in range(PEER_HEADS):
        tops = []
        for p in range(2):
            hp = 2 * h + p
            qs = q_ref[:, hp * PEER_HALF:(hp + 1) * PEER_HALF]
            st = lax.dot_general(keys_ref[hp], qs, NT_DIMS, preferred_element_type=F32,
                                 precision=lax.Precision.HIGHEST)
            st_ref[hp] = st
            x = st
            vals = []
            for _ in range(k):
                m = jnp.max(x, axis=0, keepdims=True)
                first = jnp.min(jnp.where(x == m, row, PEER_KEYS), axis=0, keepdims=True)
                x = jnp.where(row == first, neg, x)
                vals.append(m)
            tops.append(jnp.concatenate(vals, axis=0))
        a1, a2 = tops
        m1, m2 = a1[0:1], a2[0:1]
        e1, e2 = jnp.exp(a1 - m1), jnp.exp(a2 - m2)
        cand = jnp.concatenate([a1[a:a + 1] + a2 for a in range(k)], axis=0)
        prob = jnp.concatenate([e1[a:a + 1] * e2 for a in range(k)], axis=0)
        x = cand
        left = jnp.full((1, tm), float(k), F32)
        tau = jnp.full((1, tm), jnp.inf, F32)
        for _ in range(k):
            m = jnp.max(x, axis=0, keepdims=True)
            eq = x == m
            tau = jnp.where(left > 0, m, tau)
            left = left - jnp.sum(jnp.where(eq, 1.0, 0.0), axis=0, keepdims=True)
            x = jnp.where(eq, neg, x)
        z = jnp.sum(jnp.where(cand >= tau, prob, 0.0), axis=0, keepdims=True)
        et_ref[2 * h] = jnp.exp(st_ref[2 * h] - m1) / z
        et_ref[2 * h + 1] = jnp.exp(st_ref[2 * h + 1] - m2)
        tau_ref[h:h + 1, :] = tau


def _peer_route(q, sub_keys, *, tm):
    t = q.shape[0]
    assert t % tm == 0
    hp = 2 * PEER_HEADS
    keys = sub_keys.reshape(hp, PEER_KEYS, PEER_HALF)
    big = jax.ShapeDtypeStruct((hp, PEER_KEYS, t), F32)
    big_spec = pl.BlockSpec((hp, PEER_KEYS, tm), lambda i: (0, 0, i))
    return pl.pallas_call(
        _peer_route_body,
        grid=(t // tm,),
        in_specs=[pl.BlockSpec((tm, hp * PEER_HALF), lambda i: (i, 0)),
                  pl.BlockSpec((hp, PEER_KEYS, PEER_HALF), lambda i: (0, 0, 0))],
        out_specs=[big_spec, big_spec, pl.BlockSpec((PEER_HEADS, tm), lambda i: (0, i))],
        out_shape=[big, big, jax.ShapeDtypeStruct((PEER_HEADS, t), F32)],
        compiler_params=_params("arbitrary"),
        name="peer_route",
    )(q, keys)


def _peer_dense_body(xt_ref, u_ref, v_ref, st_ref, et_ref, tau_ref, o_ref):
    e = pl.program_id(1)
    te = u_ref.shape[0]

    @pl.when(e == 0)
    def _():
        o_ref[...] = jnp.zeros_like(o_ref)

    act_t = jnp.dot(u_ref[...], xt_ref[...], preferred_element_type=F32)
    blocks = []
    for r in range(te // PEER_KEYS):
        i1 = e * (te // PEER_KEYS) + r
        gate = None
        for h in range(PEER_HEADS):
            s = st_ref[2 * h, pl.ds(i1, 1), :] + st_ref[2 * h + 1]
            p = et_ref[2 * h, pl.ds(i1, 1), :] * et_ref[2 * h + 1]
            sel = jnp.where(s >= tau_ref[h:h + 1, :], p, 0.0)
            gate = sel if gate is None else gate + sel
        rows = slice(r * PEER_KEYS, (r + 1) * PEER_KEYS)
        blocks.append(jax.nn.gelu(act_t[rows]) * gate)
    w_t = jnp.concatenate(blocks, axis=0)
    o_ref[...] += jnp.dot(w_t.T.astype(BF16), v_ref[...], preferred_element_type=F32)


def _peer_dense(h_t, u, v, st, et, tau, *, tm, te):
    d, t = h_t.shape
    n_exp = u.shape[0]
    hp = 2 * PEER_HEADS
    assert t % tm == 0 and n_exp % te == 0 and te % PEER_KEYS == 0
    once = dict(pipeline_mode=pl.Buffered(1))
    return pl.pallas_call(
        _peer_dense_body,
        grid=(t // tm, n_exp // te),
        in_specs=[pl.BlockSpec((d, tm), lambda i, e: (0, i), **once),
                  pl.BlockSpec((te, d), lambda i, e: (e, 0)),
                  pl.BlockSpec((te, d), lambda i, e: (e, 0)),
                  pl.BlockSpec((hp, PEER_KEYS, tm), lambda i, e: (0, 0, i), **once),
                  pl.BlockSpec((hp, PEER_KEYS, tm), lambda i, e: (0, 0, i), **once),
                  pl.BlockSpec((PEER_HEADS, tm), lambda i, e: (0, i))],
        out_specs=pl.BlockSpec((tm, d), lambda i, e: (i, 0)),
        out_shape=jax.ShapeDtypeStruct((t, d), F32),
        compiler_params=_params("arbitrary", "arbitrary"),
        name="peer_dense",
    )(h_t, u, v, st, et, tau)


MM_TM, MM_TN = 512, 512
ROW_TM = 256
PEER_TM, PEER_TE = 512, 512


def kernel(x, mem, positions, w_in, gm_ln_g, gm_ln_b, gm_w_s, gm_b_s, w_mem_kv, w_gate, b_gate,
           w_br_gmlp, w_br_dil, w_br_mem, w_out, ln_mix_g, ln_mix_b, peer_w_query, peer_sub_keys,
           peer_u, peer_v, ln_ffn_g, ln_ffn_b):
    batch, seq, d = x.shape
    t = batch * seq
    assert w_in.shape[0] == DEPTH and w_in.shape[2] == IN_WIDTH
    x2 = x.reshape(t, d)
    x_bf = x2.astype(BF16)
    mem_bf = mem.reshape(-1, d).astype(BF16)

    proj = _matmul(x_bf, w_in.reshape(d, IN_WIDTH), out_dtype=BF16, tm=MM_TM, tn=MM_TN,
                   name="in_proj")
    gates = _matmul(x_bf, w_gate.reshape(d, 3 * d), bias=b_gate.reshape(3 * d), act="sigmoid",
                    out_dtype=BF16, tm=MM_TM, tn=MM_TN, name="gate_proj")
    mkv = _matmul(mem_bf, w_mem_kv.reshape(d, 2 * MEM_WIDTH), out_dtype=BF16, tm=MM_TM, tn=MM_TN,
                  name="mem_kv_proj")

    a_out = _gmlp(proj, gm_ln_g.reshape(-1), gm_ln_b.reshape(-1), gm_w_s.reshape(GM_GROUPS, BLOCK, BLOCK),
                  gm_b_s.reshape(GM_GROUPS, BLOCK), tm=512)

    q_rot, k_rot = _rotary(proj, positions, tm=512)
    outs, lses = [], []
    for g, dil in enumerate(DILATIONS):
        o, lse = _dilated_attention(q_rot, k_rot, proj, g, dil, batch, seq)
        outs.append(o)
        lses.append(lse)
    b_out = _dil_merge(outs, lses, tm=1024)

    m_out = _mem_attention(proj, mkv, batch, seq, tm=512)

    merged = _merge(a_out, b_out, m_out, w_br_gmlp.reshape(GM_WIDTH, d),
                    w_br_dil.reshape(DIL_GROUP_WIDTH, d), w_br_mem.reshape(MEM_WIDTH, d), gates,
                    tm=MM_TM, tn=MM_TN)
    z1 = _matmul(merged, w_out.reshape(d, d), res=x2, alpha=ALPHA, out_dtype=F32, tm=MM_TM,
                 tn=MM_TN, name="out_proj")
    h1, h1_bf, h1_t = _layer_norm(z1, ln_mix_g.reshape(-1), ln_mix_b.reshape(-1), want_t=True,
                                  tm=ROW_TM, name="ln_mix")

    hq = PEER_HEADS * 2 * PEER_HALF
    q = _matmul(h1_bf, peer_w_query.reshape(d, hq), out_dtype=F32, tm=MM_TM, tn=MM_TN,
                name="peer_query")
    st, et, tau = _peer_route(q, peer_sub_keys, tm=256)
    n_exp = PEER_KEYS * PEER_KEYS
    y = _peer_dense(h1_t, peer_u.reshape(n_exp, d).astype(BF16), peer_v.reshape(n_exp, d).astype(BF16),
                    st, et, tau, tm=PEER_TM, te=PEER_TE)
    (out,) = _layer_norm(h1, ln_ffn_g.reshape(-1), ln_ffn_b.reshape(-1), add=y, alpha=ALPHA,
                         tm=ROW_TM, name="ln_ffn")
    return out.reshape(batch, seq, d)
```

```python
import functools

import jax
import jax.numpy as jnp
from jax import lax
from jax.experimental import pallas as pl
from jax.experimental.pallas import tpu as pltpu

F32 = jnp.float32
BF16 = jnp.bfloat16

HEAD_DIM = 128
BLOCK = 128
GM_GROUPS = 12
GM_WIDTH = GM_GROUPS * 128
DILATIONS = (1, 4, 16)
DIL_HEADS = 4
DIL_GROUP_WIDTH = DIL_HEADS * HEAD_DIM
DIL_WIDTH = len(DILATIONS) * DIL_GROUP_WIDTH
MEM_HEADS = 4
MEM_HEAD_DIM = 256
MEM_WIDTH = MEM_HEADS * MEM_HEAD_DIM
PEER_HEADS = 8
PEER_KEYS = 128
PEER_TOPK = 16
PEER_HALF = 128
ROPE_THETA = 10000.0
LN_EPS = 1e-5
DEPTH = 1
ALPHA = (2 * DEPTH) ** 0.25
NEG_INF = -1e30

OFF_GU, OFF_GV = 0, GM_WIDTH
OFF_DQ = 2 * GM_WIDTH
OFF_DK = OFF_DQ + DIL_WIDTH
OFF_DV = OFF_DK + DIL_WIDTH
OFF_MQ = OFF_DV + DIL_WIDTH
IN_WIDTH = OFF_MQ + MEM_WIDTH

V7X_VMEM_BYTES = 64 * 1024 * 1024
VMEM_LIMIT = V7X_VMEM_BYTES - 8 * 1024 * 1024

NT_DIMS = (((1,), (1,)), ((), ()))


def _params(*sem):
    return pltpu.CompilerParams(dimension_semantics=sem, vmem_limit_bytes=VMEM_LIMIT)


def _mm_body(*refs, act, alpha, has_bias, has_res):
    x_ref, w_ref = refs[0], refs[1]
    k = 2
    b_ref = r_ref = None
    if has_bias:
        b_ref = refs[k]
        k += 1
    if has_res:
        r_ref = refs[k]
        k += 1
    o_ref, wb_ref = refs[k], refs[k + 1]

    @pl.when(pl.program_id(1) == 0)
    def _():
        wb_ref[...] = w_ref[...].astype(BF16)

    acc = jnp.dot(x_ref[...], wb_ref[...], preferred_element_type=F32)
    if has_bias:
        acc = acc + b_ref[...]
    if act == "sigmoid":
        acc = jax.nn.sigmoid(acc)
    if has_res:
        acc = alpha * r_ref[...] + acc
    o_ref[...] = acc.astype(o_ref.dtype)


def _matmul(x, w, *, out_dtype, tm, tn, bias=None, res=None, alpha=1.0, act=None, name):
    m, k = x.shape
    n = w.shape[1]
    tm, tn = min(tm, m), min(tn, n)
    assert m % tm == 0 and n % tn == 0
    in_specs = [pl.BlockSpec((tm, k), lambda j, i: (i, 0)),
                pl.BlockSpec((k, tn), lambda j, i: (0, j))]
    args = [x, w]
    if bias is not None:
        in_specs.append(pl.BlockSpec((1, tn), lambda j, i: (0, j)))
        args.append(bias.reshape(1, n))
    if res is not None:
        in_specs.append(pl.BlockSpec((tm, tn), lambda j, i: (i, j)))
        args.append(res)
    body = functools.partial(_mm_body, act=act, alpha=alpha, has_bias=bias is not None,
                             has_res=res is not None)
    return pl.pallas_call(
        body,
        grid=(n // tn, m // tm),
        in_specs=in_specs,
        out_specs=pl.BlockSpec((tm, tn), lambda j, i: (i, j)),
        out_shape=jax.ShapeDtypeStruct((m, n), out_dtype),
        scratch_shapes=[pltpu.VMEM((k, tn), BF16)],
        compiler_params=_params("arbitrary", "arbitrary"),
        name=name,
    )(*args)


def _ln_rows(z, g, b):
    mu = jnp.mean(z, axis=-1, keepdims=True)
    zc = z - mu
    var = jnp.mean(zc * zc, axis=-1, keepdims=True)
    return zc * lax.rsqrt(var + LN_EPS) * g + b


def _ln_body(*refs, alpha, has_add, want_t):
    k = 0
    z = refs[k][...]
    k += 1
    if has_add:
        z = alpha * z + refs[k][...]
        k += 1
    g_ref, b_ref = refs[k], refs[k + 1]
    k += 2
    h = _ln_rows(z, g_ref[...], b_ref[...])
    refs[k][...] = h
    if want_t:
        refs[k + 1][...] = h.astype(BF16)
        refs[k + 2][...] = h.T.astype(BF16)


def _layer_norm(z, g, b, *, add=None, alpha=1.0, want_t=False, tm, name):
    t, d = z.shape
    tm = min(tm, t)
    assert t % tm == 0
    row = pl.BlockSpec((tm, d), lambda i: (i, 0))
    vec = pl.BlockSpec((1, d), lambda i: (0, 0))
    in_specs, args = [row], [z]
    if add is not None:
        in_specs.append(row)
        args.append(add)
    in_specs += [vec, vec]
    args += [g.reshape(1, d), b.reshape(1, d)]
    out_specs, out_shape = [row], [jax.ShapeDtypeStruct((t, d), F32)]
    if want_t:
        out_specs += [row, pl.BlockSpec((d, tm), lambda i: (0, i))]
        out_shape += [jax.ShapeDtypeStruct((t, d), BF16), jax.ShapeDtypeStruct((d, t), BF16)]
    return pl.pallas_call(
        functools.partial(_ln_body, alpha=alpha, has_add=add is not None, want_t=want_t),
        grid=(t // tm,),
        in_specs=in_specs,
        out_specs=out_specs,
        out_shape=out_shape,
        compiler_params=_params("arbitrary"),
        name=name,
    )(*args)


def _gmlp_body(gu_ref, gv_ref, g_ref, b_ref, ws_ref, bs_ref, o_ref):
    tm = gu_ref.shape[0]
    u = jax.nn.gelu(gu_ref[...].astype(F32))
    v = _ln_rows(jax.nn.gelu(gv_ref[...].astype(F32)), g_ref[...], b_ref[...]).astype(BF16)
    causal = (lax.broadcasted_iota(jnp.int32, (BLOCK, BLOCK), 0)
              >= lax.broadcasted_iota(jnp.int32, (BLOCK, BLOCK), 1))
    for g in range(GM_GROUPS):
        cols = slice(g * 128, (g + 1) * 128)
        w = jnp.where(causal, ws_ref[g], 0.0).astype(BF16)
        for c in range(tm // BLOCK):
            rows = slice(c * BLOCK, (c + 1) * BLOCK)
            mixed = jnp.dot(w, v[rows, cols], preferred_element_type=F32) + bs_ref[g]
            o_ref[rows, cols] = (u[rows, cols] * mixed).astype(o_ref.dtype)


def _gmlp(proj, ln_g, ln_b, w_s, b_s, *, tm):
    t = proj.shape[0]
    assert t % tm == 0 and tm % BLOCK == 0
    bs_full = jnp.broadcast_to(b_s[:, :, None], (GM_GROUPS, BLOCK, 128))
    return pl.pallas_call(
        _gmlp_body,
        grid=(t // tm,),
        in_specs=[pl.BlockSpec((tm, GM_WIDTH), lambda i: (i, OFF_GU // GM_WIDTH)),
                  pl.BlockSpec((tm, GM_WIDTH), lambda i: (i, OFF_GV // GM_WIDTH)),
                  pl.BlockSpec((1, GM_WIDTH), lambda i: (0, 0)),
                  pl.BlockSpec((1, GM_WIDTH), lambda i: (0, 0)),
                  pl.BlockSpec((GM_GROUPS, BLOCK, BLOCK), lambda i: (0, 0, 0)),
                  pl.BlockSpec((GM_GROUPS, BLOCK, 128), lambda i: (0, 0, 0))],
        out_specs=pl.BlockSpec((tm, GM_WIDTH), lambda i: (i, 0)),
        out_shape=jax.ShapeDtypeStruct((t, GM_WIDTH), BF16),
        compiler_params=_params("arbitrary"),
        name="gmlp",
    )(proj, proj, ln_g.reshape(1, GM_WIDTH), ln_b.reshape(1, GM_WIDTH), w_s, bs_full)


PERM_TILE = BLOCK * max(DILATIONS)


def _rope_table_body(pos_ref, inv_ref, cos_ref, sin_ref):
    half = HEAD_DIM // 2
    ang = pos_ref[...] * inv_ref[...]
    lane = lax.broadcasted_iota(jnp.int32, ang.shape, 1)
    sin = jnp.sin(ang)
    cos_ref[...] = jnp.cos(ang)
    sin_ref[...] = jnp.where(lane < half, -sin, sin)


def _rope_table(positions, *, tm):
    t = positions.size
    assert t % tm == 0
    half = HEAD_DIM // 2
    inv_freq = ROPE_THETA ** (-jnp.arange(half, dtype=F32) / half)
    inv = jnp.concatenate([inv_freq, inv_freq]).reshape(1, HEAD_DIM)
    pos = positions.astype(F32).reshape(t, 1)
    out = jax.ShapeDtypeStruct((t, HEAD_DIM), F32)
    return pl.pallas_call(
        _rope_table_body,
        grid=(t // tm,),
        in_specs=[pl.BlockSpec((tm, 1), lambda i: (i, 0)),
                  pl.BlockSpec((1, HEAD_DIM), lambda i: (0, 0))],
        out_specs=[pl.BlockSpec((tm, HEAD_DIM), lambda i: (i, 0))] * 2,
        out_shape=[out, out],
        compiler_params=_params("arbitrary"),
        name="rope_table",
    )(pos, inv)


def _rope_perm_body(q_ref, k_ref, v_ref, cos_ref, sin_ref, qo_ref, ko_ref, vo_ref, buf_ref, *, d):
    tm = q_ref.shape[0]
    half = HEAD_DIM // 2
    span = BLOCK * d
    for h in range(DIL_HEADS):
        cols = slice(h * HEAD_DIM, (h + 1) * HEAD_DIM)
        for src, dst, rotate in ((q_ref, qo_ref, True), (k_ref, ko_ref, True), (v_ref, vo_ref, False)):
            t = src[:, cols].astype(F32)
            if rotate:
                swapped = jnp.concatenate([t[:, half:], t[:, :half]], axis=1)
                t = t * cos_ref[...] + swapped * sin_ref[...]
            if d == 1:
                dst[:, cols] = t.astype(dst.dtype)
                continue
            buf_ref[...] = t
            for n in range(tm // span):
                for r in range(d):
                    rows = buf_ref[pl.ds(n * span + r, BLOCK, stride=d), :]
                    dst[n * span + r * BLOCK:n * span + (r + 1) * BLOCK, cols] = rows.astype(dst.dtype)


def _rope_perm(proj, cos, sin, group, d):
    t = proj.shape[0]
    tm, gw = PERM_TILE, DIL_GROUP_WIDTH
    assert t % tm == 0 and tm % (BLOCK * d) == 0
    col = lambda off: pl.BlockSpec((tm, gw), lambda i: (i, off // gw + group))
    tab = pl.BlockSpec((tm, HEAD_DIM), lambda i: (i, 0))
    out = jax.ShapeDtypeStruct((t, gw), BF16)
    return pl.pallas_call(
        functools.partial(_rope_perm_body, d=d),
        grid=(t // tm,),
        in_specs=[col(OFF_DQ), col(OFF_DK), col(OFF_DV), tab, tab],
        out_specs=[pl.BlockSpec((tm, gw), lambda i: (i, 0))] * 3,
        out_shape=[out, out, out],
        scratch_shapes=[pltpu.VMEM((tm, HEAD_DIM), F32)],
        compiler_params=_params("arbitrary"),
        name=f"rope_perm_d{d}",
    )(proj, proj, proj, cos, sin)


def _dil_body(q_ref, kc_ref, vc_ref, kp_ref, vp_ref, o_ref, lse_ref, *, d, nblk):
    step = pl.program_id(1)
    qi = lax.broadcasted_iota(jnp.int32, (BLOCK, BLOCK), 0)
    ki = lax.broadcasted_iota(jnp.int32, (BLOCK, BLOCK), 1)
    mask_cur = ki <= qi
    scale = HEAD_DIM ** -0.5
    for i in range(nblk):
        rows = slice(i * BLOCK, (i + 1) * BLOCK)
        if d >= nblk:
            kpr, vpr, prow, has_prev = kp_ref, vp_ref, rows, step >= d // nblk
        elif i == 0:
            kpr, vpr, prow, has_prev = kp_ref, vp_ref, slice(0, BLOCK), step > 0
        else:
            kpr, vpr, prow, has_prev = kc_ref, vc_ref, slice((i - 1) * BLOCK, i * BLOCK), None
        mask_prev = ki >= (qi if has_prev is None else qi + jnp.where(has_prev, 0, BLOCK))
        for h in range(DIL_HEADS):
            cols = slice(h * HEAD_DIM, (h + 1) * HEAD_DIM)
            q = q_ref[rows, cols]
            sp = lax.dot_general(q, kpr[prow, cols], NT_DIMS, preferred_element_type=F32) * scale
            sc = lax.dot_general(q, kc_ref[rows, cols], NT_DIMS, preferred_element_type=F32) * scale
            sp = jnp.where(mask_prev, sp, NEG_INF)
            sc = jnp.where(mask_cur, sc, NEG_INF)
            m = jnp.maximum(jnp.max(sp, axis=-1, keepdims=True), jnp.max(sc, axis=-1, keepdims=True))
            pp = jnp.exp(sp - m)
            pc = jnp.exp(sc - m)
            den = jnp.sum(pp, axis=-1, keepdims=True) + jnp.sum(pc, axis=-1, keepdims=True)
            o = (jnp.dot(pp.astype(BF16), vpr[prow, cols], preferred_element_type=F32)
                 + jnp.dot(pc.astype(BF16), vc_ref[rows, cols], preferred_element_type=F32))
            o_ref[rows, cols] = o / den
            lse_ref[rows, cols] = jnp.broadcast_to(m + jnp.log(den), (BLOCK, HEAD_DIM))


def _dilated_attention(q, k, v, d, batch, seq, *, nblk):
    gw = DIL_GROUP_WIDTH
    rows = nblk * BLOCK
    assert seq % rows == 0 and (d == 1 or d % nblk == 0)
    steps = seq // rows
    cur = pl.BlockSpec((rows, gw), lambda b, s: (b * steps + s, 0))
    if d >= nblk:
        back = d // nblk
        prev = pl.BlockSpec((rows, gw), lambda b, s: (b * steps + jnp.maximum(s - back, 0), 0))
    else:
        prev = pl.BlockSpec((BLOCK, gw),
                            lambda b, s: (b * steps * nblk + jnp.maximum(s * nblk - 1, 0), 0))
    out = jax.ShapeDtypeStruct((batch * seq, gw), F32)
    return pl.pallas_call(
        functools.partial(_dil_body, d=d, nblk=nblk),
        grid=(batch, steps),
        in_specs=[cur, cur, cur, prev, prev],
        out_specs=[cur, cur],
        out_shape=[out, out],
        compiler_params=_params("arbitrary", "arbitrary"),
        name=f"dil_attn_d{d}",
    )(q, k, v, k, v)


def _dil_merge_body(*refs):
    n = len(DILATIONS)
    o_refs, l_refs, out_ref, bufs = refs[:n], refs[n:2 * n], refs[2 * n], refs[2 * n + 1:]
    tm = out_ref.shape[0]
    outs, lses = [], []
    k = 0
    for d, o_ref, l_ref in zip(DILATIONS, o_refs, l_refs):
        if d == 1:
            outs.append(o_ref[...])
            lses.append(l_ref[...])
            continue
        span = BLOCK * d
        for src, buf in ((o_ref, bufs[k]), (l_ref, bufs[k + 1])):
            for s in range(tm // span):
                for r in range(d):
                    buf[pl.ds(s * span + r, BLOCK, stride=d), :] = (
                        src[s * span + r * BLOCK:s * span + (r + 1) * BLOCK, :])
        outs.append(bufs[k][...])
        lses.append(bufs[k + 1][...])
        k += 2
    m = functools.reduce(jnp.maximum, lses)
    es = [jnp.exp(l - m) for l in lses]
    num = sum(e * o for e, o in zip(es, outs))
    out_ref[...] = (num / sum(es)).astype(out_ref.dtype)


def _dil_merge(outs, lses):
    t, w = outs[0].shape
    tm = PERM_TILE
    assert t % tm == 0
    spec = pl.BlockSpec((tm, HEAD_DIM), lambda i, h: (i, h))
    n_buf = 2 * sum(1 for d in DILATIONS if d != 1)
    return pl.pallas_call(
        _dil_merge_body,
        grid=(t // tm, w // HEAD_DIM),
        in_specs=[spec] * (2 * len(DILATIONS)),
        out_specs=spec,
        out_shape=jax.ShapeDtypeStruct((t, w), BF16),
        scratch_shapes=[pltpu.VMEM((tm, HEAD_DIM), F32)] * n_buf,
        compiler_params=_params("arbitrary", "arbitrary"),
        name="dil_merge",
    )(*outs, *lses)


def _mem_body(q0, q1, q2, q3, kv_ref, o_ref):
    scale = MEM_HEAD_DIM ** -0.5
    for h, q_ref in enumerate((q0, q1, q2, q3)):
        cols = slice(h * MEM_HEAD_DIM, (h + 1) * MEM_HEAD_DIM)
        k = kv_ref[:, cols]
        v = kv_ref[:, MEM_WIDTH + h * MEM_HEAD_DIM:MEM_WIDTH + (h + 1) * MEM_HEAD_DIM]
        s = lax.dot_general(q_ref[...], k, NT_DIMS, preferred_element_type=F32) * scale
        m = jnp.max(s, axis=-1, keepdims=True)
        p = jnp.exp(s - m)
        den = jnp.sum(p, axis=-1, keepdims=True)
        o = jnp.dot(p.astype(BF16), v, preferred_element_type=F32) / den
        o_ref[:, cols] = o.astype(o_ref.dtype)


def _mem_attention(proj, mkv, batch, seq, *, tm):
    mem_len = mkv.shape[0] // batch
    assert seq % tm == 0
    steps = seq // tm
    q_specs = [pl.BlockSpec((tm, MEM_HEAD_DIM),
                            lambda b, i, h=h: (b * steps + i, OFF_MQ // MEM_HEAD_DIM + h))
               for h in range(MEM_HEADS)]
    return pl.pallas_call(
        _mem_body,
        grid=(batch, steps),
        in_specs=q_specs + [pl.BlockSpec((mem_len, 2 * MEM_WIDTH), lambda b, i: (b, 0))],
        out_specs=pl.BlockSpec((tm, MEM_WIDTH), lambda b, i: (b * steps + i, 0)),
        out_shape=jax.ShapeDtypeStruct((batch * seq, MEM_WIDTH), BF16),
        compiler_params=_params("arbitrary", "arbitrary"),
        name="mem_attn",
    )(proj, proj, proj, proj, mkv)


def _merge_body(a_ref, b_ref, m_ref, wa_ref, wb_ref, wm_ref, g0_ref, g1_ref, g2_ref, o_ref,
                wa_bf, wb_bf, wm_bf):
    @pl.when(pl.program_id(1) == 0)
    def _():
        wa_bf[...] = wa_ref[...].astype(BF16)
        wb_bf[...] = wb_ref[...].astype(BF16)
        wm_bf[...] = wm_ref[...].astype(BF16)

    ya = jnp.dot(a_ref[...], wa_bf[...], preferred_element_type=F32)
    yb = jnp.dot(b_ref[...], wb_bf[...], preferred_element_type=F32)
    ym = jnp.dot(m_ref[...], wm_bf[...], preferred_element_type=F32)
    o = (g0_ref[...].astype(F32) * ya + g1_ref[...].astype(F32) * yb
         + g2_ref[...].astype(F32) * ym)
    o_ref[...] = o.astype(o_ref.dtype)


def _merge(a_out, b_out, m_out, w_a, w_b, w_m, gates, *, tm, tn):
    t = a_out.shape[0]
    d = w_a.shape[1]
    assert t % tm == 0 and d % tn == 0
    nj = d // tn
    rows = lambda w: pl.BlockSpec((tm, w), lambda j, i: (i, 0))
    wcol = lambda k: pl.BlockSpec((k, tn), lambda j, i: (0, j))
    gate = lambda br: pl.BlockSpec((tm, tn), lambda j, i, br=br: (i, br * nj + j))
    return pl.pallas_call(
        _merge_body,
        grid=(nj, t // tm),
        in_specs=[rows(GM_WIDTH), rows(DIL_GROUP_WIDTH), rows(MEM_WIDTH),
                  wcol(GM_WIDTH), wcol(DIL_GROUP_WIDTH), wcol(MEM_WIDTH),
                  gate(0), gate(1), gate(2)],
        out_specs=pl.BlockSpec((tm, tn), lambda j, i: (i, j)),
        out_shape=jax.ShapeDtypeStruct((t, d), BF16),
        scratch_shapes=[pltpu.VMEM((GM_WIDTH, tn), BF16), pltpu.VMEM((DIL_GROUP_WIDTH, tn), BF16),
                        pltpu.VMEM((MEM_WIDTH, tn), BF16)],
        compiler_params=_params("arbitrary", "arbitrary"),
        name="branch_merge",
    )(a_out, b_out, m_out, w_a, w_b, w_m, gates, gates, gates)


def _peer_route_body(q_ref, keys_ref, st_ref, et_ref, tau_ref):
    tm = q_ref.shape[0]
    k = PEER_TOPK
    row = lax.broadcasted_iota(jnp.int32, (PEER_KEYS, tm), 0)
    neg = -jnp.inf
    for h in range(PEER_HEADS):
        tops = []
        for p in range(2):
            hp = 2 * h + p
            qs = q_ref[:, hp * PEER_HALF:(hp + 1) * PEER_HALF]
            st = lax.dot_general(keys_ref[hp], qs, NT_DIMS, preferred_element_type=F32,
                                 precision=lax.Precision.HIGHEST)
            st_ref[hp] = st
            x = st
            vals = []
            for _ in range(k):
                m = jnp.max(x, axis=0, keepdims=True)
                first = jnp.min(jnp.where(x == m, row, PEER_KEYS), axis=0, keepdims=True)
                x = jnp.where(row == first, neg, x)
                vals.append(m)
            tops.append(jnp.concatenate(vals, axis=0))
        a1, a2 = tops
        m1, m2 = a1[0:1], a2[0:1]
        e1, e2 = jnp.exp(a1 - m1), jnp.exp(a2 - m2)
        cand = jnp.concatenate([a1[a:a + 1] + a2 for a in range(k)], axis=0)
        prob = jnp.concatenate([e1[a:a + 1] * e2 for a in range(k)], axis=0)
        x = cand
        left = jnp.full((1, tm), float(k), F32)
        tau = jnp.full((1, tm), jnp.inf, F32)
        for _ in range(k):
            m = jnp.max(x, axis=0, keepdims=True)
            eq = x == m
            tau = jnp.where(left > 0, m, tau)
            left = left - jnp.sum(jnp.where(eq, 1.0, 0.0), axis=0, keepdims=True)
            x = jnp.where(eq, neg, x)
        z = jnp.sum(jnp.where(cand >= tau, prob, 0.0), axis=0, keepdims=True)
        et_ref[2 * h] = jnp.exp(st_ref[2 * h] - m1) / z
        et_ref[2 * h + 1] = jnp.exp(st_ref[2 * h + 1] - m2)
        tau_ref[h:h + 1, :] = tau


def _peer_route(q, sub_keys, *, tm):
    t = q.shape[0]
    assert t % tm == 0
    hp = 2 * PEER_HEADS
    keys = sub_keys.reshape(hp, PEER_KEYS, PEER_HALF)
    big = jax.ShapeDtypeStruct((hp, PEER_KEYS, t), F32)
    big_spec = pl.BlockSpec((hp, PEER_KEYS, tm), lambda i: (0, 0, i))
    return pl.pallas_call(
        _peer_route_body,
        grid=(t // tm,),
        in_specs=[pl.BlockSpec((tm, hp * PEER_HALF), lambda i: (i, 0)),
                  pl.BlockSpec((hp, PEER_KEYS, PEER_HALF), lambda i: (0, 0, 0))],
        out_specs=[big_spec, big_spec, pl.BlockSpec((PEER_HEADS, tm), lambda i: (0, i))],
        out_shape=[big, big, jax.ShapeDtypeStruct((PEER_HEADS, t), F32)],
        compiler_params=_params("arbitrary"),
        name="peer_route",
    )(q, keys)


def _peer_dense_body(xt_ref, u_ref, v_ref, st_ref, et_ref, tau_ref, o_ref):
    e = pl.program_id(1)
    te = u_ref.shape[0]

    @pl.when(e == 0)
    def _():
        o_ref[...] = jnp.zeros_like(o_ref)

    act_t = jnp.dot(u_ref[...], xt_ref[...], preferred_element_type=F32)
    blocks = []
    for r in range(te // PEER_KEYS):
        i1 = e * (te // PEER_KEYS) + r
        gate = None
        for h in range(PEER_HEADS):
            s = st_ref[2 * h, pl.ds(i1, 1), :] + st_ref[2 * h + 1]
            p = et_ref[2 * h, pl.ds(i1, 1), :] * et_ref[2 * h + 1]
            sel = jnp.where(s >= tau_ref[h:h + 1, :], p, 0.0)
            gate = sel if gate is None else gate + sel
        rows = slice(r * PEER_KEYS, (r + 1) * PEER_KEYS)
        blocks.append(jax.nn.gelu(act_t[rows]) * gate)
    w_t = jnp.concatenate(blocks, axis=0)
    o_ref[...] += jnp.dot(w_t.T.astype(BF16), v_ref[...], preferred_element_type=F32)


def _peer_dense(h_t, u, v, st, et, tau, *, tm, te):
    d, t = h_t.shape
    n_exp = u.shape[0]
    hp = 2 * PEER_HEADS
    assert t % tm == 0 and n_exp % te == 0 and te % PEER_KEYS == 0
    once = dict(pipeline_mode=pl.Buffered(1))
    return pl.pallas_call(
        _peer_dense_body,
        grid=(t // tm, n_exp // te),
        in_specs=[pl.BlockSpec((d, tm), lambda i, e: (0, i), **once),
                  pl.BlockSpec((te, d), lambda i, e: (e, 0)),
                  pl.BlockSpec((te, d), lambda i, e: (e, 0)),
                  pl.BlockSpec((hp, PEER_KEYS, tm), lambda i, e: (0, 0, i), **once),
                  pl.BlockSpec((hp, PEER_KEYS, tm), lambda i, e: (0, 0, i), **once),
                  pl.BlockSpec((PEER_HEADS, tm), lambda i, e: (0, i))],
        out_specs=pl.BlockSpec((tm, d), lambda i, e: (i, 0)),
        out_shape=jax.ShapeDtypeStruct((t, d), F32),
        compiler_params=_params("arbitrary", "arbitrary"),
        name="peer_dense",
    )(h_t, u, v, st, et, tau)


MM_TM, MM_TN = 512, 512
ROW_TM = 256
DIL_BLOCKS_PER_STEP = 4
PEER_TM, PEER_TE = 512, 512


def kernel(x, mem, positions, w_in, gm_ln_g, gm_ln_b, gm_w_s, gm_b_s, w_mem_kv, w_gate, b_gate,
           w_br_gmlp, w_br_dil, w_br_mem, w_out, ln_mix_g, ln_mix_b, peer_w_query, peer_sub_keys,
           peer_u, peer_v, ln_ffn_g, ln_ffn_b):
    batch, seq, d = x.shape
    t = batch * seq
    assert w_in.shape[0] == DEPTH and w_in.shape[2] == IN_WIDTH
    x2 = x.reshape(t, d)
    x_bf = x2.astype(BF16)
    mem_bf = mem.reshape(-1, d).astype(BF16)

    proj = _matmul(x_bf, w_in.reshape(d, IN_WIDTH), out_dtype=BF16, tm=MM_TM, tn=MM_TN,
                   name="in_proj")
    gates = _matmul(x_bf, w_gate.reshape(d, 3 * d), bias=b_gate.reshape(3 * d), act="sigmoid",
                    out_dtype=BF16, tm=MM_TM, tn=MM_TN, name="gate_proj")
    mkv = _matmul(mem_bf, w_mem_kv.reshape(d, 2 * MEM_WIDTH), out_dtype=BF16, tm=MM_TM, tn=MM_TN,
                  name="mem_kv_proj")

    a_out = _gmlp(proj, gm_ln_g.reshape(-1), gm_ln_b.reshape(-1), gm_w_s.reshape(GM_GROUPS, BLOCK, BLOCK),
                  gm_b_s.reshape(GM_GROUPS, BLOCK), tm=512)

    cos, sin = _rope_table(positions, tm=1024)
    outs, lses = [], []
    for g, dil in enumerate(DILATIONS):
        qg, kg, vg = _rope_perm(proj, cos, sin, g, dil)
        o, lse = _dilated_attention(qg, kg, vg, dil, batch, seq, nblk=DIL_BLOCKS_PER_STEP)
        outs.append(o)
        lses.append(lse)
    b_out = _dil_merge(outs, lses)

    m_out = _mem_attention(proj, mkv, batch, seq, tm=512)

    merged = _merge(a_out, b_out, m_out, w_br_gmlp.reshape(GM_WIDTH, d),
                    w_br_dil.reshape(DIL_GROUP_WIDTH, d), w_br_mem.reshape(MEM_WIDTH, d), gates,
                    tm=MM_TM, tn=MM_TN)
    z1 = _matmul(merged, w_out.reshape(d, d), res=x2, alpha=ALPHA, out_dtype=F32, tm=MM_TM,
                 tn=MM_TN, name="out_proj")
    h1, h1_bf, h1_t = _layer_norm(z1, ln_mix_g.reshape(-1), ln_mix_b.reshape(-1), want_t=True,
                                  tm=ROW_TM, name="ln_mix")

    hq = PEER_HEADS * 2 * PEER_HALF
    q = _matmul(h1_bf, peer_w_query.reshape(d, hq), out_dtype=F32, tm=MM_TM, tn=MM_TN,
                name="peer_query")
    st, et, tau = _peer_route(q, peer_sub_keys, tm=256)
    n_exp = PEER_KEYS * PEER_KEYS
    y = _peer_dense(h1_t, peer_u.reshape(n_exp, d).astype(BF16), peer_v.reshape(n_exp, d).astype(BF16),
                    st, et, tau, tm=PEER_TM, te=PEER_TE)
    (out,) = _layer_norm(h1, ln_ffn_g.reshape(-1), ln_ffn_b.reshape(-1), add=y, alpha=ALPHA,
                         tm=ROW_TM, name="ln_ffn")
    return out.reshape(batch, seq, d)
```

```python
import functools

import jax
import jax.numpy as jnp
from jax import lax
from jax.experimental import pallas as pl
from jax.experimental.pallas import tpu as pltpu

F32 = jnp.float32
BF16 = jnp.bfloat16

HEAD_DIM = 128
BLOCK = 128
GM_GROUPS = 12
GM_WIDTH = GM_GROUPS * 128
DILATIONS = (1, 4, 16)
DIL_HEADS = 4
DIL_GROUP_WIDTH = DIL_HEADS * HEAD_DIM
DIL_WIDTH = len(DILATIONS) * DIL_GROUP_WIDTH
MEM_HEADS = 4
MEM_HEAD_DIM = 256
MEM_WIDTH = MEM_HEADS * MEM_HEAD_DIM
PEER_HEADS = 8
PEER_KEYS = 128
PEER_TOPK = 16
PEER_HALF = 128
ROPE_THETA = 10000.0
LN_EPS = 1e-5
DEPTH = 1
ALPHA = (2 * DEPTH) ** 0.25
NEG_INF = -1e30

OFF_GU, OFF_GV = 0, GM_WIDTH
OFF_DQ = 2 * GM_WIDTH
OFF_DK = OFF_DQ + DIL_WIDTH
OFF_DV = OFF_DK + DIL_WIDTH
OFF_MQ = OFF_DV + DIL_WIDTH
IN_WIDTH = OFF_MQ + MEM_WIDTH

V7X_VMEM_BYTES = 64 * 1024 * 1024
VMEM_LIMIT = V7X_VMEM_BYTES - 8 * 1024 * 1024

NT_DIMS = (((1,), (1,)), ((), ()))


def _params(*sem):
    return pltpu.CompilerParams(dimension_semantics=sem, vmem_limit_bytes=VMEM_LIMIT)


def _mm_body(*refs, act, alpha, has_bias, has_res):
    x_ref, w_ref = refs[0], refs[1]
    k = 2
    b_ref = r_ref = None
    if has_bias:
        b_ref = refs[k]
        k += 1
    if has_res:
        r_ref = refs[k]
        k += 1
    o_ref, wb_ref = refs[k], refs[k + 1]

    @pl.when(pl.program_id(1) == 0)
    def _():
        wb_ref[...] = w_ref[...].astype(BF16)

    acc = jnp.dot(x_ref[...], wb_ref[...], preferred_element_type=F32)
    if has_bias:
        acc = acc + b_ref[...]
    if act == "sigmoid":
        acc = jax.nn.sigmoid(acc)
    if has_res:
        acc = alpha * r_ref[...] + acc
    o_ref[...] = acc.astype(o_ref.dtype)


def _matmul(x, w, *, out_dtype, tm, tn, bias=None, res=None, alpha=1.0, act=None, name):
    m, k = x.shape
    n = w.shape[1]
    tm, tn = min(tm, m), min(tn, n)
    assert m % tm == 0 and n % tn == 0
    in_specs = [pl.BlockSpec((tm, k), lambda j, i: (i, 0)),
                pl.BlockSpec((k, tn), lambda j, i: (0, j))]
    args = [x, w]
    if bias is not None:
        in_specs.append(pl.BlockSpec((1, tn), lambda j, i: (0, j)))
        args.append(bias.reshape(1, n))
    if res is not None:
        in_specs.append(pl.BlockSpec((tm, tn), lambda j, i: (i, j)))
        args.append(res)
    body = functools.partial(_mm_body, act=act, alpha=alpha, has_bias=bias is not None,
                             has_res=res is not None)
    return pl.pallas_call(
        body,
        grid=(n // tn, m // tm),
        in_specs=in_specs,
        out_specs=pl.BlockSpec((tm, tn), lambda j, i: (i, j)),
        out_shape=jax.ShapeDtypeStruct((m, n), out_dtype),
        scratch_shapes=[pltpu.VMEM((k, tn), BF16)],
        compiler_params=_params("arbitrary", "arbitrary"),
        name=name,
    )(*args)


def _ln_rows(z, g, b):
    mu = jnp.mean(z, axis=-1, keepdims=True)
    zc = z - mu
    var = jnp.mean(zc * zc, axis=-1, keepdims=True)
    return zc * lax.rsqrt(var + LN_EPS) * g + b


def _ln_body(*refs, alpha, has_add, want_t):
    k = 0
    z = refs[k][...]
    k += 1
    if has_add:
        z = alpha * z + refs[k][...]
        k += 1
    g_ref, b_ref = refs[k], refs[k + 1]
    k += 2
    h = _ln_rows(z, g_ref[...], b_ref[...])
    refs[k][...] = h
    if want_t:
        refs[k + 1][...] = h.astype(BF16)
        refs[k + 2][...] = h.T.astype(BF16)


def _layer_norm(z, g, b, *, add=None, alpha=1.0, want_t=False, tm, name):
    t, d = z.shape
    tm = min(tm, t)
    assert t % tm == 0
    row = pl.BlockSpec((tm, d), lambda i: (i, 0))
    vec = pl.BlockSpec((1, d), lambda i: (0, 0))
    in_specs, args = [row], [z]
    if add is not None:
        in_specs.append(row)
        args.append(add)
    in_specs += [vec, vec]
    args += [g.reshape(1, d), b.reshape(1, d)]
    out_specs, out_shape = [row], [jax.ShapeDtypeStruct((t, d), F32)]
    if want_t:
        out_specs += [row, pl.BlockSpec((d, tm), lambda i: (0, i))]
        out_shape += [jax.ShapeDtypeStruct((t, d), BF16), jax.ShapeDtypeStruct((d, t), BF16)]
    return pl.pallas_call(
        functools.partial(_ln_body, alpha=alpha, has_add=add is not None, want_t=want_t),
        grid=(t // tm,),
        in_specs=in_specs,
        out_specs=out_specs,
        out_shape=out_shape,
        compiler_params=_params("arbitrary"),
        name=name,
    )(*args)


def _gmlp_body(gu_ref, gv_ref, g_ref, b_ref, ws_ref, bs_ref, o_ref):
    tm = gu_ref.shape[0]
    u = jax.nn.gelu(gu_ref[...].astype(F32))
    v = _ln_rows(jax.nn.gelu(gv_ref[...].astype(F32)), g_ref[...], b_ref[...]).astype(BF16)
    causal = (lax.broadcasted_iota(jnp.int32, (BLOCK, BLOCK), 0)
              >= lax.broadcasted_iota(jnp.int32, (BLOCK, BLOCK), 1))
    for g in range(GM_GROUPS):
        cols = slice(g * 128, (g + 1) * 128)
        w = jnp.where(causal, ws_ref[g], 0.0).astype(BF16)
        for c in range(tm // BLOCK):
            rows = slice(c * BLOCK, (c + 1) * BLOCK)
            mixed = jnp.dot(w, v[rows, cols], preferred_element_type=F32) + bs_ref[g]
            o_ref[rows, cols] = (u[rows, cols] * mixed).astype(o_ref.dtype)


def _gmlp(proj, ln_g, ln_b, w_s, b_s, *, tm):
    t = proj.shape[0]
    assert t % tm == 0 and tm % BLOCK == 0
    bs_full = jnp.broadcast_to(b_s[:, :, None], (GM_GROUPS, BLOCK, 128))
    return pl.pallas_call(
        _gmlp_body,
        grid=(t // tm,),
        in_specs=[pl.BlockSpec((tm, GM_WIDTH), lambda i: (i, OFF_GU // GM_WIDTH)),
                  pl.BlockSpec((tm, GM_WIDTH), lambda i: (i, OFF_GV // GM_WIDTH)),
                  pl.BlockSpec((1, GM_WIDTH), lambda i: (0, 0)),
                  pl.BlockSpec((1, GM_WIDTH), lambda i: (0, 0)),
                  pl.BlockSpec((GM_GROUPS, BLOCK, BLOCK), lambda i: (0, 0, 0)),
                  pl.BlockSpec((GM_GROUPS, BLOCK, 128), lambda i: (0, 0, 0))],
        out_specs=pl.BlockSpec((tm, GM_WIDTH), lambda i: (i, 0)),
        out_shape=jax.ShapeDtypeStruct((t, GM_WIDTH), BF16),
        compiler_params=_params("arbitrary"),
        name="gmlp",
    )(proj, proj, ln_g.reshape(1, GM_WIDTH), ln_b.reshape(1, GM_WIDTH), w_s, bs_full)


PERM_TILE = BLOCK * max(DILATIONS)


def _rope_table_body(pos_ref, inv_ref, cos_ref, sin_ref):
    half = HEAD_DIM // 2
    ang = pos_ref[...] * inv_ref[...]
    lane = lax.broadcasted_iota(jnp.int32, ang.shape, 1)
    sin = jnp.sin(ang)
    cos_ref[...] = jnp.cos(ang)
    sin_ref[...] = jnp.where(lane < half, -sin, sin)


def _rope_table(positions, *, tm):
    t = positions.size
    assert t % tm == 0
    half = HEAD_DIM // 2
    inv_freq = ROPE_THETA ** (-jnp.arange(half, dtype=F32) / half)
    inv = jnp.concatenate([inv_freq, inv_freq]).reshape(1, HEAD_DIM)
    pos = positions.astype(F32).reshape(t, 1)
    out = jax.ShapeDtypeStruct((t, HEAD_DIM), F32)
    return pl.pallas_call(
        _rope_table_body,
        grid=(t // tm,),
        in_specs=[pl.BlockSpec((tm, 1), lambda i: (i, 0)),
                  pl.BlockSpec((1, HEAD_DIM), lambda i: (0, 0))],
        out_specs=[pl.BlockSpec((tm, HEAD_DIM), lambda i: (i, 0))] * 2,
        out_shape=[out, out],
        compiler_params=_params("arbitrary"),
        name="rope_table",
    )(pos, inv)


def _rope_perm_body(q_ref, k_ref, v_ref, cos_ref, sin_ref, qo_ref, ko_ref, vo_ref, buf_ref, *, d):
    tm = q_ref.shape[0]
    half = HEAD_DIM // 2
    span = BLOCK * d
    for h in range(DIL_HEADS):
        cols = slice(h * HEAD_DIM, (h + 1) * HEAD_DIM)
        for src, dst, rotate in ((q_ref, qo_ref, True), (k_ref, ko_ref, True), (v_ref, vo_ref, False)):
            t = src[:, cols].astype(F32)
            if rotate:
                swapped = jnp.concatenate([t[:, half:], t[:, :half]], axis=1)
                t = t * cos_ref[...] + swapped * sin_ref[...]
            if d == 1:
                dst[:, cols] = t.astype(dst.dtype)
                continue
            buf_ref[...] = t
            for n in range(tm // span):
                for r in range(d):
                    rows = buf_ref[pl.ds(n * span + r, BLOCK, stride=d), :]
                    dst[n * span + r * BLOCK:n * span + (r + 1) * BLOCK, cols] = rows.astype(dst.dtype)


def _rope_perm(proj, cos, sin, group, d):
    t = proj.shape[0]
    tm, gw = PERM_TILE, DIL_GROUP_WIDTH
    assert t % tm == 0 and tm % (BLOCK * d) == 0
    col = lambda off: pl.BlockSpec((tm, gw), lambda i: (i, off // gw + group))
    tab = pl.BlockSpec((tm, HEAD_DIM), lambda i: (i, 0))
    out = jax.ShapeDtypeStruct((t, gw), BF16)
    return pl.pallas_call(
        functools.partial(_rope_perm_body, d=d),
        grid=(t // tm,),
        in_specs=[col(OFF_DQ), col(OFF_DK), col(OFF_DV), tab, tab],
        out_specs=[pl.BlockSpec((tm, gw), lambda i: (i, 0))] * 3,
        out_shape=[out, out, out],
        scratch_shapes=[pltpu.VMEM((tm, HEAD_DIM), F32)],
        compiler_params=_params("arbitrary"),
        name=f"rope_perm_d{d}",
    )(proj, proj, proj, cos, sin)


def _dil_body(q_ref, kc_ref, vc_ref, kp_ref, vp_ref, o_ref, lse_ref, *, d, nblk):
    step = pl.program_id(1)
    qi = lax.broadcasted_iota(jnp.int32, (BLOCK, BLOCK), 0)
    ki = lax.broadcasted_iota(jnp.int32, (BLOCK, BLOCK), 1)
    mask_cur = ki <= qi
    scale = HEAD_DIM ** -0.5
    for i in range(nblk):
        rows = slice(i * BLOCK, (i + 1) * BLOCK)
        if d >= nblk:
            kpr, vpr, prow, has_prev = kp_ref, vp_ref, rows, step >= d // nblk
        elif i == 0:
            kpr, vpr, prow, has_prev = kp_ref, vp_ref, slice(0, BLOCK), step > 0
        else:
            kpr, vpr, prow, has_prev = kc_ref, vc_ref, slice((i - 1) * BLOCK, i * BLOCK), None
        mask_prev = ki >= (qi if has_prev is None else qi + jnp.where(has_prev, 0, BLOCK))
        for h in range(DIL_HEADS):
            cols = slice(h * HEAD_DIM, (h + 1) * HEAD_DIM)
            q = q_ref[rows, cols]
            sp = lax.dot_general(q, kpr[prow, cols], NT_DIMS, preferred_element_type=F32) * scale
            sc = lax.dot_general(q, kc_ref[rows, cols], NT_DIMS, preferred_element_type=F32) * scale
            sp = jnp.where(mask_prev, sp, NEG_INF)
            sc = jnp.where(mask_cur, sc, NEG_INF)
            m = jnp.maximum(jnp.max(sp, axis=-1, keepdims=True), jnp.max(sc, axis=-1, keepdims=True))
            pp = jnp.exp(sp - m)
            pc = jnp.exp(sc - m)
            den = jnp.sum(pp, axis=-1, keepdims=True) + jnp.sum(pc, axis=-1, keepdims=True)
            o = (jnp.dot(pp.astype(BF16), vpr[prow, cols], preferred_element_type=F32)
                 + jnp.dot(pc.astype(BF16), vc_ref[rows, cols], preferred_element_type=F32))
            o_ref[rows, cols] = o / den
            lse_ref[rows, cols] = jnp.broadcast_to(m + jnp.log(den), (BLOCK, HEAD_DIM))


def _dilated_attention(q, k, v, d, batch, seq, *, nblk):
    gw = DIL_GROUP_WIDTH
    rows = nblk * BLOCK
    assert seq % rows == 0 and (d == 1 or d % nblk == 0)
    steps = seq // rows
    cur = pl.BlockSpec((rows, gw), lambda b, s: (b * steps + s, 0))
    if d >= nblk:
        back = d // nblk
        prev = pl.BlockSpec((rows, gw), lambda b, s: (b * steps + jnp.maximum(s - back, 0), 0))
    else:
        prev = pl.BlockSpec((BLOCK, gw),
                            lambda b, s: (b * steps * nblk + jnp.maximum(s * nblk - 1, 0), 0))
    out = jax.ShapeDtypeStruct((batch * seq, gw), F32)
    return pl.pallas_call(
        functools.partial(_dil_body, d=d, nblk=nblk),
        grid=(batch, steps),
        in_specs=[cur, cur, cur, prev, prev],
        out_specs=[cur, cur],
        out_shape=[out, out],
        compiler_params=_params("arbitrary", "arbitrary"),
        name=f"dil_attn_d{d}",
    )(q, k, v, k, v)


def _dil_merge_body(*refs):
    n = len(DILATIONS)
    o_refs, l_refs, out_ref, bufs = refs[:n], refs[n:2 * n], refs[2 * n], refs[2 * n + 1:]
    tm = out_ref.shape[0]
    outs, lses = [], []
    k = 0
    for d, o_ref, l_ref in zip(DILATIONS, o_refs, l_refs):
        if d == 1:
            outs.append(o_ref[...])
            lses.append(l_ref[...])
            continue
        span = BLOCK * d
        for src, buf in ((o_ref, bufs[k]), (l_ref, bufs[k + 1])):
            for s in range(tm // span):
                for r in range(d):
                    buf[pl.ds(s * span + r, BLOCK, stride=d), :] = (
                        src[s * span + r * BLOCK:s * span + (r + 1) * BLOCK, :])
        outs.append(bufs[k][...])
        lses.append(bufs[k + 1][...])
        k += 2
    m = functools.reduce(jnp.maximum, lses)
    es = [jnp.exp(l - m) for l in lses]
    num = sum(e * o for e, o in zip(es, outs))
    out_ref[...] = (num / sum(es)).astype(out_ref.dtype)


def _dil_merge(outs, lses):
    t, w = outs[0].shape
    tm = PERM_TILE
    assert t % tm == 0
    spec = pl.BlockSpec((tm, HEAD_DIM), lambda i, h: (i, h))
    n_buf = 2 * sum(1 for d in DILATIONS if d != 1)
    return pl.pallas_call(
        _dil_merge_body,
        grid=(t // tm, w // HEAD_DIM),
        in_specs=[spec] * (2 * len(DILATIONS)),
        out_specs=spec,
        out_shape=jax.ShapeDtypeStruct((t, w), BF16),
        scratch_shapes=[pltpu.VMEM((tm, HEAD_DIM), F32)] * n_buf,
        compiler_params=_params("arbitrary", "arbitrary"),
        name="dil_merge",
    )(*outs, *lses)


def _mem_body(q0, q1, q2, q3, kv_ref, o_ref):
    scale = MEM_HEAD_DIM ** -0.5
    for h, q_ref in enumerate((q0, q1, q2, q3)):
        cols = slice(h * MEM_HEAD_DIM, (h + 1) * MEM_HEAD_DIM)
        k = kv_ref[:, cols]
        v = kv_ref[:, MEM_WIDTH + h * MEM_HEAD_DIM:MEM_WIDTH + (h + 1) * MEM_HEAD_DIM]
        s = lax.dot_general(q_ref[...], k, NT_DIMS, preferred_element_type=F32) * scale
        m = jnp.max(s, axis=-1, keepdims=True)
        p = jnp.exp(s - m)
        den = jnp.sum(p, axis=-1, keepdims=True)
        o = jnp.dot(p.astype(BF16), v, preferred_element_type=F32) / den
        o_ref[:, cols] = o.astype(o_ref.dtype)


def _mem_attention(proj, mkv, batch, seq, *, tm):
    mem_len = mkv.shape[0] // batch
    assert seq % tm == 0
    steps = seq // tm
    q_specs = [pl.BlockSpec((tm, MEM_HEAD_DIM),
                            lambda b, i, h=h: (b * steps + i, OFF_MQ // MEM_HEAD_DIM + h))
               for h in range(MEM_HEADS)]
    return pl.pallas_call(
        _mem_body,
        grid=(batch, steps),
        in_specs=q_specs + [pl.BlockSpec((mem_len, 2 * MEM_WIDTH), lambda b, i: (b, 0))],
        out_specs=pl.BlockSpec((tm, MEM_WIDTH), lambda b, i: (b * steps + i, 0)),
        out_shape=jax.ShapeDtypeStruct((batch * seq, MEM_WIDTH), BF16),
        compiler_params=_params("arbitrary", "arbitrary"),
        name="mem_attn",
    )(proj, proj, proj, proj, mkv)


def _merge_body(a_ref, b_ref, m_ref, wa_ref, wb_ref, wm_ref, g0_ref, g1_ref, g2_ref, o_ref,
                wa_bf, wb_bf, wm_bf):
    @pl.when(pl.program_id(1) == 0)
    def _():
        wa_bf[...] = wa_ref[...].astype(BF16)
        wb_bf[...] = wb_ref[...].astype(BF16)
        wm_bf[...] = wm_ref[...].astype(BF16)

    ya = jnp.dot(a_ref[...], wa_bf[...], preferred_element_type=F32)
    yb = jnp.dot(b_ref[...], wb_bf[...], preferred_element_type=F32)
    ym = jnp.dot(m_ref[...], wm_bf[...], preferred_element_type=F32)
    o = (g0_ref[...].astype(F32) * ya + g1_ref[...].astype(F32) * yb
         + g2_ref[...].astype(F32) * ym)
    o_ref[...] = o.astype(o_ref.dtype)


def _merge(a_out, b_out, m_out, w_a, w_b, w_m, gates, *, tm, tn):
    t = a_out.shape[0]
    d = w_a.shape[1]
    assert t % tm == 0 and d % tn == 0
    nj = d // tn
    rows = lambda w: pl.BlockSpec((tm, w), lambda j, i: (i, 0))
    wcol = lambda k: pl.BlockSpec((k, tn), lambda j, i: (0, j))
    gate = lambda br: pl.BlockSpec((tm, tn), lambda j, i, br=br: (i, br * nj + j))
    return pl.pallas_call(
        _merge_body,
        grid=(nj, t // tm),
        in_specs=[rows(GM_WIDTH), rows(DIL_GROUP_WIDTH), rows(MEM_WIDTH),
                  wcol(GM_WIDTH), wcol(DIL_GROUP_WIDTH), wcol(MEM_WIDTH),
                  gate(0), gate(1), gate(2)],
        out_specs=pl.BlockSpec((tm, tn), lambda j, i: (i, j)),
        out_shape=jax.ShapeDtypeStruct((t, d), BF16),
        scratch_shapes=[pltpu.VMEM((GM_WIDTH, tn), BF16), pltpu.VMEM((DIL_GROUP_WIDTH, tn), BF16),
                        pltpu.VMEM((MEM_WIDTH, tn), BF16)],
        compiler_params=_params("arbitrary", "arbitrary"),
        name="branch_merge",
    )(a_out, b_out, m_out, w_a, w_b, w_m, gates, gates, gates)


def _top_values(x, row, k, want_rank):
    vals = []
    rank = jnp.full(x.shape, float(PEER_KEYS), F32) if want_rank else None
    for r in range(k):
        m = jnp.max(x, axis=0, keepdims=True)
        first = jnp.min(jnp.where(x == m, row, PEER_KEYS), axis=0, keepdims=True)
        hit = row == first
        x = jnp.where(hit, -jnp.inf, x)
        if want_rank:
            rank = jnp.where(hit, float(r), rank)
        vals.append(m)
    return jnp.concatenate(vals, axis=0), rank


def _staircase(a1, a2, combine, fill):
    r8 = lax.broadcasted_iota(jnp.int32, (8, a1.shape[1]), 0)
    r16 = lax.broadcasted_iota(jnp.int32, a1.shape, 0)
    lo8 = lambda x: x[0:8]
    return jnp.concatenate([
        combine(a1[0:1], a2),
        combine(a1[1:2], lo8(a2)),
        jnp.where(r8 < 5, combine(a1[2:3], lo8(a2)), fill),
        jnp.where(r8 < 4, combine(a1[3:4], lo8(a2)), fill),
        jnp.where(r16 >= 4, combine(a1, a2[0:1]), fill),
        jnp.where(r8 >= 4, combine(lo8(a1), a2[1:2]), fill),
        jnp.where(r8 == 4, combine(lo8(a1), a2[2:3]), fill),
    ], axis=0)


def _peer_route_body(q_ref, keys_ref, g_ref, cnt_ref, e1_ref, rank_ref, e2_ref):
    tm = q_ref.shape[0]
    k = PEER_TOPK
    row = lax.broadcasted_iota(jnp.int32, (PEER_KEYS, tm), 0)
    for h in range(PEER_HEADS):
        sts, tops = [], []
        for p in range(2):
            hp = 2 * h + p
            qs = q_ref[:, hp * PEER_HALF:(hp + 1) * PEER_HALF]
            st = lax.dot_general(keys_ref[hp], qs, NT_DIMS, preferred_element_type=F32,
                                 precision=lax.Precision.HIGHEST)
            top, rank = _top_values(st, row, k, want_rank=p == 1)
            sts.append(st)
            tops.append(top)
        a1, a2 = tops
        m1, m2 = a1[0:1], a2[0:1]
        cand = _staircase(a1, a2, jnp.add, -jnp.inf)
        prob = _staircase(jnp.exp(a1 - m1), jnp.exp(a2 - m2), jnp.multiply, 0.0)
        x = cand
        left = jnp.full((1, tm), float(k), F32)
        tau = jnp.full((1, tm), jnp.inf, F32)
        for _ in range(k):
            m = jnp.max(x, axis=0, keepdims=True)
            eq = x == m
            tau = jnp.where(left > 0, m, tau)
            left = left - jnp.sum(jnp.where(eq, 1.0, 0.0), axis=0, keepdims=True)
            x = jnp.where(eq, -jnp.inf, x)
        z = jnp.sum(jnp.where(cand >= tau, prob, 0.0), axis=0, keepdims=True)
        cnt = jnp.zeros((PEER_KEYS, tm), F32)
        for b in range(k):
            cnt = cnt + jnp.where(sts[0] + a2[b:b + 1] >= tau, 1.0, 0.0)
        cnt_ref[h] = cnt
        e1_ref[h] = jnp.exp(sts[0] - m1) / z
        rank_ref[h] = rank.astype(BF16)
        e2_ref[h] = jnp.exp(sts[1] - m2).astype(BF16)

    def tile16(row_f32):
        packed = jnp.broadcast_to(row_f32, (16, tm)).astype(BF16)
        return jnp.concatenate([packed] * (PEER_KEYS // 16), axis=0)

    def build(i1, carry):
        gate = None
        for h in range(PEER_HEADS):
            cnt = tile16(cnt_ref[h, pl.ds(i1, 1), :])
            e1 = tile16(e1_ref[h, pl.ds(i1, 1), :])
            sel = jnp.where(rank_ref[h] < cnt, e1 * e2_ref[h], jnp.zeros((), BF16))
            gate = sel if gate is None else gate + sel
        g_ref[pl.ds(pl.multiple_of(i1 * PEER_KEYS, PEER_KEYS), PEER_KEYS), :] = gate
        return carry

    lax.fori_loop(0, PEER_KEYS, build, 0)


def _peer_route(q, sub_keys, *, tm):
    t = q.shape[0]
    assert t % tm == 0
    hp = 2 * PEER_HEADS
    keys = sub_keys.reshape(hp, PEER_KEYS, PEER_HALF)
    n_exp = PEER_KEYS * PEER_KEYS
    per_head = lambda dt: pltpu.VMEM((PEER_HEADS, PEER_KEYS, tm), dt)
    return pl.pallas_call(
        _peer_route_body,
        grid=(t // tm,),
        in_specs=[pl.BlockSpec((tm, hp * PEER_HALF), lambda i: (i, 0)),
                  pl.BlockSpec((hp, PEER_KEYS, PEER_HALF), lambda i: (0, 0, 0))],
        out_specs=pl.BlockSpec((n_exp, tm), lambda i: (0, i)),
        out_shape=jax.ShapeDtypeStruct((n_exp, t), BF16),
        scratch_shapes=[per_head(F32), per_head(F32), per_head(BF16), per_head(BF16)],
        compiler_params=_params("arbitrary"),
        name="peer_route",
    )(q, keys)


def _peer_dense_body(xt_ref, u_ref, v_ref, g_ref, o_ref):
    @pl.when(pl.program_id(1) == 0)
    def _():
        o_ref[...] = jnp.zeros_like(o_ref)

    act_t = jnp.dot(u_ref[...], xt_ref[...], preferred_element_type=F32)
    w_t = jax.nn.gelu(act_t) * g_ref[...].astype(F32)
    o_ref[...] += jnp.dot(w_t.T.astype(BF16), v_ref[...], preferred_element_type=F32)


def _peer_dense(h_t, u, v, gate_t, *, tm, te):
    d, t = h_t.shape
    n_exp = u.shape[0]
    assert t % tm == 0 and n_exp % te == 0
    once = dict(pipeline_mode=pl.Buffered(1))
    return pl.pallas_call(
        _peer_dense_body,
        grid=(t // tm, n_exp // te),
        in_specs=[pl.BlockSpec((d, tm), lambda i, e: (0, i), **once),
                  pl.BlockSpec((te, d), lambda i, e: (e, 0)),
                  pl.BlockSpec((te, d), lambda i, e: (e, 0)),
                  pl.BlockSpec((te, tm), lambda i, e: (e, i))],
        out_specs=pl.BlockSpec((tm, d), lambda i, e: (i, 0)),
        out_shape=jax.ShapeDtypeStruct((t, d), F32),
        compiler_params=_params("arbitrary", "arbitrary"),
        name="peer_dense",
    )(h_t, u, v, gate_t)


MM_TM, MM_TN = 512, 512
ROW_TM = 256
DIL_BLOCKS_PER_STEP = 4
PEER_TM, PEER_TE = 512, 512
ROUTE_TM = 256


def kernel(x, mem, positions, w_in, gm_ln_g, gm_ln_b, gm_w_s, gm_b_s, w_mem_kv, w_gate, b_gate,
           w_br_gmlp, w_br_dil, w_br_mem, w_out, ln_mix_g, ln_mix_b, peer_w_query, peer_sub_keys,
           peer_u, peer_v, ln_ffn_g, ln_ffn_b):
    batch, seq, d = x.shape
    t = batch * seq
    assert w_in.shape[0] == DEPTH and w_in.shape[2] == IN_WIDTH
    x2 = x.reshape(t, d)
    x_bf = x2.astype(BF16)
    mem_bf = mem.reshape(-1, d).astype(BF16)

    proj = _matmul(x_bf, w_in.reshape(d, IN_WIDTH), out_dtype=BF16, tm=MM_TM, tn=MM_TN,
                   name="in_proj")
    gates = _matmul(x_bf, w_gate.reshape(d, 3 * d), bias=b_gate.reshape(3 * d), act="sigmoid",
                    out_dtype=BF16, tm=MM_TM, tn=MM_TN, name="gate_proj")
    mkv = _matmul(mem_bf, w_mem_kv.reshape(d, 2 * MEM_WIDTH), out_dtype=BF16, tm=MM_TM, tn=MM_TN,
                  name="mem_kv_proj")

    a_out = _gmlp(proj, gm_ln_g.reshape(-1), gm_ln_b.reshape(-1), gm_w_s.reshape(GM_GROUPS, BLOCK, BLOCK),
                  gm_b_s.reshape(GM_GROUPS, BLOCK), tm=512)

    cos, sin = _rope_table(positions, tm=1024)
    outs, lses = [], []
    for g, dil in enumerate(DILATIONS):
        qg, kg, vg = _rope_perm(proj, cos, sin, g, dil)
        o, lse = _dilated_attention(qg, kg, vg, dil, batch, seq, nblk=DIL_BLOCKS_PER_STEP)
        outs.append(o)
        lses.append(lse)
    b_out = _dil_merge(outs, lses)

    m_out = _mem_attention(proj, mkv, batch, seq, tm=512)

    merged = _merge(a_out, b_out, m_out, w_br_gmlp.reshape(GM_WIDTH, d),
                    w_br_dil.reshape(DIL_GROUP_WIDTH, d), w_br_mem.reshape(MEM_WIDTH, d), gates,
                    tm=MM_TM, tn=MM_TN)
    z1 = _matmul(merged, w_out.reshape(d, d), res=x2, alpha=ALPHA, out_dtype=F32, tm=MM_TM,
                 tn=MM_TN, name="out_proj")
    h1, h1_bf, h1_t = _layer_norm(z1, ln_mix_g.reshape(-1), ln_mix_b.reshape(-1), want_t=True,
                                  tm=ROW_TM, name="ln_mix")

    hq = PEER_HEADS * 2 * PEER_HALF
    q = _matmul(h1_bf, peer_w_query.reshape(d, hq), out_dtype=F32, tm=MM_TM, tn=MM_TN,
                name="peer_query")
    gate_t = _peer_route(q, peer_sub_keys, tm=ROUTE_TM)
    n_exp = PEER_KEYS * PEER_KEYS
    y = _peer_dense(h1_t, peer_u.reshape(n_exp, d).astype(BF16), peer_v.reshape(n_exp, d).astype(BF16),
                    gate_t, tm=PEER_TM, te=PEER_TE)
    (out,) = _layer_norm(h1, ln_ffn_g.reshape(-1), ln_ffn_b.reshape(-1), add=y, alpha=ALPHA,
                         tm=ROW_TM, name="ln_ffn")
    return out.reshape(batch, seq, d)
```

```python
import functools

import jax
import jax.numpy as jnp
from jax import lax
from jax.experimental import pallas as pl
from jax.experimental.pallas import tpu as pltpu

F32 = jnp.float32
BF16 = jnp.bfloat16

HEAD_DIM = 128
BLOCK = 128
GM_GROUPS = 12
GM_WIDTH = GM_GROUPS * 128
DILATIONS = (1, 4, 16)
DIL_HEADS = 4
DIL_GROUP_WIDTH = DIL_HEADS * HEAD_DIM
DIL_WIDTH = len(DILATIONS) * DIL_GROUP_WIDTH
MEM_HEADS = 4
MEM_HEAD_DIM = 256
MEM_WIDTH = MEM_HEADS * MEM_HEAD_DIM
PEER_HEADS = 8
PEER_KEYS = 128
PEER_TOPK = 16
PEER_HALF = 128
ROPE_THETA = 10000.0
LN_EPS = 1e-5
DEPTH = 1
ALPHA = (2 * DEPTH) ** 0.25
NEG_INF = -1e30

OFF_GU, OFF_GV = 0, GM_WIDTH
OFF_DQ = 2 * GM_WIDTH
OFF_DK = OFF_DQ + DIL_WIDTH
OFF_DV = OFF_DK + DIL_WIDTH
OFF_MQ = OFF_DV + DIL_WIDTH
IN_WIDTH = OFF_MQ + MEM_WIDTH

V7X_VMEM_BYTES = 64 * 1024 * 1024
VMEM_LIMIT = V7X_VMEM_BYTES - 8 * 1024 * 1024

NT_DIMS = (((1,), (1,)), ((), ()))


def _params(*sem):
    return pltpu.CompilerParams(dimension_semantics=sem, vmem_limit_bytes=VMEM_LIMIT)


def _mm_body(*refs, act, alpha, has_bias, has_res):
    x_ref, w_ref = refs[0], refs[1]
    k = 2
    b_ref = r_ref = None
    if has_bias:
        b_ref = refs[k]
        k += 1
    if has_res:
        r_ref = refs[k]
        k += 1
    o_ref, wb_ref = refs[k], refs[k + 1]

    @pl.when(pl.program_id(1) == 0)
    def _():
        wb_ref[...] = w_ref[...].astype(BF16)

    acc = jnp.dot(x_ref[...], wb_ref[...], preferred_element_type=F32)
    if has_bias:
        acc = acc + b_ref[...]
    if act == "sigmoid":
        acc = jax.nn.sigmoid(acc)
    if has_res:
        acc = alpha * r_ref[...] + acc
    o_ref[...] = acc.astype(o_ref.dtype)


def _matmul(x, w, *, out_dtype, tm, tn, bias=None, res=None, alpha=1.0, act=None, name):
    m, k = x.shape
    n = w.shape[1]
    tm, tn = min(tm, m), min(tn, n)
    assert m % tm == 0 and n % tn == 0
    in_specs = [pl.BlockSpec((tm, k), lambda j, i: (i, 0)),
                pl.BlockSpec((k, tn), lambda j, i: (0, j))]
    args = [x, w]
    if bias is not None:
        in_specs.append(pl.BlockSpec((1, tn), lambda j, i: (0, j)))
        args.append(bias.reshape(1, n))
    if res is not None:
        in_specs.append(pl.BlockSpec((tm, tn), lambda j, i: (i, j)))
        args.append(res)
    body = functools.partial(_mm_body, act=act, alpha=alpha, has_bias=bias is not None,
                             has_res=res is not None)
    return pl.pallas_call(
        body,
        grid=(n // tn, m // tm),
        in_specs=in_specs,
        out_specs=pl.BlockSpec((tm, tn), lambda j, i: (i, j)),
        out_shape=jax.ShapeDtypeStruct((m, n), out_dtype),
        scratch_shapes=[pltpu.VMEM((k, tn), BF16)],
        compiler_params=_params("arbitrary", "arbitrary"),
        name=name,
    )(*args)


def _ln_rows(z, g, b):
    mu = jnp.mean(z, axis=-1, keepdims=True)
    zc = z - mu
    var = jnp.mean(zc * zc, axis=-1, keepdims=True)
    return zc * lax.rsqrt(var + LN_EPS) * g + b


def _ln_body(*refs, alpha, has_add, want_t):
    k = 0
    z = refs[k][...]
    k += 1
    if has_add:
        z = alpha * z + refs[k][...]
        k += 1
    g_ref, b_ref = refs[k], refs[k + 1]
    k += 2
    h = _ln_rows(z, g_ref[...], b_ref[...])
    refs[k][...] = h
    if want_t:
        refs[k + 1][...] = h.astype(BF16)
        refs[k + 2][...] = h.T.astype(BF16)


def _layer_norm(z, g, b, *, add=None, alpha=1.0, want_t=False, tm, name):
    t, d = z.shape
    tm = min(tm, t)
    assert t % tm == 0
    row = pl.BlockSpec((tm, d), lambda i: (i, 0))
    vec = pl.BlockSpec((1, d), lambda i: (0, 0))
    in_specs, args = [row], [z]
    if add is not None:
        in_specs.append(row)
        args.append(add)
    in_specs += [vec, vec]
    args += [g.reshape(1, d), b.reshape(1, d)]
    out_specs, out_shape = [row], [jax.ShapeDtypeStruct((t, d), F32)]
    if want_t:
        out_specs += [row, pl.BlockSpec((d, tm), lambda i: (0, i))]
        out_shape += [jax.ShapeDtypeStruct((t, d), BF16), jax.ShapeDtypeStruct((d, t), BF16)]
    return pl.pallas_call(
        functools.partial(_ln_body, alpha=alpha, has_add=add is not None, want_t=want_t),
        grid=(t // tm,),
        in_specs=in_specs,
        out_specs=out_specs,
        out_shape=out_shape,
        compiler_params=_params("arbitrary"),
        name=name,
    )(*args)


def _gmlp_body(gu_ref, gv_ref, g_ref, b_ref, ws_ref, bs_ref, o_ref):
    tm = gu_ref.shape[0]
    u = jax.nn.gelu(gu_ref[...].astype(F32))
    v = _ln_rows(jax.nn.gelu(gv_ref[...].astype(F32)), g_ref[...], b_ref[...]).astype(BF16)
    causal = (lax.broadcasted_iota(jnp.int32, (BLOCK, BLOCK), 0)
              >= lax.broadcasted_iota(jnp.int32, (BLOCK, BLOCK), 1))
    for g in range(GM_GROUPS):
        cols = slice(g * 128, (g + 1) * 128)
        w = jnp.where(causal, ws_ref[g], 0.0).astype(BF16)
        for c in range(tm // BLOCK):
            rows = slice(c * BLOCK, (c + 1) * BLOCK)
            mixed = jnp.dot(w, v[rows, cols], preferred_element_type=F32) + bs_ref[g]
            o_ref[rows, cols] = (u[rows, cols] * mixed).astype(o_ref.dtype)


def _gmlp(proj, ln_g, ln_b, w_s, b_s, *, tm):
    t = proj.shape[0]
    assert t % tm == 0 and tm % BLOCK == 0
    bs_full = jnp.broadcast_to(b_s[:, :, None], (GM_GROUPS, BLOCK, 128))
    return pl.pallas_call(
        _gmlp_body,
        grid=(t // tm,),
        in_specs=[pl.BlockSpec((tm, GM_WIDTH), lambda i: (i, OFF_GU // GM_WIDTH)),
                  pl.BlockSpec((tm, GM_WIDTH), lambda i: (i, OFF_GV // GM_WIDTH)),
                  pl.BlockSpec((1, GM_WIDTH), lambda i: (0, 0)),
                  pl.BlockSpec((1, GM_WIDTH), lambda i: (0, 0)),
                  pl.BlockSpec((GM_GROUPS, BLOCK, BLOCK), lambda i: (0, 0, 0)),
                  pl.BlockSpec((GM_GROUPS, BLOCK, 128), lambda i: (0, 0, 0))],
        out_specs=pl.BlockSpec((tm, GM_WIDTH), lambda i: (i, 0)),
        out_shape=jax.ShapeDtypeStruct((t, GM_WIDTH), BF16),
        compiler_params=_params("arbitrary"),
        name="gmlp",
    )(proj, proj, ln_g.reshape(1, GM_WIDTH), ln_b.reshape(1, GM_WIDTH), w_s, bs_full)


PERM_TILE = BLOCK * max(DILATIONS)


def _rope_table_body(pos_ref, inv_ref, cos_ref, sin_ref):
    half = HEAD_DIM // 2
    ang = pos_ref[...] * inv_ref[...]
    lane = lax.broadcasted_iota(jnp.int32, ang.shape, 1)
    sin = jnp.sin(ang)
    cos_ref[...] = jnp.cos(ang)
    sin_ref[...] = jnp.where(lane < half, -sin, sin)


def _rope_table(positions, *, tm):
    t = positions.size
    assert t % tm == 0
    half = HEAD_DIM // 2
    inv_freq = ROPE_THETA ** (-jnp.arange(half, dtype=F32) / half)
    inv = jnp.concatenate([inv_freq, inv_freq]).reshape(1, HEAD_DIM)
    pos = positions.astype(F32).reshape(t, 1)
    out = jax.ShapeDtypeStruct((t, HEAD_DIM), F32)
    return pl.pallas_call(
        _rope_table_body,
        grid=(t // tm,),
        in_specs=[pl.BlockSpec((tm, 1), lambda i: (i, 0)),
                  pl.BlockSpec((1, HEAD_DIM), lambda i: (0, 0))],
        out_specs=[pl.BlockSpec((tm, HEAD_DIM), lambda i: (i, 0))] * 2,
        out_shape=[out, out],
        compiler_params=_params("arbitrary"),
        name="rope_table",
    )(pos, inv)


def _rope_perm_body(q_ref, k_ref, v_ref, cos_ref, sin_ref, qo_ref, ko_ref, vo_ref, buf_ref, *, d):
    tm = q_ref.shape[0]
    half = HEAD_DIM // 2
    span = BLOCK * d
    for h in range(DIL_HEADS):
        cols = slice(h * HEAD_DIM, (h + 1) * HEAD_DIM)
        for src, dst, rotate in ((q_ref, qo_ref, True), (k_ref, ko_ref, True), (v_ref, vo_ref, False)):
            t = src[:, cols].astype(F32)
            if rotate:
                swapped = jnp.concatenate([t[:, half:], t[:, :half]], axis=1)
                t = t * cos_ref[...] + swapped * sin_ref[...]
            if d == 1:
                dst[:, cols] = t.astype(dst.dtype)
                continue
            buf_ref[...] = t
            for n in range(tm // span):
                for r in range(d):
                    rows = buf_ref[pl.ds(n * span + r, BLOCK, stride=d), :]
                    dst[n * span + r * BLOCK:n * span + (r + 1) * BLOCK, cols] = rows.astype(dst.dtype)


def _rope_perm(proj, cos, sin, group, d):
    t = proj.shape[0]
    tm, gw = PERM_TILE, DIL_GROUP_WIDTH
    assert t % tm == 0 and tm % (BLOCK * d) == 0
    col = lambda off: pl.BlockSpec((tm, gw), lambda i: (i, off // gw + group))
    tab = pl.BlockSpec((tm, HEAD_DIM), lambda i: (i, 0))
    out = jax.ShapeDtypeStruct((t, gw), BF16)
    return pl.pallas_call(
        functools.partial(_rope_perm_body, d=d),
        grid=(t // tm,),
        in_specs=[col(OFF_DQ), col(OFF_DK), col(OFF_DV), tab, tab],
        out_specs=[pl.BlockSpec((tm, gw), lambda i: (i, 0))] * 3,
        out_shape=[out, out, out],
        scratch_shapes=[pltpu.VMEM((tm, HEAD_DIM), F32)],
        compiler_params=_params("arbitrary"),
        name=f"rope_perm_d{d}",
    )(proj, proj, proj, cos, sin)


def _dil_body(q_ref, kc_ref, vc_ref, kp_ref, vp_ref, o_ref, lse_ref, *, d, nblk):
    step = pl.program_id(1)
    qi = lax.broadcasted_iota(jnp.int32, (BLOCK, BLOCK), 0)
    ki = lax.broadcasted_iota(jnp.int32, (BLOCK, BLOCK), 1)
    mask_cur = ki <= qi
    scale = HEAD_DIM ** -0.5
    for i in range(nblk):
        rows = slice(i * BLOCK, (i + 1) * BLOCK)
        if d >= nblk:
            kpr, vpr, prow, has_prev = kp_ref, vp_ref, rows, step >= d // nblk
        elif i == 0:
            kpr, vpr, prow, has_prev = kp_ref, vp_ref, slice(0, BLOCK), step > 0
        else:
            kpr, vpr, prow, has_prev = kc_ref, vc_ref, slice((i - 1) * BLOCK, i * BLOCK), None
        mask_prev = ki >= (qi if has_prev is None else qi + jnp.where(has_prev, 0, BLOCK))
        for h in range(DIL_HEADS):
            cols = slice(h * HEAD_DIM, (h + 1) * HEAD_DIM)
            q = q_ref[rows, cols]
            sp = lax.dot_general(q, kpr[prow, cols], NT_DIMS, preferred_element_type=F32) * scale
            sc = lax.dot_general(q, kc_ref[rows, cols], NT_DIMS, preferred_element_type=F32) * scale
            sp = jnp.where(mask_prev, sp, NEG_INF)
            sc = jnp.where(mask_cur, sc, NEG_INF)
            m = jnp.maximum(jnp.max(sp, axis=-1, keepdims=True), jnp.max(sc, axis=-1, keepdims=True))
            pp = jnp.exp(sp - m)
            pc = jnp.exp(sc - m)
            den = jnp.sum(pp, axis=-1, keepdims=True) + jnp.sum(pc, axis=-1, keepdims=True)
            o = (jnp.dot(pp.astype(BF16), vpr[prow, cols], preferred_element_type=F32)
                 + jnp.dot(pc.astype(BF16), vc_ref[rows, cols], preferred_element_type=F32))
            o_ref[rows, cols] = o / den
            lse_ref[rows, cols] = jnp.broadcast_to(m + jnp.log(den), (BLOCK, HEAD_DIM))


def _dilated_attention(q, k, v, d, batch, seq, *, nblk):
    gw = DIL_GROUP_WIDTH
    rows = nblk * BLOCK
    assert seq % rows == 0 and (d == 1 or d % nblk == 0)
    steps = seq // rows
    cur = pl.BlockSpec((rows, gw), lambda b, s: (b * steps + s, 0))
    if d >= nblk:
        back = d // nblk
        prev = pl.BlockSpec((rows, gw), lambda b, s: (b * steps + jnp.maximum(s - back, 0), 0))
    else:
        prev = pl.BlockSpec((BLOCK, gw),
                            lambda b, s: (b * steps * nblk + jnp.maximum(s * nblk - 1, 0), 0))
    out = jax.ShapeDtypeStruct((batch * seq, gw), F32)
    return pl.pallas_call(
        functools.partial(_dil_body, d=d, nblk=nblk),
        grid=(batch, steps),
        in_specs=[cur, cur, cur, prev, prev],
        out_specs=[cur, cur],
        out_shape=[out, out],
        compiler_params=_params("arbitrary", "arbitrary"),
        name=f"dil_attn_d{d}",
    )(q, k, v, k, v)


def _dil_merge_body(*refs):
    n = len(DILATIONS)
    o_refs, l_refs, out_ref, bufs = refs[:n], refs[n:2 * n], refs[2 * n], refs[2 * n + 1:]
    tm = out_ref.shape[0]
    outs, lses = [], []
    k = 0
    for d, o_ref, l_ref in zip(DILATIONS, o_refs, l_refs):
        if d == 1:
            outs.append(o_ref[...])
            lses.append(l_ref[...])
            continue
        span = BLOCK * d
        for src, buf in ((o_ref, bufs[k]), (l_ref, bufs[k + 1])):
            for s in range(tm // span):
                for r in range(d):
                    buf[pl.ds(s * span + r, BLOCK, stride=d), :] = (
                        src[s * span + r * BLOCK:s * span + (r + 1) * BLOCK, :])
        outs.append(bufs[k][...])
        lses.append(bufs[k + 1][...])
        k += 2
    m = functools.reduce(jnp.maximum, lses)
    es = [jnp.exp(l - m) for l in lses]
    num = sum(e * o for e, o in zip(es, outs))
    out_ref[...] = (num / sum(es)).astype(out_ref.dtype)


def _dil_merge(outs, lses):
    t, w = outs[0].shape
    tm = PERM_TILE
    assert t % tm == 0
    spec = pl.BlockSpec((tm, HEAD_DIM), lambda i, h: (i, h))
    n_buf = 2 * sum(1 for d in DILATIONS if d != 1)
    return pl.pallas_call(
        _dil_merge_body,
        grid=(t // tm, w // HEAD_DIM),
        in_specs=[spec] * (2 * len(DILATIONS)),
        out_specs=spec,
        out_shape=jax.ShapeDtypeStruct((t, w), BF16),
        scratch_shapes=[pltpu.VMEM((tm, HEAD_DIM), F32)] * n_buf,
        compiler_params=_params("arbitrary", "arbitrary"),
        name="dil_merge",
    )(*outs, *lses)


def _mem_body(q0, q1, q2, q3, kv_ref, o_ref):
    scale = MEM_HEAD_DIM ** -0.5
    for h, q_ref in enumerate((q0, q1, q2, q3)):
        cols = slice(h * MEM_HEAD_DIM, (h + 1) * MEM_HEAD_DIM)
        k = kv_ref[:, cols]
        v = kv_ref[:, MEM_WIDTH + h * MEM_HEAD_DIM:MEM_WIDTH + (h + 1) * MEM_HEAD_DIM]
        s = lax.dot_general(q_ref[...], k, NT_DIMS, preferred_element_type=F32) * scale
        m = jnp.max(s, axis=-1, keepdims=True)
        p = jnp.exp(s - m)
        den = jnp.sum(p, axis=-1, keepdims=True)
        o = jnp.dot(p.astype(BF16), v, preferred_element_type=F32) / den
        o_ref[:, cols] = o.astype(o_ref.dtype)


def _mem_attention(proj, mkv, batch, seq, *, tm):
    mem_len = mkv.shape[0] // batch
    assert seq % tm == 0
    steps = seq // tm
    q_specs = [pl.BlockSpec((tm, MEM_HEAD_DIM),
                            lambda b, i, h=h: (b * steps + i, OFF_MQ // MEM_HEAD_DIM + h))
               for h in range(MEM_HEADS)]
    return pl.pallas_call(
        _mem_body,
        grid=(batch, steps),
        in_specs=q_specs + [pl.BlockSpec((mem_len, 2 * MEM_WIDTH), lambda b, i: (b, 0))],
        out_specs=pl.BlockSpec((tm, MEM_WIDTH), lambda b, i: (b * steps + i, 0)),
        out_shape=jax.ShapeDtypeStruct((batch * seq, MEM_WIDTH), BF16),
        compiler_params=_params("arbitrary", "arbitrary"),
        name="mem_attn",
    )(proj, proj, proj, proj, mkv)


def _merge_body(a_ref, b_ref, m_ref, wa_ref, wb_ref, wm_ref, g0_ref, g1_ref, g2_ref, o_ref,
                wa_bf, wb_bf, wm_bf):
    @pl.when(pl.program_id(1) == 0)
    def _():
        wa_bf[...] = wa_ref[...].astype(BF16)
        wb_bf[...] = wb_ref[...].astype(BF16)
        wm_bf[...] = wm_ref[...].astype(BF16)

    ya = jnp.dot(a_ref[...], wa_bf[...], preferred_element_type=F32)
    yb = jnp.dot(b_ref[...], wb_bf[...], preferred_element_type=F32)
    ym = jnp.dot(m_ref[...], wm_bf[...], preferred_element_type=F32)
    o = (g0_ref[...].astype(F32) * ya + g1_ref[...].astype(F32) * yb
         + g2_ref[...].astype(F32) * ym)
    o_ref[...] = o.astype(o_ref.dtype)


def _merge(a_out, b_out, m_out, w_a, w_b, w_m, gates, *, tm, tn):
    t = a_out.shape[0]
    d = w_a.shape[1]
    assert t % tm == 0 and d % tn == 0
    nj = d // tn
    rows = lambda w: pl.BlockSpec((tm, w), lambda j, i: (i, 0))
    wcol = lambda k: pl.BlockSpec((k, tn), lambda j, i: (0, j))
    gate = lambda br: pl.BlockSpec((tm, tn), lambda j, i, br=br: (i, br * nj + j))
    return pl.pallas_call(
        _merge_body,
        grid=(nj, t // tm),
        in_specs=[rows(GM_WIDTH), rows(DIL_GROUP_WIDTH), rows(MEM_WIDTH),
                  wcol(GM_WIDTH), wcol(DIL_GROUP_WIDTH), wcol(MEM_WIDTH),
                  gate(0), gate(1), gate(2)],
        out_specs=pl.BlockSpec((tm, tn), lambda j, i: (i, j)),
        out_shape=jax.ShapeDtypeStruct((t, d), BF16),
        scratch_shapes=[pltpu.VMEM((GM_WIDTH, tn), BF16), pltpu.VMEM((DIL_GROUP_WIDTH, tn), BF16),
                        pltpu.VMEM((MEM_WIDTH, tn), BF16)],
        compiler_params=_params("arbitrary", "arbitrary"),
        name="branch_merge",
    )(a_out, b_out, m_out, w_a, w_b, w_m, gates, gates, gates)


def _sorting_network(n):
    pairs = []
    p = 1
    while p < n:
        k = p
        while k >= 1:
            for j in range(k % p, n - k, 2 * k):
                for i in range(min(k, n - j - k)):
                    if (i + j) // (2 * p) == (i + j + k) // (2 * p):
                        pairs.append((i + j, i + j + k))
            k //= 2
        p *= 2
    return pairs


SUBLANES = 8


def _top_values(x, k):
    n_groups = x.shape[0] // SUBLANES
    assert n_groups >= k
    g = [x[SUBLANES * j:SUBLANES * (j + 1), :] for j in range(n_groups)]
    for i, j in _sorting_network(n_groups):
        g[i], g[j] = jnp.maximum(g[i], g[j]), jnp.minimum(g[i], g[j])
    sub = lax.broadcasted_iota(jnp.int32, g[0].shape, 0)
    vals = []
    for r in range(k):
        m = jnp.max(g[0], axis=0, keepdims=True)
        vals.append(m)
        first = jnp.min(jnp.where(g[0] == m, sub, SUBLANES), axis=0, keepdims=True)
        hit = sub == first
        for j in range(k - 1 - r):
            g[j] = jnp.where(hit, g[j + 1], g[j])
    return jnp.concatenate(vals, axis=0)


def _staircase(a1, a2, combine, fill):
    r8 = lax.broadcasted_iota(jnp.int32, (8, a1.shape[1]), 0)
    r16 = lax.broadcasted_iota(jnp.int32, a1.shape, 0)
    lo8 = lambda x: x[0:8]
    return jnp.concatenate([
        combine(a1[0:1], a2),
        combine(a1[1:2], lo8(a2)),
        jnp.where(r8 < 5, combine(a1[2:3], lo8(a2)), fill),
        jnp.where(r8 < 4, combine(a1[3:4], lo8(a2)), fill),
        jnp.where(r16 >= 4, combine(a1, a2[0:1]), fill),
        jnp.where(r8 >= 4, combine(lo8(a1), a2[1:2]), fill),
        jnp.where(r8 == 4, combine(lo8(a1), a2[2:3]), fill),
    ], axis=0)


def _peer_route_body(q_ref, keys_ref, g_ref, cnt_ref, e1_ref, rank_ref, e2_ref):
    tm = q_ref.shape[0]
    k = PEER_TOPK
    for h in range(PEER_HEADS):
        sts, tops = [], []
        for p in range(2):
            hp = 2 * h + p
            qs = q_ref[:, hp * PEER_HALF:(hp + 1) * PEER_HALF]
            st = lax.dot_general(keys_ref[hp], qs, NT_DIMS, preferred_element_type=F32,
                                 precision=lax.Precision.HIGHEST)
            sts.append(st)
            tops.append(_top_values(st, k))
        a1, a2 = tops
        m1, m2 = a1[0:1], a2[0:1]
        cand = _staircase(a1, a2, jnp.add, -jnp.inf)
        prob = _staircase(jnp.exp(a1 - m1), jnp.exp(a2 - m2), jnp.multiply, 0.0)
        x = cand
        left = jnp.full((1, tm), float(k), F32)
        tau = jnp.full((1, tm), jnp.inf, F32)
        for _ in range(k):
            m = jnp.max(x, axis=0, keepdims=True)
            eq = x == m
            tau = jnp.where(left > 0, m, tau)
            left = left - jnp.sum(jnp.where(eq, 1.0, 0.0), axis=0, keepdims=True)
            x = jnp.where(eq, -jnp.inf, x)
        z = jnp.sum(jnp.where(cand >= tau, prob, 0.0), axis=0, keepdims=True)
        cnt = jnp.zeros((PEER_KEYS, tm), F32)
        rank = jnp.zeros((PEER_KEYS, tm), F32)
        for b in range(k):
            cnt = cnt + jnp.where(sts[0] + a2[b:b + 1] >= tau, 1.0, 0.0)
            rank = rank + jnp.where(a2[b:b + 1] > sts[1], 1.0, 0.0)
        cnt_ref[h] = cnt
        e1_ref[h] = jnp.exp(sts[0] - m1) / z
        rank_ref[h] = rank.astype(BF16)
        e2_ref[h] = jnp.exp(sts[1] - m2).astype(BF16)

    def tile16(row_f32):
        packed = jnp.broadcast_to(row_f32, (16, tm)).astype(BF16)
        return jnp.concatenate([packed] * (PEER_KEYS // 16), axis=0)

    def build(i1, carry):
        gate = None
        for h in range(PEER_HEADS):
            cnt = tile16(cnt_ref[h, pl.ds(i1, 1), :])
            e1 = tile16(e1_ref[h, pl.ds(i1, 1), :])
            sel = jnp.where(rank_ref[h] < cnt, e1 * e2_ref[h], jnp.zeros((), BF16))
            gate = sel if gate is None else gate + sel
        g_ref[pl.ds(pl.multiple_of(i1 * PEER_KEYS, PEER_KEYS), PEER_KEYS), :] = gate
        return carry

    lax.fori_loop(0, PEER_KEYS, build, 0)


def _peer_route(q, sub_keys, *, tm):
    t = q.shape[0]
    assert t % tm == 0
    hp = 2 * PEER_HEADS
    keys = sub_keys.reshape(hp, PEER_KEYS, PEER_HALF)
    n_exp = PEER_KEYS * PEER_KEYS
    per_head = lambda dt: pltpu.VMEM((PEER_HEADS, PEER_KEYS, tm), dt)
    return pl.pallas_call(
        _peer_route_body,
        grid=(t // tm,),
        in_specs=[pl.BlockSpec((tm, hp * PEER_HALF), lambda i: (i, 0)),
                  pl.BlockSpec((hp, PEER_KEYS, PEER_HALF), lambda i: (0, 0, 0))],
        out_specs=pl.BlockSpec((n_exp, tm), lambda i: (0, i)),
        out_shape=jax.ShapeDtypeStruct((n_exp, t), BF16),
        scratch_shapes=[per_head(F32), per_head(F32), per_head(BF16), per_head(BF16)],
        compiler_params=_params("arbitrary"),
        name="peer_route",
    )(q, keys)


def _peer_dense_body(xt_ref, u_ref, v_ref, g_ref, o_ref):
    @pl.when(pl.program_id(1) == 0)
    def _():
        o_ref[...] = jnp.zeros_like(o_ref)

    act_t = jnp.dot(u_ref[...], xt_ref[...], preferred_element_type=F32)
    w_t = jax.nn.gelu(act_t) * g_ref[...].astype(F32)
    o_ref[...] += jnp.dot(w_t.T.astype(BF16), v_ref[...], preferred_element_type=F32)


def _peer_dense(h_t, u, v, gate_t, *, tm, te):
    d, t = h_t.shape
    n_exp = u.shape[0]
    assert t % tm == 0 and n_exp % te == 0
    once = dict(pipeline_mode=pl.Buffered(1))
    return pl.pallas_call(
        _peer_dense_body,
        grid=(t // tm, n_exp // te),
        in_specs=[pl.BlockSpec((d, tm), lambda i, e: (0, i), **once),
                  pl.BlockSpec((te, d), lambda i, e: (e, 0)),
                  pl.BlockSpec((te, d), lambda i, e: (e, 0)),
                  pl.BlockSpec((te, tm), lambda i, e: (e, i))],
        out_specs=pl.BlockSpec((tm, d), lambda i, e: (i, 0), **once),
        out_shape=jax.ShapeDtypeStruct((t, d), F32),
        compiler_params=_params("arbitrary", "arbitrary"),
        name="peer_dense",
    )(h_t, u, v, gate_t)


MM_TM, MM_TN = 1024, 512
ROW_TM = 256
DIL_BLOCKS_PER_STEP = 4
PEER_TM, PEER_TE = 1024, 512
ROUTE_TM = 256


def kernel(x, mem, positions, w_in, gm_ln_g, gm_ln_b, gm_w_s, gm_b_s, w_mem_kv, w_gate, b_gate,
           w_br_gmlp, w_br_dil, w_br_mem, w_out, ln_mix_g, ln_mix_b, peer_w_query, peer_sub_keys,
           peer_u, peer_v, ln_ffn_g, ln_ffn_b):
    batch, seq, d = x.shape
    t = batch * seq
    assert w_in.shape[0] == DEPTH and w_in.shape[2] == IN_WIDTH
    x2 = x.reshape(t, d)
    x_bf = x2.astype(BF16)
    mem_bf = mem.reshape(-1, d).astype(BF16)

    proj = _matmul(x_bf, w_in.reshape(d, IN_WIDTH), out_dtype=BF16, tm=MM_TM, tn=MM_TN,
                   name="in_proj")
    gates = _matmul(x_bf, w_gate.reshape(d, 3 * d), bias=b_gate.reshape(3 * d), act="sigmoid",
                    out_dtype=BF16, tm=MM_TM, tn=MM_TN, name="gate_proj")
    mkv = _matmul(mem_bf, w_mem_kv.reshape(d, 2 * MEM_WIDTH), out_dtype=BF16, tm=MM_TM, tn=MM_TN,
                  name="mem_kv_proj")

    a_out = _gmlp(proj, gm_ln_g.reshape(-1), gm_ln_b.reshape(-1), gm_w_s.reshape(GM_GROUPS, BLOCK, BLOCK),
                  gm_b_s.reshape(GM_GROUPS, BLOCK), tm=512)

    cos, sin = _rope_table(positions, tm=1024)
    outs, lses = [], []
    for g, dil in enumerate(DILATIONS):
        qg, kg, vg = _rope_perm(proj, cos, sin, g, dil)
        o, lse = _dilated_attention(qg, kg, vg, dil, batch, seq, nblk=DIL_BLOCKS_PER_STEP)
        outs.append(o)
        lses.append(lse)
    b_out = _dil_merge(outs, lses)

    m_out = _mem_attention(proj, mkv, batch, seq, tm=512)

    merged = _merge(a_out, b_out, m_out, w_br_gmlp.reshape(GM_WIDTH, d),
                    w_br_dil.reshape(DIL_GROUP_WIDTH, d), w_br_mem.reshape(MEM_WIDTH, d), gates,
                    tm=MM_TM, tn=MM_TN)
    z1 = _matmul(merged, w_out.reshape(d, d), res=x2, alpha=ALPHA, out_dtype=F32, tm=MM_TM,
                 tn=MM_TN, name="out_proj")
    h1, h1_bf, h1_t = _layer_norm(z1, ln_mix_g.reshape(-1), ln_mix_b.reshape(-1), want_t=True,
                                  tm=ROW_TM, name="ln_mix")

    hq = PEER_HEADS * 2 * PEER_HALF
    q = _matmul(h1_bf, peer_w_query.reshape(d, hq), out_dtype=F32, tm=MM_TM, tn=MM_TN,
                name="peer_query")
    gate_t = _peer_route(q, peer_sub_keys, tm=ROUTE_TM)
    n_exp = PEER_KEYS * PEER_KEYS
    y = _peer_dense(h1_t, peer_u.reshape(n_exp, d).astype(BF16), peer_v.reshape(n_exp, d).astype(BF16),
                    gate_t, tm=PEER_TM, te=PEER_TE)
    (out,) = _layer_norm(h1, ln_ffn_g.reshape(-1), ln_ffn_b.reshape(-1), add=y, alpha=ALPHA,
                         tm=ROW_TM, name="ln_ffn")
    return out.reshape(batch, seq, d)
```

```python
import functools

import jax
import jax.numpy as jnp
from jax import lax
from jax.experimental import pallas as pl
from jax.experimental.pallas import tpu as pltpu

F32 = jnp.float32
BF16 = jnp.bfloat16

HEAD_DIM = 128
BLOCK = 128
GM_GROUPS = 12
GM_WIDTH = GM_GROUPS * 128
DILATIONS = (1, 4, 16)
DIL_HEADS = 4
DIL_GROUP_WIDTH = DIL_HEADS * HEAD_DIM
DIL_WIDTH = len(DILATIONS) * DIL_GROUP_WIDTH
MEM_HEADS = 4
MEM_HEAD_DIM = 256
MEM_WIDTH = MEM_HEADS * MEM_HEAD_DIM
PEER_HEADS = 8
PEER_KEYS = 128
PEER_TOPK = 16
PEER_HALF = 128
ROPE_THETA = 10000.0
LN_EPS = 1e-5
DEPTH = 1
ALPHA = (2 * DEPTH) ** 0.25
NEG_INF = -1e30

OFF_GU, OFF_GV = 0, GM_WIDTH
OFF_DQ = 2 * GM_WIDTH
OFF_DK = OFF_DQ + DIL_WIDTH
OFF_DV = OFF_DK + DIL_WIDTH
OFF_MQ = OFF_DV + DIL_WIDTH
IN_WIDTH = OFF_MQ + MEM_WIDTH

V7X_VMEM_BYTES = 64 * 1024 * 1024
VMEM_LIMIT = V7X_VMEM_BYTES - 8 * 1024 * 1024

NT_DIMS = (((1,), (1,)), ((), ()))
SIDE_ROWS = 128


def _params(*sem):
    return pltpu.CompilerParams(dimension_semantics=sem, vmem_limit_bytes=VMEM_LIMIT)


def _mm_body(*refs, act, alpha, has_bias, has_res, has_side):
    x_ref, w_ref = refs[0], refs[1]
    k = 2
    b_ref = r_ref = s_ref = None
    if has_bias:
        b_ref = refs[k]
        k += 1
    if has_res:
        r_ref = refs[k]
        k += 1
    if has_side:
        s_ref = refs[k]
        k += 1
    o_ref = refs[k]
    k += 1
    if has_side:
        refs[k][...] = s_ref[...].astype(BF16)
        k += 1
    wb_ref = refs[k]

    @pl.when(pl.program_id(1) == 0)
    def _():
        wb_ref[...] = w_ref[...].astype(BF16)

    acc = jnp.dot(x_ref[...], wb_ref[...], preferred_element_type=F32)
    if has_bias:
        acc = acc + b_ref[...]
    if act == "sigmoid":
        acc = jax.nn.sigmoid(acc)
    if has_res:
        acc = alpha * r_ref[...] + acc
    o_ref[...] = acc.astype(o_ref.dtype)


def _matmul(x, w, *, out_dtype, tm, tn, bias=None, res=None, alpha=1.0, act=None, side=None, name):
    m, k = x.shape
    n = w.shape[1]
    tm, tn = min(tm, m), min(tn, n)
    assert m % tm == 0 and n % tn == 0
    steps_i = m // tm
    in_specs = [pl.BlockSpec((tm, k), lambda j, i: (i, 0)),
                pl.BlockSpec((k, tn), lambda j, i: (0, j))]
    args = [x, w]
    if bias is not None:
        in_specs.append(pl.BlockSpec((1, tn), lambda j, i: (0, j)))
        args.append(bias.reshape(1, n))
    if res is not None:
        in_specs.append(pl.BlockSpec((tm, tn), lambda j, i: (i, j)))
        args.append(res)
    out_specs = [pl.BlockSpec((tm, tn), lambda j, i: (i, j))]
    out_shape = [jax.ShapeDtypeStruct((m, n), out_dtype)]
    if side is not None:
        rows, width = side.shape
        side_rows = SIDE_ROWS
        while rows // side_rows > (n // tn) * steps_i:
            side_rows *= 2
        n_blocks = rows // side_rows
        assert rows % side_rows == 0
        spec = pl.BlockSpec((side_rows, width),
                            lambda j, i: (jnp.minimum(j * steps_i + i, n_blocks - 1), 0))
        in_specs.append(spec)
        args.append(side)
        out_specs.append(spec)
        out_shape.append(jax.ShapeDtypeStruct(side.shape, BF16))
    body = functools.partial(_mm_body, act=act, alpha=alpha, has_bias=bias is not None,
                             has_res=res is not None, has_side=side is not None)
    outs = pl.pallas_call(
        body,
        grid=(n // tn, steps_i),
        in_specs=in_specs,
        out_specs=out_specs,
        out_shape=out_shape,
        scratch_shapes=[pltpu.VMEM((k, tn), BF16)],
        compiler_params=_params("arbitrary", "arbitrary"),
        name=name,
    )(*args)
    return outs if side is not None else outs[0]


def _ln_rows(z, g, b):
    mu = jnp.mean(z, axis=-1, keepdims=True)
    zc = z - mu
    var = jnp.mean(zc * zc, axis=-1, keepdims=True)
    return zc * lax.rsqrt(var + LN_EPS) * g + b


def _ln_body(*refs, alpha, has_add, want_t):
    k = 0
    z = refs[k][...]
    k += 1
    if has_add:
        z = alpha * z + refs[k][...]
        k += 1
    g_ref, b_ref = refs[k], refs[k + 1]
    k += 2
    h = _ln_rows(z, g_ref[...], b_ref[...])
    refs[k][...] = h
    if want_t:
        refs[k + 1][...] = h.astype(BF16)
        refs[k + 2][...] = h.T.astype(BF16)


def _layer_norm(z, g, b, *, add=None, alpha=1.0, want_t=False, tm, name):
    t, d = z.shape
    tm = min(tm, t)
    assert t % tm == 0
    row = pl.BlockSpec((tm, d), lambda i: (i, 0))
    vec = pl.BlockSpec((1, d), lambda i: (0, 0))
    in_specs, args = [row], [z]
    if add is not None:
        in_specs.append(row)
        args.append(add)
    in_specs += [vec, vec]
    args += [g.reshape(1, d), b.reshape(1, d)]
    out_specs, out_shape = [row], [jax.ShapeDtypeStruct((t, d), F32)]
    if want_t:
        out_specs += [row, pl.BlockSpec((d, tm), lambda i: (0, i))]
        out_shape += [jax.ShapeDtypeStruct((t, d), BF16), jax.ShapeDtypeStruct((d, t), BF16)]
    return pl.pallas_call(
        functools.partial(_ln_body, alpha=alpha, has_add=add is not None, want_t=want_t),
        grid=(t // tm,),
        in_specs=in_specs,
        out_specs=out_specs,
        out_shape=out_shape,
        compiler_params=_params("arbitrary"),
        name=name,
    )(*args)


def _gmlp_body(gu_ref, gv_ref, g_ref, b_ref, ws_ref, bs_ref, o_ref):
    tm = gu_ref.shape[0]
    u = jax.nn.gelu(gu_ref[...].astype(F32))
    v = _ln_rows(jax.nn.gelu(gv_ref[...].astype(F32)), g_ref[...], b_ref[...]).astype(BF16)
    causal = (lax.broadcasted_iota(jnp.int32, (BLOCK, BLOCK), 0)
              >= lax.broadcasted_iota(jnp.int32, (BLOCK, BLOCK), 1))
    for g in range(GM_GROUPS):
        cols = slice(g * 128, (g + 1) * 128)
        w = jnp.where(causal, ws_ref[g], 0.0).astype(BF16)
        for c in range(tm // BLOCK):
            rows = slice(c * BLOCK, (c + 1) * BLOCK)
            mixed = jnp.dot(w, v[rows, cols], preferred_element_type=F32) + bs_ref[g]
            o_ref[rows, cols] = (u[rows, cols] * mixed).astype(o_ref.dtype)


def _gmlp(proj, ln_g, ln_b, w_s, b_s, *, tm):
    t = proj.shape[0]
    assert t % tm == 0 and tm % BLOCK == 0
    bs_full = jnp.broadcast_to(b_s[:, :, None], (GM_GROUPS, BLOCK, 128))
    return pl.pallas_call(
        _gmlp_body,
        grid=(t // tm,),
        in_specs=[pl.BlockSpec((tm, GM_WIDTH), lambda i: (i, OFF_GU // GM_WIDTH)),
                  pl.BlockSpec((tm, GM_WIDTH), lambda i: (i, OFF_GV // GM_WIDTH)),
                  pl.BlockSpec((1, GM_WIDTH), lambda i: (0, 0)),
                  pl.BlockSpec((1, GM_WIDTH), lambda i: (0, 0)),
                  pl.BlockSpec((GM_GROUPS, BLOCK, BLOCK), lambda i: (0, 0, 0)),
                  pl.BlockSpec((GM_GROUPS, BLOCK, 128), lambda i: (0, 0, 0))],
        out_specs=pl.BlockSpec((tm, GM_WIDTH), lambda i: (i, 0)),
        out_shape=jax.ShapeDtypeStruct((t, GM_WIDTH), BF16),
        compiler_params=_params("arbitrary"),
        name="gmlp",
    )(proj, proj, ln_g.reshape(1, GM_WIDTH), ln_b.reshape(1, GM_WIDTH), w_s, bs_full)


PERM_TILE = BLOCK * max(DILATIONS)


def _rope_table_body(pos_ref, inv_ref, cos_ref, sin_ref):
    half = HEAD_DIM // 2
    ang = pos_ref[...] * inv_ref[...]
    lane = lax.broadcasted_iota(jnp.int32, ang.shape, 1)
    sin = jnp.sin(ang)
    cos_ref[...] = jnp.cos(ang)
    sin_ref[...] = jnp.where(lane < half, -sin, sin)


def _rope_table(positions, *, tm):
    t = positions.size
    assert t % tm == 0
    half = HEAD_DIM // 2
    inv_freq = ROPE_THETA ** (-jnp.arange(half, dtype=F32) / half)
    inv = jnp.concatenate([inv_freq, inv_freq]).reshape(1, HEAD_DIM)
    pos = positions.astype(F32).reshape(t, 1)
    out = jax.ShapeDtypeStruct((t, HEAD_DIM), F32)
    return pl.pallas_call(
        _rope_table_body,
        grid=(t // tm,),
        in_specs=[pl.BlockSpec((tm, 1), lambda i: (i, 0)),
                  pl.BlockSpec((1, HEAD_DIM), lambda i: (0, 0))],
        out_specs=[pl.BlockSpec((tm, HEAD_DIM), lambda i: (i, 0))] * 2,
        out_shape=[out, out],
        compiler_params=_params("arbitrary"),
        name="rope_table",
    )(pos, inv)


def _rope_perm_body(q_ref, k_ref, v_ref, cos_ref, sin_ref, qo_ref, ko_ref, vo_ref, buf_ref, *, d):
    tm = q_ref.shape[0]
    half = HEAD_DIM // 2
    span = BLOCK * d
    for h in range(DIL_HEADS):
        cols = slice(h * HEAD_DIM, (h + 1) * HEAD_DIM)
        for src, dst, rotate in ((q_ref, qo_ref, True), (k_ref, ko_ref, True), (v_ref, vo_ref, False)):
            t = src[:, cols].astype(F32)
            if rotate:
                swapped = jnp.concatenate([t[:, half:], t[:, :half]], axis=1)
                t = t * cos_ref[...] + swapped * sin_ref[...]
            if d == 1:
                dst[:, cols] = t.astype(dst.dtype)
                continue
            buf_ref[...] = t
            for n in range(tm // span):
                for r in range(d):
                    rows = buf_ref[pl.ds(n * span + r, BLOCK, stride=d), :]
                    dst[n * span + r * BLOCK:n * span + (r + 1) * BLOCK, cols] = rows.astype(dst.dtype)


def _rope_perm(proj, cos, sin, group, d):
    t = proj.shape[0]
    tm, gw = PERM_TILE, DIL_GROUP_WIDTH
    assert t % tm == 0 and tm % (BLOCK * d) == 0
    col = lambda off: pl.BlockSpec((tm, gw), lambda i: (i, off // gw + group))
    tab = pl.BlockSpec((tm, HEAD_DIM), lambda i: (i, 0))
    out = jax.ShapeDtypeStruct((t, gw), BF16)
    return pl.pallas_call(
        functools.partial(_rope_perm_body, d=d),
        grid=(t // tm,),
        in_specs=[col(OFF_DQ), col(OFF_DK), col(OFF_DV), tab, tab],
        out_specs=[pl.BlockSpec((tm, gw), lambda i: (i, 0))] * 3,
        out_shape=[out, out, out],
        scratch_shapes=[pltpu.VMEM((tm, HEAD_DIM), F32)],
        compiler_params=_params("arbitrary"),
        name=f"rope_perm_d{d}",
    )(proj, proj, proj, cos, sin)


def _dil_body(q_ref, kc_ref, vc_ref, kp_ref, vp_ref, o_ref, lse_ref, *, d, nblk):
    step = pl.program_id(1)
    qi = lax.broadcasted_iota(jnp.int32, (BLOCK, BLOCK), 0)
    ki = lax.broadcasted_iota(jnp.int32, (BLOCK, BLOCK), 1)
    mask_cur = ki <= qi
    scale = HEAD_DIM ** -0.5
    for i in range(nblk):
        rows = slice(i * BLOCK, (i + 1) * BLOCK)
        if d >= nblk:
            kpr, vpr, prow, has_prev = kp_ref, vp_ref, rows, step >= d // nblk
        elif i == 0:
            kpr, vpr, prow, has_prev = kp_ref, vp_ref, slice(0, BLOCK), step > 0
        else:
            kpr, vpr, prow, has_prev = kc_ref, vc_ref, slice((i - 1) * BLOCK, i * BLOCK), None
        mask_prev = ki >= (qi if has_prev is None else qi + jnp.where(has_prev, 0, BLOCK))
        mask = jnp.concatenate([mask_prev, mask_cur], axis=1)
        for h in range(DIL_HEADS):
            cols = slice(h * HEAD_DIM, (h + 1) * HEAD_DIM)
            keys = jnp.concatenate([kpr[prow, cols], kc_ref[rows, cols]], axis=0)
            vals = jnp.concatenate([vpr[prow, cols], vc_ref[rows, cols]], axis=0)
            s = lax.dot_general(q_ref[rows, cols], keys, NT_DIMS, preferred_element_type=F32)
            s = jnp.where(mask, s * scale, NEG_INF)
            m = jnp.max(s, axis=-1, keepdims=True)
            p = jnp.exp(s - m).astype(BF16)
            v_aug = jnp.concatenate([vals, jnp.ones_like(vals)], axis=1)
            o_aug = jnp.dot(p, v_aug, preferred_element_type=F32)
            den = o_aug[:, HEAD_DIM:]
            o_ref[rows, cols] = o_aug[:, :HEAD_DIM] / den
            lse_ref[rows, cols] = m + jnp.log(den)


def _dilated_attention(q, k, v, d, batch, seq, *, nblk):
    gw = DIL_GROUP_WIDTH
    rows = nblk * BLOCK
    assert seq % rows == 0 and (d == 1 or d % nblk == 0)
    steps = seq // rows
    cur = pl.BlockSpec((rows, gw), lambda b, s: (b * steps + s, 0))
    if d >= nblk:
        back = d // nblk
        prev = pl.BlockSpec((rows, gw), lambda b, s: (b * steps + jnp.maximum(s - back, 0), 0))
    else:
        prev = pl.BlockSpec((BLOCK, gw),
                            lambda b, s: (b * steps * nblk + jnp.maximum(s * nblk - 1, 0), 0))
    out = jax.ShapeDtypeStruct((batch * seq, gw), F32)
    return pl.pallas_call(
        functools.partial(_dil_body, d=d, nblk=nblk),
        grid=(batch, steps),
        in_specs=[cur, cur, cur, prev, prev],
        out_specs=[cur, cur],
        out_shape=[out, out],
        compiler_params=_params("arbitrary", "arbitrary"),
        name=f"dil_attn_d{d}",
    )(q, k, v, k, v)


def _dil_merge_body(*refs):
    n = len(DILATIONS)
    o_refs, l_refs, out_ref, bufs = refs[:n], refs[n:2 * n], refs[2 * n], refs[2 * n + 1:]
    tm = out_ref.shape[0]
    outs, lses = [], []
    k = 0
    for d, o_ref, l_ref in zip(DILATIONS, o_refs, l_refs):
        if d == 1:
            outs.append(o_ref[...])
            lses.append(l_ref[...])
            continue
        span = BLOCK * d
        for src, buf in ((o_ref, bufs[k]), (l_ref, bufs[k + 1])):
            for s in range(tm // span):
                for r in range(d):
                    buf[pl.ds(s * span + r, BLOCK, stride=d), :] = (
                        src[s * span + r * BLOCK:s * span + (r + 1) * BLOCK, :])
        outs.append(bufs[k][...])
        lses.append(bufs[k + 1][...])
        k += 2
    m = functools.reduce(jnp.maximum, lses)
    es = [jnp.exp(l - m) for l in lses]
    num = sum(e * o for e, o in zip(es, outs))
    out_ref[...] = (num / sum(es)).astype(out_ref.dtype)


def _dil_merge(outs, lses):
    t, w = outs[0].shape
    tm = PERM_TILE
    assert t % tm == 0
    spec = pl.BlockSpec((tm, HEAD_DIM), lambda i, h: (i, h))
    n_buf = 2 * sum(1 for d in DILATIONS if d != 1)
    return pl.pallas_call(
        _dil_merge_body,
        grid=(t // tm, w // HEAD_DIM),
        in_specs=[spec] * (2 * len(DILATIONS)),
        out_specs=spec,
        out_shape=jax.ShapeDtypeStruct((t, w), BF16),
        scratch_shapes=[pltpu.VMEM((tm, HEAD_DIM), F32)] * n_buf,
        compiler_params=_params("arbitrary", "arbitrary"),
        name="dil_merge",
    )(*outs, *lses)


def _mem_body(q0, q1, q2, q3, kv_ref, o_ref):
    scale = MEM_HEAD_DIM ** -0.5
    for h, q_ref in enumerate((q0, q1, q2, q3)):
        cols = slice(h * MEM_HEAD_DIM, (h + 1) * MEM_HEAD_DIM)
        k = kv_ref[:, cols]
        v = kv_ref[:, MEM_WIDTH + h * MEM_HEAD_DIM:MEM_WIDTH + (h + 1) * MEM_HEAD_DIM]
        s = lax.dot_general(q_ref[...], k, NT_DIMS, preferred_element_type=F32) * scale
        m = jnp.max(s, axis=-1, keepdims=True)
        p = jnp.exp(s - m)
        den = jnp.sum(p, axis=-1, keepdims=True)
        o = jnp.dot(p.astype(BF16), v, preferred_element_type=F32) / den
        o_ref[:, cols] = o.astype(o_ref.dtype)


def _mem_attention(proj, mkv, batch, seq, *, tm):
    mem_len = mkv.shape[0] // batch
    assert seq % tm == 0
    steps = seq // tm
    q_specs = [pl.BlockSpec((tm, MEM_HEAD_DIM),
                            lambda b, i, h=h: (b * steps + i, OFF_MQ // MEM_HEAD_DIM + h))
               for h in range(MEM_HEADS)]
    return pl.pallas_call(
        _mem_body,
        grid=(batch, steps),
        in_specs=q_specs + [pl.BlockSpec((mem_len, 2 * MEM_WIDTH), lambda b, i: (b, 0))],
        out_specs=pl.BlockSpec((tm, MEM_WIDTH), lambda b, i: (b * steps + i, 0)),
        out_shape=jax.ShapeDtypeStruct((batch * seq, MEM_WIDTH), BF16),
        compiler_params=_params("arbitrary", "arbitrary"),
        name="mem_attn",
    )(proj, proj, proj, proj, mkv)


def _merge_body(a_ref, b_ref, m_ref, wa_ref, wb_ref, wm_ref, g0_ref, g1_ref, g2_ref, o_ref,
                wa_bf, wb_bf, wm_bf):
    @pl.when(pl.program_id(1) == 0)
    def _():
        wa_bf[...] = wa_ref[...].astype(BF16)
        wb_bf[...] = wb_ref[...].astype(BF16)
        wm_bf[...] = wm_ref[...].astype(BF16)

    ya = jnp.dot(a_ref[...], wa_bf[...], preferred_element_type=F32)
    yb = jnp.dot(b_ref[...], wb_bf[...], preferred_element_type=F32)
    ym = jnp.dot(m_ref[...], wm_bf[...], preferred_element_type=F32)
    o = (g0_ref[...].astype(F32) * ya + g1_ref[...].astype(F32) * yb
         + g2_ref[...].astype(F32) * ym)
    o_ref[...] = o.astype(o_ref.dtype)


def _merge(a_out, b_out, m_out, w_a, w_b, w_m, gates, *, tm, tn):
    t = a_out.shape[0]
    d = w_a.shape[1]
    assert t % tm == 0 and d % tn == 0
    nj = d // tn
    rows = lambda w: pl.BlockSpec((tm, w), lambda j, i: (i, 0))
    wcol = lambda k: pl.BlockSpec((k, tn), lambda j, i: (0, j))
    gate = lambda br: pl.BlockSpec((tm, tn), lambda j, i, br=br: (i, br * nj + j))
    return pl.pallas_call(
        _merge_body,
        grid=(nj, t // tm),
        in_specs=[rows(GM_WIDTH), rows(DIL_GROUP_WIDTH), rows(MEM_WIDTH),
                  wcol(GM_WIDTH), wcol(DIL_GROUP_WIDTH), wcol(MEM_WIDTH),
                  gate(0), gate(1), gate(2)],
        out_specs=pl.BlockSpec((tm, tn), lambda j, i: (i, j)),
        out_shape=jax.ShapeDtypeStruct((t, d), BF16),
        scratch_shapes=[pltpu.VMEM((GM_WIDTH, tn), BF16), pltpu.VMEM((DIL_GROUP_WIDTH, tn), BF16),
                        pltpu.VMEM((MEM_WIDTH, tn), BF16)],
        compiler_params=_params("arbitrary", "arbitrary"),
        name="branch_merge",
    )(a_out, b_out, m_out, w_a, w_b, w_m, gates, gates, gates)


def _sorting_network(n):
    pairs = []
    p = 1
    while p < n:
        k = p
        while k >= 1:
            for j in range(k % p, n - k, 2 * k):
                for i in range(min(k, n - j - k)):
                    if (i + j) // (2 * p) == (i + j + k) // (2 * p):
                        pairs.append((i + j, i + j + k))
            k //= 2
        p *= 2
    return pairs


SUBLANES = 8


def _top_values(x, k):
    n_groups = x.shape[0] // SUBLANES
    assert n_groups >= k
    g = [x[SUBLANES * j:SUBLANES * (j + 1), :] for j in range(n_groups)]
    for i, j in _sorting_network(n_groups):
        g[i], g[j] = jnp.maximum(g[i], g[j]), jnp.minimum(g[i], g[j])
    sub = lax.broadcasted_iota(jnp.int32, g[0].shape, 0)
    vals = []
    for r in range(k):
        m = jnp.max(g[0], axis=0, keepdims=True)
        vals.append(m)
        first = jnp.min(jnp.where(g[0] == m, sub, SUBLANES), axis=0, keepdims=True)
        hit = sub == first
        for j in range(k - 1 - r):
            g[j] = jnp.where(hit, g[j + 1], g[j])
    return jnp.concatenate(vals, axis=0)


def _staircase(a1, a2, combine, fill):
    r8 = lax.broadcasted_iota(jnp.int32, (8, a1.shape[1]), 0)
    r16 = lax.broadcasted_iota(jnp.int32, a1.shape, 0)
    lo8 = lambda x: x[0:8]
    return jnp.concatenate([
        combine(a1[0:1], a2),
        combine(a1[1:2], lo8(a2)),
        jnp.where(r8 < 5, combine(a1[2:3], lo8(a2)), fill),
        jnp.where(r8 < 4, combine(a1[3:4], lo8(a2)), fill),
        jnp.where(r16 >= 4, combine(a1, a2[0:1]), fill),
        jnp.where(r8 >= 4, combine(lo8(a1), a2[1:2]), fill),
        jnp.where(r8 == 4, combine(lo8(a1), a2[2:3]), fill),
    ], axis=0)


def _peer_route_body(q_ref, keys_ref, g_ref, cnt_ref, e1_ref, rank_ref, e2_ref):
    tm = q_ref.shape[0]
    k = PEER_TOPK
    for h in range(PEER_HEADS):
        sts, tops = [], []
        for p in range(2):
            hp = 2 * h + p
            qs = q_ref[:, hp * PEER_HALF:(hp + 1) * PEER_HALF]
            st = lax.dot_general(keys_ref[hp], qs, NT_DIMS, preferred_element_type=F32,
                                 precision=lax.Precision.HIGHEST)
            sts.append(st)
            tops.append(_top_values(st, k))
        a1, a2 = tops
        m1, m2 = a1[0:1], a2[0:1]
        cand = _staircase(a1, a2, jnp.add, -jnp.inf)
        prob = _staircase(jnp.exp(a1 - m1), jnp.exp(a2 - m2), jnp.multiply, 0.0)
        x = cand
        left = jnp.full((1, tm), float(k), F32)
        tau = jnp.full((1, tm), jnp.inf, F32)
        for _ in range(k):
            m = jnp.max(x, axis=0, keepdims=True)
            eq = x == m
            tau = jnp.where(left > 0, m, tau)
            left = left - jnp.sum(jnp.where(eq, 1.0, 0.0), axis=0, keepdims=True)
            x = jnp.where(eq, -jnp.inf, x)
        z = jnp.sum(jnp.where(cand >= tau, prob, 0.0), axis=0, keepdims=True)
        cnt = jnp.zeros((PEER_KEYS, tm), F32)
        rank = jnp.zeros((PEER_KEYS, tm), F32)
        for b in range(k):
            cnt = cnt + jnp.where(sts[0] + a2[b:b + 1] >= tau, 1.0, 0.0)
            rank = rank + jnp.where(a2[b:b + 1] > sts[1], 1.0, 0.0)
        cnt_ref[h] = cnt
        e1_ref[h] = jnp.exp(sts[0] - m1) / z
        rank_ref[h] = rank.astype(BF16)
        e2_ref[h] = jnp.exp(sts[1] - m2).astype(BF16)

    def tile16(row_f32):
        packed = jnp.broadcast_to(row_f32, (16, tm)).astype(BF16)
        return jnp.concatenate([packed] * (PEER_KEYS // 16), axis=0)

    def build(i1, carry):
        gate = None
        for h in range(PEER_HEADS):
            cnt = tile16(cnt_ref[h, pl.ds(i1, 1), :])
            e1 = tile16(e1_ref[h, pl.ds(i1, 1), :])
            sel = jnp.where(rank_ref[h] < cnt, e1 * e2_ref[h], jnp.zeros((), BF16))
            gate = sel if gate is None else gate + sel
        g_ref[pl.ds(pl.multiple_of(i1 * PEER_KEYS, PEER_KEYS), PEER_KEYS), :] = gate
        return carry

    lax.fori_loop(0, PEER_KEYS, build, 0)


def _peer_route(q, sub_keys, *, tm):
    t = q.shape[0]
    assert t % tm == 0
    hp = 2 * PEER_HEADS
    keys = sub_keys.reshape(hp, PEER_KEYS, PEER_HALF)
    n_exp = PEER_KEYS * PEER_KEYS
    per_head = lambda dt: pltpu.VMEM((PEER_HEADS, PEER_KEYS, tm), dt)
    return pl.pallas_call(
        _peer_route_body,
        grid=(t // tm,),
        in_specs=[pl.BlockSpec((tm, hp * PEER_HALF), lambda i: (i, 0)),
                  pl.BlockSpec((hp, PEER_KEYS, PEER_HALF), lambda i: (0, 0, 0))],
        out_specs=pl.BlockSpec((n_exp, tm), lambda i: (0, i)),
        out_shape=jax.ShapeDtypeStruct((n_exp, t), BF16),
        scratch_shapes=[per_head(F32), per_head(F32), per_head(BF16), per_head(BF16)],
        compiler_params=_params("arbitrary"),
        name="peer_route",
    )(q, keys)


def _peer_dense_body(xt_ref, u_ref, v_ref, g_ref, o_ref):
    @pl.when(pl.program_id(1) == 0)
    def _():
        o_ref[...] = jnp.zeros_like(o_ref)

    act_t = jnp.dot(u_ref[...], xt_ref[...], preferred_element_type=F32)
    w_t = jax.nn.gelu(act_t) * g_ref[...].astype(F32)
    o_ref[...] += jnp.dot(w_t.T.astype(BF16), v_ref[...], preferred_element_type=F32)


def _peer_dense(h_t, u, v, gate_t, *, tm, te):
    d, t = h_t.shape
    n_exp = u.shape[0]
    assert t % tm == 0 and n_exp % te == 0
    once = dict(pipeline_mode=pl.Buffered(1))
    return pl.pallas_call(
        _peer_dense_body,
        grid=(t // tm, n_exp // te),
        in_specs=[pl.BlockSpec((d, tm), lambda i, e: (0, i), **once),
                  pl.BlockSpec((te, d), lambda i, e: (e, 0)),
                  pl.BlockSpec((te, d), lambda i, e: (e, 0)),
                  pl.BlockSpec((te, tm), lambda i, e: (e, i))],
        out_specs=pl.BlockSpec((tm, d), lambda i, e: (i, 0), **once),
        out_shape=jax.ShapeDtypeStruct((t, d), F32),
        compiler_params=_params("arbitrary", "arbitrary"),
        name="peer_dense",
    )(h_t, u, v, gate_t)


MM_TM, MM_TN = 1024, 512
ROW_TM = 256
DIL_BLOCKS_PER_STEP = 4
PEER_TM, PEER_TE = 1024, 512
ROUTE_TM = 256


def kernel(x, mem, positions, w_in, gm_ln_g, gm_ln_b, gm_w_s, gm_b_s, w_mem_kv, w_gate, b_gate,
           w_br_gmlp, w_br_dil, w_br_mem, w_out, ln_mix_g, ln_mix_b, peer_w_query, peer_sub_keys,
           peer_u, peer_v, ln_ffn_g, ln_ffn_b):
    batch, seq, d = x.shape
    t = batch * seq
    assert w_in.shape[0] == DEPTH and w_in.shape[2] == IN_WIDTH
    x2 = x.reshape(t, d)
    x_bf = x2.astype(BF16)
    mem_bf = mem.reshape(-1, d).astype(BF16)

    n_exp = PEER_KEYS * PEER_KEYS
    proj, v_bf = _matmul(x_bf, w_in.reshape(d, IN_WIDTH), out_dtype=BF16, tm=MM_TM, tn=MM_TN,
                         side=peer_v.reshape(n_exp, d), name="in_proj")
    gates, u_bf = _matmul(x_bf, w_gate.reshape(d, 3 * d), bias=b_gate.reshape(3 * d),
                          act="sigmoid", out_dtype=BF16, tm=MM_TM, tn=MM_TN,
                          side=peer_u.reshape(n_exp, d), name="gate_proj")
    mkv = _matmul(mem_bf, w_mem_kv.reshape(d, 2 * MEM_WIDTH), out_dtype=BF16, tm=MM_TM, tn=MM_TN,
                  name="mem_kv_proj")

    a_out = _gmlp(proj, gm_ln_g.reshape(-1), gm_ln_b.reshape(-1), gm_w_s.reshape(GM_GROUPS, BLOCK, BLOCK),
                  gm_b_s.reshape(GM_GROUPS, BLOCK), tm=512)

    cos, sin = _rope_table(positions, tm=1024)
    outs, lses = [], []
    for g, dil in enumerate(DILATIONS):
        qg, kg, vg = _rope_perm(proj, cos, sin, g, dil)
        o, lse = _dilated_attention(qg, kg, vg, dil, batch, seq, nblk=DIL_BLOCKS_PER_STEP)
        outs.append(o)
        lses.append(lse)
    b_out = _dil_merge(outs, lses)

    m_out = _mem_attention(proj, mkv, batch, seq, tm=512)

    merged = _merge(a_out, b_out, m_out, w_br_gmlp.reshape(GM_WIDTH, d),
                    w_br_dil.reshape(DIL_GROUP_WIDTH, d), w_br_mem.reshape(MEM_WIDTH, d), gates,
                    tm=MM_TM, tn=MM_TN)
    z1 = _matmul(merged, w_out.reshape(d, d), res=x2, alpha=ALPHA, out_dtype=F32, tm=MM_TM,
                 tn=MM_TN, name="out_proj")
    h1, h1_bf, h1_t = _layer_norm(z1, ln_mix_g.reshape(-1), ln_mix_b.reshape(-1), want_t=True,
                                  tm=ROW_TM, name="ln_mix")

    hq = PEER_HEADS * 2 * PEER_HALF
    q = _matmul(h1_bf, peer_w_query.reshape(d, hq), out_dtype=F32, tm=MM_TM, tn=MM_TN,
                name="peer_query")
    gate_t = _peer_route(q, peer_sub_keys, tm=ROUTE_TM)
    y = _peer_dense(h1_t, u_bf, v_bf, gate_t, tm=PEER_TM, te=PEER_TE)
    (out,) = _layer_norm(h1, ln_ffn_g.reshape(-1), ln_ffn_b.reshape(-1), add=y, alpha=ALPHA,
                         tm=ROW_TM, name="ln_ffn")
    return out.reshape(batch, seq, d)
```

```python
import functools

import jax
import jax.numpy as jnp
from jax import lax
from jax.experimental import pallas as pl
from jax.experimental.pallas import tpu as pltpu

F32 = jnp.float32
BF16 = jnp.bfloat16

HEAD_DIM = 128
BLOCK = 128
GM_GROUPS = 12
GM_WIDTH = GM_GROUPS * 128
DILATIONS = (1, 4, 16)
DIL_HEADS = 4
DIL_GROUP_WIDTH = DIL_HEADS * HEAD_DIM
DIL_WIDTH = len(DILATIONS) * DIL_GROUP_WIDTH
MEM_HEADS = 4
MEM_HEAD_DIM = 256
MEM_WIDTH = MEM_HEADS * MEM_HEAD_DIM
PEER_HEADS = 8
PEER_KEYS = 128
PEER_TOPK = 16
PEER_HALF = 128
ROPE_THETA = 10000.0
LN_EPS = 1e-5
DEPTH = 1
ALPHA = (2 * DEPTH) ** 0.25
NEG_INF = -1e30

OFF_GU, OFF_GV = 0, GM_WIDTH
OFF_DQ = 2 * GM_WIDTH
OFF_DK = OFF_DQ + DIL_WIDTH
OFF_DV = OFF_DK + DIL_WIDTH
OFF_MQ = OFF_DV + DIL_WIDTH
IN_WIDTH = OFF_MQ + MEM_WIDTH

V7X_VMEM_BYTES = 64 * 1024 * 1024
VMEM_LIMIT = V7X_VMEM_BYTES - 8 * 1024 * 1024

NT_DIMS = (((1,), (1,)), ((), ()))
SIDE_ROWS = 128


def _params(*sem):
    return pltpu.CompilerParams(dimension_semantics=sem, vmem_limit_bytes=VMEM_LIMIT)


def _mm_body(*refs, act, alpha, has_bias, has_res, has_side):
    x_ref, w_ref = refs[0], refs[1]
    k = 2
    b_ref = r_ref = s_ref = None
    if has_bias:
        b_ref = refs[k]
        k += 1
    if has_res:
        r_ref = refs[k]
        k += 1
    if has_side:
        s_ref = refs[k]
        k += 1
    o_ref = refs[k]
    k += 1
    if has_side:
        refs[k][...] = s_ref[...].astype(BF16)
        k += 1
    wb_ref = refs[k]

    @pl.when(pl.program_id(1) == 0)
    def _():
        wb_ref[...] = w_ref[...].astype(BF16)

    acc = jnp.dot(x_ref[...], wb_ref[...], preferred_element_type=F32)
    if has_bias:
        acc = acc + b_ref[...]
    if act == "sigmoid":
        acc = 0.5 * jnp.tanh(0.5 * acc) + 0.5
    if has_res:
        acc = alpha * r_ref[...] + acc
    o_ref[...] = acc.astype(o_ref.dtype)


def _matmul(x, w, *, out_dtype, tm, tn, bias=None, res=None, alpha=1.0, act=None, side=None, name):
    m, k = x.shape
    n = w.shape[1]
    tm, tn = min(tm, m), min(tn, n)
    assert m % tm == 0 and n % tn == 0
    steps_i = m // tm
    in_specs = [pl.BlockSpec((tm, k), lambda j, i: (i, 0)),
                pl.BlockSpec((k, tn), lambda j, i: (0, j))]
    args = [x, w]
    if bias is not None:
        in_specs.append(pl.BlockSpec((1, tn), lambda j, i: (0, j)))
        args.append(bias.reshape(1, n))
    if res is not None:
        in_specs.append(pl.BlockSpec((tm, tn), lambda j, i: (i, j)))
        args.append(res)
    out_specs = [pl.BlockSpec((tm, tn), lambda j, i: (i, j))]
    out_shape = [jax.ShapeDtypeStruct((m, n), out_dtype)]
    if side is not None:
        rows, width = side.shape
        side_rows = SIDE_ROWS
        while rows // side_rows > (n // tn) * steps_i:
            side_rows *= 2
        n_blocks = rows // side_rows
        assert rows % side_rows == 0
        spec = pl.BlockSpec((side_rows, width),
                            lambda j, i: (jnp.minimum(j * steps_i + i, n_blocks - 1), 0))
        in_specs.append(spec)
        args.append(side)
        out_specs.append(spec)
        out_shape.append(jax.ShapeDtypeStruct(side.shape, BF16))
    body = functools.partial(_mm_body, act=act, alpha=alpha, has_bias=bias is not None,
                             has_res=res is not None, has_side=side is not None)
    outs = pl.pallas_call(
        body,
        grid=(n // tn, steps_i),
        in_specs=in_specs,
        out_specs=out_specs,
        out_shape=out_shape,
        scratch_shapes=[pltpu.VMEM((k, tn), BF16)],
        compiler_params=_params("arbitrary", "arbitrary"),
        name=name,
    )(*args)
    return outs if side is not None else outs[0]


def _ln_rows(z, g, b):
    mu = jnp.mean(z, axis=-1, keepdims=True)
    zc = z - mu
    var = jnp.mean(zc * zc, axis=-1, keepdims=True)
    return zc * lax.rsqrt(var + LN_EPS) * g + b


def _ln_body(*refs, alpha, has_add, want_t):
    k = 0
    z = refs[k][...]
    k += 1
    if has_add:
        z = alpha * z + refs[k][...]
        k += 1
    g_ref, b_ref = refs[k], refs[k + 1]
    k += 2
    h = _ln_rows(z, g_ref[...], b_ref[...])
    refs[k][...] = h
    if want_t:
        refs[k + 1][...] = h.astype(BF16)
        refs[k + 2][...] = h.T.astype(BF16)


def _layer_norm(z, g, b, *, add=None, alpha=1.0, want_t=False, tm, name):
    t, d = z.shape
    tm = min(tm, t)
    assert t % tm == 0
    row = pl.BlockSpec((tm, d), lambda i: (i, 0))
    vec = pl.BlockSpec((1, d), lambda i: (0, 0))
    in_specs, args = [row], [z]
    if add is not None:
        in_specs.append(row)
        args.append(add)
    in_specs += [vec, vec]
    args += [g.reshape(1, d), b.reshape(1, d)]
    out_specs, out_shape = [row], [jax.ShapeDtypeStruct((t, d), F32)]
    if want_t:
        out_specs += [row, pl.BlockSpec((d, tm), lambda i: (0, i))]
        out_shape += [jax.ShapeDtypeStruct((t, d), BF16), jax.ShapeDtypeStruct((d, t), BF16)]
    return pl.pallas_call(
        functools.partial(_ln_body, alpha=alpha, has_add=add is not None, want_t=want_t),
        grid=(t // tm,),
        in_specs=in_specs,
        out_specs=out_specs,
        out_shape=out_shape,
        compiler_params=_params("arbitrary"),
        name=name,
    )(*args)


def _gmlp_body(gu_ref, gv_ref, g_ref, b_ref, ws_ref, bs_ref, o_ref):
    tm = gu_ref.shape[0]
    u = jax.nn.gelu(gu_ref[...].astype(F32))
    v = _ln_rows(jax.nn.gelu(gv_ref[...].astype(F32)), g_ref[...], b_ref[...]).astype(BF16)
    causal = (lax.broadcasted_iota(jnp.int32, (BLOCK, BLOCK), 0)
              >= lax.broadcasted_iota(jnp.int32, (BLOCK, BLOCK), 1))
    for g in range(GM_GROUPS):
        cols = slice(g * 128, (g + 1) * 128)
        w = jnp.where(causal, ws_ref[g], 0.0).astype(BF16)
        for c in range(tm // BLOCK):
            rows = slice(c * BLOCK, (c + 1) * BLOCK)
            mixed = jnp.dot(w, v[rows, cols], preferred_element_type=F32) + bs_ref[g]
            o_ref[rows, cols] = (u[rows, cols] * mixed).astype(o_ref.dtype)


def _gmlp(proj, ln_g, ln_b, w_s, b_s, *, tm):
    t = proj.shape[0]
    assert t % tm == 0 and tm % BLOCK == 0
    bs_full = jnp.broadcast_to(b_s[:, :, None], (GM_GROUPS, BLOCK, 128))
    return pl.pallas_call(
        _gmlp_body,
        grid=(t // tm,),
        in_specs=[pl.BlockSpec((tm, GM_WIDTH), lambda i: (i, OFF_GU // GM_WIDTH)),
                  pl.BlockSpec((tm, GM_WIDTH), lambda i: (i, OFF_GV // GM_WIDTH)),
                  pl.BlockSpec((1, GM_WIDTH), lambda i: (0, 0)),
                  pl.BlockSpec((1, GM_WIDTH), lambda i: (0, 0)),
                  pl.BlockSpec((GM_GROUPS, BLOCK, BLOCK), lambda i: (0, 0, 0)),
                  pl.BlockSpec((GM_GROUPS, BLOCK, 128), lambda i: (0, 0, 0))],
        out_specs=pl.BlockSpec((tm, GM_WIDTH), lambda i: (i, 0)),
        out_shape=jax.ShapeDtypeStruct((t, GM_WIDTH), BF16),
        compiler_params=_params("arbitrary"),
        name="gmlp",
    )(proj, proj, ln_g.reshape(1, GM_WIDTH), ln_b.reshape(1, GM_WIDTH), w_s, bs_full)


PERM_TILE = BLOCK * max(DILATIONS)


def _rope_table_body(pos_ref, inv_ref, cos_ref, sin_ref):
    half = HEAD_DIM // 2
    ang = pos_ref[...] * inv_ref[...]
    lane = lax.broadcasted_iota(jnp.int32, ang.shape, 1)
    sin = jnp.sin(ang)
    cos_ref[...] = jnp.cos(ang)
    sin_ref[...] = jnp.where(lane < half, -sin, sin)


def _rope_table(positions, *, tm):
    t = positions.size
    assert t % tm == 0
    half = HEAD_DIM // 2
    inv_freq = ROPE_THETA ** (-jnp.arange(half, dtype=F32) / half)
    inv = jnp.concatenate([inv_freq, inv_freq]).reshape(1, HEAD_DIM)
    pos = positions.astype(F32).reshape(t, 1)
    out = jax.ShapeDtypeStruct((t, HEAD_DIM), F32)
    return pl.pallas_call(
        _rope_table_body,
        grid=(t // tm,),
        in_specs=[pl.BlockSpec((tm, 1), lambda i: (i, 0)),
                  pl.BlockSpec((1, HEAD_DIM), lambda i: (0, 0))],
        out_specs=[pl.BlockSpec((tm, HEAD_DIM), lambda i: (i, 0))] * 2,
        out_shape=[out, out],
        compiler_params=_params("arbitrary"),
        name="rope_table",
    )(pos, inv)


def _rope_perm_body(q_ref, k_ref, v_ref, cos_ref, sin_ref, qo_ref, ko_ref, vo_ref, buf_ref, *, d):
    tm = q_ref.shape[0]
    half = HEAD_DIM // 2
    span = BLOCK * d
    for h in range(DIL_HEADS):
        cols = slice(h * HEAD_DIM, (h + 1) * HEAD_DIM)
        for src, dst, rotate in ((q_ref, qo_ref, True), (k_ref, ko_ref, True), (v_ref, vo_ref, False)):
            t = src[:, cols].astype(F32)
            if rotate:
                swapped = jnp.concatenate([t[:, half:], t[:, :half]], axis=1)
                t = t * cos_ref[...] + swapped * sin_ref[...]
            if d == 1:
                dst[:, cols] = t.astype(dst.dtype)
                continue
            buf_ref[...] = t
            for n in range(tm // span):
                for r in range(d):
                    rows = buf_ref[pl.ds(n * span + r, BLOCK, stride=d), :]
                    dst[n * span + r * BLOCK:n * span + (r + 1) * BLOCK, cols] = rows.astype(dst.dtype)


def _rope_perm(proj, cos, sin, group, d):
    t = proj.shape[0]
    tm, gw = PERM_TILE, DIL_GROUP_WIDTH
    assert t % tm == 0 and tm % (BLOCK * d) == 0
    col = lambda off: pl.BlockSpec((tm, gw), lambda i: (i, off // gw + group))
    tab = pl.BlockSpec((tm, HEAD_DIM), lambda i: (i, 0))
    out = jax.ShapeDtypeStruct((t, gw), BF16)
    return pl.pallas_call(
        functools.partial(_rope_perm_body, d=d),
        grid=(t // tm,),
        in_specs=[col(OFF_DQ), col(OFF_DK), col(OFF_DV), tab, tab],
        out_specs=[pl.BlockSpec((tm, gw), lambda i: (i, 0))] * 3,
        out_shape=[out, out, out],
        scratch_shapes=[pltpu.VMEM((tm, HEAD_DIM), F32)],
        compiler_params=_params("arbitrary"),
        name=f"rope_perm_d{d}",
    )(proj, proj, proj, cos, sin)


def _dil_body(q_ref, kc_ref, vc_ref, kp_ref, vp_ref, o_ref, lse_ref, *, d, nblk):
    step = pl.program_id(1)
    qi = lax.broadcasted_iota(jnp.int32, (BLOCK, BLOCK), 0)
    ki = lax.broadcasted_iota(jnp.int32, (BLOCK, BLOCK), 1)
    mask_cur = ki <= qi
    scale = HEAD_DIM ** -0.5
    for i in range(nblk):
        rows = slice(i * BLOCK, (i + 1) * BLOCK)
        if d >= nblk:
            kpr, vpr, prow, has_prev = kp_ref, vp_ref, rows, step >= d // nblk
        elif i == 0:
            kpr, vpr, prow, has_prev = kp_ref, vp_ref, slice(0, BLOCK), step > 0
        else:
            kpr, vpr, prow, has_prev = kc_ref, vc_ref, slice((i - 1) * BLOCK, i * BLOCK), None
        mask_prev = ki >= (qi if has_prev is None else qi + jnp.where(has_prev, 0, BLOCK))
        mask = jnp.concatenate([mask_prev, mask_cur], axis=1)
        for h in range(DIL_HEADS):
            cols = slice(h * HEAD_DIM, (h + 1) * HEAD_DIM)
            keys = jnp.concatenate([kpr[prow, cols], kc_ref[rows, cols]], axis=0)
            vals = jnp.concatenate([vpr[prow, cols], vc_ref[rows, cols]], axis=0)
            s = lax.dot_general(q_ref[rows, cols], keys, NT_DIMS, preferred_element_type=F32)
            s = jnp.where(mask, s * scale, NEG_INF)
            m = jnp.max(s, axis=-1, keepdims=True)
            p = jnp.exp(s - m).astype(BF16)
            v_aug = jnp.concatenate([vals, jnp.ones_like(vals)], axis=1)
            o_aug = jnp.dot(p, v_aug, preferred_element_type=F32)
            den = o_aug[:, HEAD_DIM:]
            o_ref[rows, cols] = o_aug[:, :HEAD_DIM] / den
            lse_ref[rows, cols] = m + jnp.log(den)


def _dilated_attention(q, k, v, d, batch, seq, *, nblk):
    gw = DIL_GROUP_WIDTH
    rows = nblk * BLOCK
    assert seq % rows == 0 and (d == 1 or d % nblk == 0)
    steps = seq // rows
    cur = pl.BlockSpec((rows, gw), lambda b, s: (b * steps + s, 0))
    if d >= nblk:
        back = d // nblk
        prev = pl.BlockSpec((rows, gw), lambda b, s: (b * steps + jnp.maximum(s - back, 0), 0))
    else:
        prev = pl.BlockSpec((BLOCK, gw),
                            lambda b, s: (b * steps * nblk + jnp.maximum(s * nblk - 1, 0), 0))
    out = jax.ShapeDtypeStruct((batch * seq, gw), F32)
    return pl.pallas_call(
        functools.partial(_dil_body, d=d, nblk=nblk),
        grid=(batch, steps),
        in_specs=[cur, cur, cur, prev, prev],
        out_specs=[cur, cur],
        out_shape=[out, out],
        compiler_params=_params("arbitrary", "arbitrary"),
        name=f"dil_attn_d{d}",
    )(q, k, v, k, v)


def _dil_merge_body(*refs):
    n = len(DILATIONS)
    o_refs, l_refs, out_ref, bufs = refs[:n], refs[n:2 * n], refs[2 * n], refs[2 * n + 1:]
    tm = out_ref.shape[0]
    outs, lses = [], []
    k = 0
    for d, o_ref, l_ref in zip(DILATIONS, o_refs, l_refs):
        if d == 1:
            outs.append(o_ref[...])
            lses.append(l_ref[...])
            continue
        span = BLOCK * d
        for src, buf in ((o_ref, bufs[k]), (l_ref, bufs[k + 1])):
            for s in range(tm // span):
                for r in range(d):
                    buf[pl.ds(s * span + r, BLOCK, stride=d), :] = (
                        src[s * span + r * BLOCK:s * span + (r + 1) * BLOCK, :])
        outs.append(bufs[k][...])
        lses.append(bufs[k + 1][...])
        k += 2
    m = functools.reduce(jnp.maximum, lses)
    es = [jnp.exp(l - m) for l in lses]
    num = sum(e * o for e, o in zip(es, outs))
    out_ref[...] = (num / sum(es)).astype(out_ref.dtype)


def _dil_merge(outs, lses):
    t, w = outs[0].shape
    tm = PERM_TILE
    assert t % tm == 0
    spec = pl.BlockSpec((tm, HEAD_DIM), lambda i, h: (i, h))
    n_buf = 2 * sum(1 for d in DILATIONS if d != 1)
    return pl.pallas_call(
        _dil_merge_body,
        grid=(t // tm, w // HEAD_DIM),
        in_specs=[spec] * (2 * len(DILATIONS)),
        out_specs=spec,
        out_shape=jax.ShapeDtypeStruct((t, w), BF16),
        scratch_shapes=[pltpu.VMEM((tm, HEAD_DIM), F32)] * n_buf,
        compiler_params=_params("arbitrary", "arbitrary"),
        name="dil_merge",
    )(*outs, *lses)


def _mem_body(q0, q1, q2, q3, kv_ref, o_ref):
    scale = MEM_HEAD_DIM ** -0.5
    for h, q_ref in enumerate((q0, q1, q2, q3)):
        cols = slice(h * MEM_HEAD_DIM, (h + 1) * MEM_HEAD_DIM)
        k = kv_ref[:, cols]
        v = kv_ref[:, MEM_WIDTH + h * MEM_HEAD_DIM:MEM_WIDTH + (h + 1) * MEM_HEAD_DIM]
        s = lax.dot_general(q_ref[...], k, NT_DIMS, preferred_element_type=F32) * scale
        m = jnp.max(s, axis=-1, keepdims=True)
        p = jnp.exp(s - m)
        den = jnp.sum(p, axis=-1, keepdims=True)
        o = jnp.dot(p.astype(BF16), v, preferred_element_type=F32) / den
        o_ref[:, cols] = o.astype(o_ref.dtype)


def _mem_attention(proj, mkv, batch, seq, *, tm):
    mem_len = mkv.shape[0] // batch
    assert seq % tm == 0
    steps = seq // tm
    q_specs = [pl.BlockSpec((tm, MEM_HEAD_DIM),
                            lambda b, i, h=h: (b * steps + i, OFF_MQ // MEM_HEAD_DIM + h))
               for h in range(MEM_HEADS)]
    return pl.pallas_call(
        _mem_body,
        grid=(batch, steps),
        in_specs=q_specs + [pl.BlockSpec((mem_len, 2 * MEM_WIDTH), lambda b, i: (b, 0))],
        out_specs=pl.BlockSpec((tm, MEM_WIDTH), lambda b, i: (b * steps + i, 0)),
        out_shape=jax.ShapeDtypeStruct((batch * seq, MEM_WIDTH), BF16),
        compiler_params=_params("arbitrary", "arbitrary"),
        name="mem_attn",
    )(proj, proj, proj, proj, mkv)


def _merge_body(a_ref, b_ref, m_ref, wa_ref, wb_ref, wm_ref, g0_ref, g1_ref, g2_ref, o_ref,
                wa_bf, wb_bf, wm_bf):
    @pl.when(pl.program_id(1) == 0)
    def _():
        wa_bf[...] = wa_ref[...].astype(BF16)
        wb_bf[...] = wb_ref[...].astype(BF16)
        wm_bf[...] = wm_ref[...].astype(BF16)

    ya = jnp.dot(a_ref[...], wa_bf[...], preferred_element_type=F32)
    yb = jnp.dot(b_ref[...], wb_bf[...], preferred_element_type=F32)
    ym = jnp.dot(m_ref[...], wm_bf[...], preferred_element_type=F32)
    o = (g0_ref[...].astype(F32) * ya + g1_ref[...].astype(F32) * yb
         + g2_ref[...].astype(F32) * ym)
    o_ref[...] = o.astype(o_ref.dtype)


def _merge(a_out, b_out, m_out, w_a, w_b, w_m, gates, *, tm, tn):
    t = a_out.shape[0]
    d = w_a.shape[1]
    assert t % tm == 0 and d % tn == 0
    nj = d // tn
    rows = lambda w: pl.BlockSpec((tm, w), lambda j, i: (i, 0))
    wcol = lambda k: pl.BlockSpec((k, tn), lambda j, i: (0, j))
    gate = lambda br: pl.BlockSpec((tm, tn), lambda j, i, br=br: (i, br * nj + j))
    return pl.pallas_call(
        _merge_body,
        grid=(nj, t // tm),
        in_specs=[rows(GM_WIDTH), rows(DIL_GROUP_WIDTH), rows(MEM_WIDTH),
                  wcol(GM_WIDTH), wcol(DIL_GROUP_WIDTH), wcol(MEM_WIDTH),
                  gate(0), gate(1), gate(2)],
        out_specs=pl.BlockSpec((tm, tn), lambda j, i: (i, j)),
        out_shape=jax.ShapeDtypeStruct((t, d), BF16),
        scratch_shapes=[pltpu.VMEM((GM_WIDTH, tn), BF16), pltpu.VMEM((DIL_GROUP_WIDTH, tn), BF16),
                        pltpu.VMEM((MEM_WIDTH, tn), BF16)],
        compiler_params=_params("arbitrary", "arbitrary"),
        name="branch_merge",
    )(a_out, b_out, m_out, w_a, w_b, w_m, gates, gates, gates)


def _sorting_network(n):
    pairs = []
    p = 1
    while p < n:
        k = p
        while k >= 1:
            for j in range(k % p, n - k, 2 * k):
                for i in range(min(k, n - j - k)):
                    if (i + j) // (2 * p) == (i + j + k) // (2 * p):
                        pairs.append((i + j, i + j + k))
            k //= 2
        p *= 2
    return pairs


SUBLANES = 8


def _top_values(x, k):
    n_groups = x.shape[0] // SUBLANES
    assert n_groups >= k
    g = [x[SUBLANES * j:SUBLANES * (j + 1), :] for j in range(n_groups)]
    for i, j in _sorting_network(n_groups):
        g[i], g[j] = jnp.maximum(g[i], g[j]), jnp.minimum(g[i], g[j])
    sub = lax.broadcasted_iota(jnp.int32, g[0].shape, 0)
    vals = []
    for r in range(k):
        m = jnp.max(g[0], axis=0, keepdims=True)
        vals.append(m)
        first = jnp.min(jnp.where(g[0] == m, sub, SUBLANES), axis=0, keepdims=True)
        hit = sub == first
        for j in range(k - 1 - r):
            g[j] = jnp.where(hit, g[j + 1], g[j])
    return jnp.concatenate(vals, axis=0)


def _staircase(a1, a2, combine, fill):
    r8 = lax.broadcasted_iota(jnp.int32, (8, a1.shape[1]), 0)
    r16 = lax.broadcasted_iota(jnp.int32, a1.shape, 0)
    lo8 = lambda x: x[0:8]
    return jnp.concatenate([
        combine(a1[0:1], a2),
        combine(a1[1:2], lo8(a2)),
        jnp.where(r8 < 5, combine(a1[2:3], lo8(a2)), fill),
        jnp.where(r8 < 4, combine(a1[3:4], lo8(a2)), fill),
        jnp.where(r16 >= 4, combine(a1, a2[0:1]), fill),
        jnp.where(r8 >= 4, combine(lo8(a1), a2[1:2]), fill),
        jnp.where(r8 == 4, combine(lo8(a1), a2[2:3]), fill),
    ], axis=0)


def _peer_route_body(q_ref, keys_ref, g_ref, cnt_ref, e1_ref, rank_ref, e2_ref):
    tm = q_ref.shape[0]
    k = PEER_TOPK
    for h in range(PEER_HEADS):
        sts, tops = [], []
        for p in range(2):
            hp = 2 * h + p
            qs = q_ref[:, hp * PEER_HALF:(hp + 1) * PEER_HALF]
            st = lax.dot_general(keys_ref[hp], qs, NT_DIMS, preferred_element_type=F32,
                                 precision=lax.Precision.HIGHEST)
            sts.append(st)
            tops.append(_top_values(st, k))
        a1, a2 = tops
        m1, m2 = a1[0:1], a2[0:1]
        cand = _staircase(a1, a2, jnp.add, -jnp.inf)
        prob = _staircase(jnp.exp(a1 - m1), jnp.exp(a2 - m2), jnp.multiply, 0.0)
        x = cand
        left = jnp.full((1, tm), float(k), F32)
        tau = jnp.full((1, tm), jnp.inf, F32)
        for _ in range(k):
            m = jnp.max(x, axis=0, keepdims=True)
            eq = x == m
            tau = jnp.where(left > 0, m, tau)
            left = left - jnp.sum(jnp.where(eq, 1.0, 0.0), axis=0, keepdims=True)
            x = jnp.where(eq, -jnp.inf, x)
        z = jnp.sum(jnp.where(cand >= tau, prob, 0.0), axis=0, keepdims=True)
        cnt = jnp.zeros((PEER_KEYS, tm), F32)
        rank = jnp.zeros((PEER_KEYS, tm), F32)
        for b in range(k):
            cnt = cnt + jnp.where(sts[0] + a2[b:b + 1] >= tau, 1.0, 0.0)
            rank = rank + jnp.where(a2[b:b + 1] > sts[1], 1.0, 0.0)
        cnt_ref[h] = cnt
        e1_ref[h] = jnp.exp(sts[0] - m1) / z
        rank_ref[h] = rank.astype(BF16)
        e2_ref[h] = jnp.exp(sts[1] - m2).astype(BF16)

    def tile16(row_f32):
        packed = jnp.broadcast_to(row_f32, (16, tm)).astype(BF16)
        return jnp.concatenate([packed] * (PEER_KEYS // 16), axis=0)

    def build(i1, carry):
        gate = None
        for h in range(PEER_HEADS):
            cnt = tile16(cnt_ref[h, pl.ds(i1, 1), :])
            e1 = tile16(e1_ref[h, pl.ds(i1, 1), :])
            sel = jnp.where(rank_ref[h] < cnt, e1 * e2_ref[h], jnp.zeros((), BF16))
            gate = sel if gate is None else gate + sel
        g_ref[pl.ds(pl.multiple_of(i1 * PEER_KEYS, PEER_KEYS), PEER_KEYS), :] = gate
        return carry

    lax.fori_loop(0, PEER_KEYS, build, 0)


def _peer_route(q, sub_keys, *, tm):
    t = q.shape[0]
    assert t % tm == 0
    hp = 2 * PEER_HEADS
    keys = sub_keys.reshape(hp, PEER_KEYS, PEER_HALF)
    n_exp = PEER_KEYS * PEER_KEYS
    per_head = lambda dt: pltpu.VMEM((PEER_HEADS, PEER_KEYS, tm), dt)
    return pl.pallas_call(
        _peer_route_body,
        grid=(t // tm,),
        in_specs=[pl.BlockSpec((tm, hp * PEER_HALF), lambda i: (i, 0)),
                  pl.BlockSpec((hp, PEER_KEYS, PEER_HALF), lambda i: (0, 0, 0))],
        out_specs=pl.BlockSpec((n_exp, tm), lambda i: (0, i)),
        out_shape=jax.ShapeDtypeStruct((n_exp, t), BF16),
        scratch_shapes=[per_head(F32), per_head(F32), per_head(BF16), per_head(BF16)],
        compiler_params=_params("arbitrary"),
        name="peer_route",
    )(q, keys)


def _peer_dense_body(xt_ref, u_ref, v_ref, g_ref, o_ref):
    @pl.when(pl.program_id(1) == 0)
    def _():
        o_ref[...] = jnp.zeros_like(o_ref)

    act_t = jnp.dot(u_ref[...], xt_ref[...], preferred_element_type=F32)
    w_t = jax.nn.gelu(act_t.astype(BF16)) * g_ref[...]
    o_ref[...] += jnp.dot(w_t.T, v_ref[...], preferred_element_type=F32)


def _peer_dense(h_t, u, v, gate_t, *, tm, te):
    d, t = h_t.shape
    n_exp = u.shape[0]
    assert t % tm == 0 and n_exp % te == 0
    once = dict(pipeline_mode=pl.Buffered(1))
    return pl.pallas_call(
        _peer_dense_body,
        grid=(t // tm, n_exp // te),
        in_specs=[pl.BlockSpec((d, tm), lambda i, e: (0, i), **once),
                  pl.BlockSpec((te, d), lambda i, e: (e, 0)),
                  pl.BlockSpec((te, d), lambda i, e: (e, 0)),
                  pl.BlockSpec((te, tm), lambda i, e: (e, i))],
        out_specs=pl.BlockSpec((tm, d), lambda i, e: (i, 0), **once),
        out_shape=jax.ShapeDtypeStruct((t, d), F32),
        compiler_params=_params("arbitrary", "arbitrary"),
        name="peer_dense",
    )(h_t, u, v, gate_t)


MM_TM, MM_TN = 1024, 512
ROW_TM = 256
DIL_BLOCKS_PER_STEP = 4
PEER_TM, PEER_TE = 1024, 512
ROUTE_TM = 256


def kernel(x, mem, positions, w_in, gm_ln_g, gm_ln_b, gm_w_s, gm_b_s, w_mem_kv, w_gate, b_gate,
           w_br_gmlp, w_br_dil, w_br_mem, w_out, ln_mix_g, ln_mix_b, peer_w_query, peer_sub_keys,
           peer_u, peer_v, ln_ffn_g, ln_ffn_b):
    batch, seq, d = x.shape
    t = batch * seq
    assert w_in.shape[0] == DEPTH and w_in.shape[2] == IN_WIDTH
    x2 = x.reshape(t, d)
    x_bf = x2.astype(BF16)
    mem_bf = mem.reshape(-1, d).astype(BF16)

    n_exp = PEER_KEYS * PEER_KEYS
    proj, v_bf = _matmul(x_bf, w_in.reshape(d, IN_WIDTH), out_dtype=BF16, tm=MM_TM, tn=MM_TN,
                         side=peer_v.reshape(n_exp, d), name="in_proj")
    gates, u_bf = _matmul(x_bf, w_gate.reshape(d, 3 * d), bias=b_gate.reshape(3 * d),
                          act="sigmoid", out_dtype=BF16, tm=MM_TM, tn=MM_TN,
                          side=peer_u.reshape(n_exp, d), name="gate_proj")
    mkv = _matmul(mem_bf, w_mem_kv.reshape(d, 2 * MEM_WIDTH), out_dtype=BF16, tm=MM_TM, tn=MM_TN,
                  name="mem_kv_proj")

    a_out = _gmlp(proj, gm_ln_g.reshape(-1), gm_ln_b.reshape(-1), gm_w_s.reshape(GM_GROUPS, BLOCK, BLOCK),
                  gm_b_s.reshape(GM_GROUPS, BLOCK), tm=512)

    cos, sin = _rope_table(positions, tm=1024)
    outs, lses = [], []
    for g, dil in enumerate(DILATIONS):
        qg, kg, vg = _rope_perm(proj, cos, sin, g, dil)
        o, lse = _dilated_attention(qg, kg, vg, dil, batch, seq, nblk=DIL_BLOCKS_PER_STEP)
        outs.append(o)
        lses.append(lse)
    b_out = _dil_merge(outs, lses)

    m_out = _mem_attention(proj, mkv, batch, seq, tm=512)

    merged = _merge(a_out, b_out, m_out, w_br_gmlp.reshape(GM_WIDTH, d),
                    w_br_dil.reshape(DIL_GROUP_WIDTH, d), w_br_mem.reshape(MEM_WIDTH, d), gates,
                    tm=MM_TM, tn=MM_TN)
    z1 = _matmul(merged, w_out.reshape(d, d), res=x2, alpha=ALPHA, out_dtype=F32, tm=MM_TM,
                 tn=MM_TN, name="out_proj")
    h1, h1_bf, h1_t = _layer_norm(z1, ln_mix_g.reshape(-1), ln_mix_b.reshape(-1), want_t=True,
                                  tm=ROW_TM, name="ln_mix")

    hq = PEER_HEADS * 2 * PEER_HALF
    q = _matmul(h1_bf, peer_w_query.reshape(d, hq), out_dtype=F32, tm=MM_TM, tn=MM_TN,
                name="peer_query")
    gate_t = _peer_route(q, peer_sub_keys, tm=ROUTE_TM)
    y = _peer_dense(h1_t, u_bf, v_bf, gate_t, tm=PEER_TM, te=PEER_TE)
    (out,) = _layer_norm(h1, ln_ffn_g.reshape(-1), ln_ffn_b.reshape(-1), add=y, alpha=ALPHA,
                         tm=ROW_TM, name="ln_ffn")
    return out.reshape(batch, seq, d)
```

```python
import functools

import numpy as np
import jax
import jax.numpy as jnp
from jax import lax
from jax.experimental import pallas as pl
from jax.experimental.pallas import tpu as pltpu

F32 = jnp.float32
BF16 = jnp.bfloat16

HEAD_DIM = 128
BLOCK = 128
GM_GROUPS = 12
GM_WIDTH = GM_GROUPS * 128
DILATIONS = (1, 4, 16)
DIL_HEADS = 4
DIL_GROUP_WIDTH = DIL_HEADS * HEAD_DIM
DIL_WIDTH = len(DILATIONS) * DIL_GROUP_WIDTH
MEM_HEADS = 4
MEM_HEAD_DIM = 256
MEM_WIDTH = MEM_HEADS * MEM_HEAD_DIM
PEER_HEADS = 8
PEER_KEYS = 128
PEER_TOPK = 16
PEER_HALF = 128
ROPE_THETA = 10000.0
LN_EPS = 1e-5
DEPTH = 1
ALPHA = (2 * DEPTH) ** 0.25
NEG_INF = -1e30

OFF_GU, OFF_GV = 0, GM_WIDTH
OFF_DQ = 2 * GM_WIDTH
OFF_DK = OFF_DQ + DIL_WIDTH
OFF_DV = OFF_DK + DIL_WIDTH
OFF_MQ = OFF_DV + DIL_WIDTH
IN_WIDTH = OFF_MQ + MEM_WIDTH

V7X_VMEM_BYTES = 64 * 1024 * 1024
VMEM_LIMIT = V7X_VMEM_BYTES - 8 * 1024 * 1024

NT_DIMS = (((1,), (1,)), ((), ()))
SIDE_ROWS = 128
GELU_C = float(np.float32(np.sqrt(2.0 / np.pi)))


def _params(*sem):
    return pltpu.CompilerParams(dimension_semantics=sem, vmem_limit_bytes=VMEM_LIMIT)


def _mm_body(*refs, act, alpha, has_bias, has_res, has_side):
    x_ref, w_ref = refs[0], refs[1]
    k = 2
    b_ref = r_ref = s_ref = None
    if has_bias:
        b_ref = refs[k]
        k += 1
    if has_res:
        r_ref = refs[k]
        k += 1
    if has_side:
        s_ref = refs[k]
        k += 1
    o_ref = refs[k]
    k += 1
    if has_side:
        refs[k][...] = s_ref[...].astype(BF16)
        k += 1
    wb_ref = refs[k]

    @pl.when(pl.program_id(1) == 0)
    def _():
        wb_ref[...] = w_ref[...].astype(BF16)

    acc = jnp.dot(x_ref[...], wb_ref[...], preferred_element_type=F32)
    if has_bias:
        acc = acc + b_ref[...]
    if act == "sigmoid":
        acc = 0.5 * jnp.tanh(0.5 * acc) + 0.5
    if has_res:
        acc = alpha * r_ref[...] + acc
    o_ref[...] = acc.astype(o_ref.dtype)


def _matmul(x, w, *, out_dtype, tm, tn, bias=None, res=None, alpha=1.0, act=None, side=None, name):
    m, k = x.shape
    n = w.shape[1]
    tm, tn = min(tm, m), min(tn, n)
    assert m % tm == 0 and n % tn == 0
    steps_i = m // tm
    in_specs = [pl.BlockSpec((tm, k), lambda j, i: (i, 0)),
                pl.BlockSpec((k, tn), lambda j, i: (0, j))]
    args = [x, w]
    if bias is not None:
        in_specs.append(pl.BlockSpec((1, tn), lambda j, i: (0, j)))
        args.append(bias.reshape(1, n))
    if res is not None:
        in_specs.append(pl.BlockSpec((tm, tn), lambda j, i: (i, j)))
        args.append(res)
    out_specs = [pl.BlockSpec((tm, tn), lambda j, i: (i, j))]
    out_shape = [jax.ShapeDtypeStruct((m, n), out_dtype)]
    if side is not None:
        rows, width = side.shape
        side_rows = SIDE_ROWS
        while rows // side_rows > (n // tn) * steps_i:
            side_rows *= 2
        n_blocks = rows // side_rows
        assert rows % side_rows == 0
        spec = pl.BlockSpec((side_rows, width),
                            lambda j, i: (jnp.minimum(j * steps_i + i, n_blocks - 1), 0))
        in_specs.append(spec)
        args.append(side)
        out_specs.append(spec)
        out_shape.append(jax.ShapeDtypeStruct(side.shape, BF16))
    body = functools.partial(_mm_body, act=act, alpha=alpha, has_bias=bias is not None,
                             has_res=res is not None, has_side=side is not None)
    outs = pl.pallas_call(
        body,
        grid=(n // tn, steps_i),
        in_specs=in_specs,
        out_specs=out_specs,
        out_shape=out_shape,
        scratch_shapes=[pltpu.VMEM((k, tn), BF16)],
        compiler_params=_params("arbitrary", "arbitrary"),
        name=name,
    )(*args)
    return outs if side is not None else outs[0]


def _ln_rows(z, g, b):
    mu = jnp.mean(z, axis=-1, keepdims=True)
    zc = z - mu
    var = jnp.mean(zc * zc, axis=-1, keepdims=True)
    return zc * lax.rsqrt(var + LN_EPS) * g + b


def _ln_body(*refs, alpha, has_add, want_t):
    k = 0
    z = refs[k][...]
    k += 1
    if has_add:
        z = alpha * z + refs[k][...]
        k += 1
    g_ref, b_ref = refs[k], refs[k + 1]
    k += 2
    h = _ln_rows(z, g_ref[...], b_ref[...])
    refs[k][...] = h
    if want_t:
        refs[k + 1][...] = h.astype(BF16)
        refs[k + 2][...] = h.T.astype(BF16)


def _layer_norm(z, g, b, *, add=None, alpha=1.0, want_t=False, tm, name):
    t, d = z.shape
    tm = min(tm, t)
    assert t % tm == 0
    row = pl.BlockSpec((tm, d), lambda i: (i, 0))
    vec = pl.BlockSpec((1, d), lambda i: (0, 0))
    in_specs, args = [row], [z]
    if add is not None:
        in_specs.append(row)
        args.append(add)
    in_specs += [vec, vec]
    args += [g.reshape(1, d), b.reshape(1, d)]
    out_specs, out_shape = [row], [jax.ShapeDtypeStruct((t, d), F32)]
    if want_t:
        out_specs += [row, pl.BlockSpec((d, tm), lambda i: (0, i))]
        out_shape += [jax.ShapeDtypeStruct((t, d), BF16), jax.ShapeDtypeStruct((d, t), BF16)]
    return pl.pallas_call(
        functools.partial(_ln_body, alpha=alpha, has_add=add is not None, want_t=want_t),
        grid=(t // tm,),
        in_specs=in_specs,
        out_specs=out_specs,
        out_shape=out_shape,
        compiler_params=_params("arbitrary"),
        name=name,
    )(*args)


def _gmlp_body(gu_ref, gv_ref, g_ref, b_ref, ws_ref, bs_ref, o_ref):
    tm = gu_ref.shape[0]
    u = jax.nn.gelu(gu_ref[...].astype(F32))
    v = _ln_rows(jax.nn.gelu(gv_ref[...].astype(F32)), g_ref[...], b_ref[...]).astype(BF16)
    causal = (lax.broadcasted_iota(jnp.int32, (BLOCK, BLOCK), 0)
              >= lax.broadcasted_iota(jnp.int32, (BLOCK, BLOCK), 1))
    for g in range(GM_GROUPS):
        cols = slice(g * 128, (g + 1) * 128)
        w = jnp.where(causal, ws_ref[g], 0.0).astype(BF16)
        for c in range(tm // BLOCK):
            rows = slice(c * BLOCK, (c + 1) * BLOCK)
            mixed = jnp.dot(w, v[rows, cols], preferred_element_type=F32) + bs_ref[g]
            o_ref[rows, cols] = (u[rows, cols] * mixed).astype(o_ref.dtype)


def _gmlp(proj, ln_g, ln_b, w_s, b_s, *, tm):
    t = proj.shape[0]
    assert t % tm == 0 and tm % BLOCK == 0
    bs_full = jnp.broadcast_to(b_s[:, :, None], (GM_GROUPS, BLOCK, 128))
    return pl.pallas_call(
        _gmlp_body,
        grid=(t // tm,),
        in_specs=[pl.BlockSpec((tm, GM_WIDTH), lambda i: (i, OFF_GU // GM_WIDTH)),
                  pl.BlockSpec((tm, GM_WIDTH), lambda i: (i, OFF_GV // GM_WIDTH)),
                  pl.BlockSpec((1, GM_WIDTH), lambda i: (0, 0)),
                  pl.BlockSpec((1, GM_WIDTH), lambda i: (0, 0)),
                  pl.BlockSpec((GM_GROUPS, BLOCK, BLOCK), lambda i: (0, 0, 0)),
                  pl.BlockSpec((GM_GROUPS, BLOCK, 128), lambda i: (0, 0, 0))],
        out_specs=pl.BlockSpec((tm, GM_WIDTH), lambda i: (i, 0)),
        out_shape=jax.ShapeDtypeStruct((t, GM_WIDTH), BF16),
        compiler_params=_params("arbitrary"),
        name="gmlp",
    )(proj, proj, ln_g.reshape(1, GM_WIDTH), ln_b.reshape(1, GM_WIDTH), w_s, bs_full)


PERM_TILE = BLOCK * max(DILATIONS)


def _rope_table_body(pos_ref, inv_ref, cos_ref, sin_ref):
    half = HEAD_DIM // 2
    ang = pos_ref[...] * inv_ref[...]
    lane = lax.broadcasted_iota(jnp.int32, ang.shape, 1)
    sin = jnp.sin(ang)
    cos_ref[...] = jnp.cos(ang)
    sin_ref[...] = jnp.where(lane < half, -sin, sin)


def _rope_table(positions, *, tm):
    t = positions.size
    assert t % tm == 0
    half = HEAD_DIM // 2
    inv_freq = ROPE_THETA ** (-jnp.arange(half, dtype=F32) / half)
    inv = jnp.concatenate([inv_freq, inv_freq]).reshape(1, HEAD_DIM)
    pos = positions.astype(F32).reshape(t, 1)
    out = jax.ShapeDtypeStruct((t, HEAD_DIM), F32)
    return pl.pallas_call(
        _rope_table_body,
        grid=(t // tm,),
        in_specs=[pl.BlockSpec((tm, 1), lambda i: (i, 0)),
                  pl.BlockSpec((1, HEAD_DIM), lambda i: (0, 0))],
        out_specs=[pl.BlockSpec((tm, HEAD_DIM), lambda i: (i, 0))] * 2,
        out_shape=[out, out],
        compiler_params=_params("arbitrary"),
        name="rope_table",
    )(pos, inv)


def _rope_perm_body(q_ref, k_ref, v_ref, cos_ref, sin_ref, qo_ref, ko_ref, vo_ref, buf_ref, *, d):
    tm = q_ref.shape[0]
    half = HEAD_DIM // 2
    span = BLOCK * d
    for h in range(DIL_HEADS):
        cols = slice(h * HEAD_DIM, (h + 1) * HEAD_DIM)
        for src, dst, rotate in ((q_ref, qo_ref, True), (k_ref, ko_ref, True), (v_ref, vo_ref, False)):
            t = src[:, cols].astype(F32)
            if rotate:
                swapped = jnp.concatenate([t[:, half:], t[:, :half]], axis=1)
                t = t * cos_ref[...] + swapped * sin_ref[...]
            if d == 1:
                dst[:, cols] = t.astype(dst.dtype)
                continue
            buf_ref[...] = t
            for n in range(tm // span):
                for r in range(d):
                    rows = buf_ref[pl.ds(n * span + r, BLOCK, stride=d), :]
                    dst[n * span + r * BLOCK:n * span + (r + 1) * BLOCK, cols] = rows.astype(dst.dtype)


def _rope_perm(proj, cos, sin, group, d):
    t = proj.shape[0]
    tm, gw = PERM_TILE, DIL_GROUP_WIDTH
    assert t % tm == 0 and tm % (BLOCK * d) == 0
    col = lambda off: pl.BlockSpec((tm, gw), lambda i: (i, off // gw + group))
    tab = pl.BlockSpec((tm, HEAD_DIM), lambda i: (i, 0))
    out = jax.ShapeDtypeStruct((t, gw), BF16)
    return pl.pallas_call(
        functools.partial(_rope_perm_body, d=d),
        grid=(t // tm,),
        in_specs=[col(OFF_DQ), col(OFF_DK), col(OFF_DV), tab, tab],
        out_specs=[pl.BlockSpec((tm, gw), lambda i: (i, 0))] * 3,
        out_shape=[out, out, out],
        scratch_shapes=[pltpu.VMEM((tm, HEAD_DIM), F32)],
        compiler_params=_params("arbitrary"),
        name=f"rope_perm_d{d}",
    )(proj, proj, proj, cos, sin)


def _dil_body(q_ref, kc_ref, vc_ref, kp_ref, vp_ref, o_ref, lse_ref, *, d, nblk):
    step = pl.program_id(1)
    qi = lax.broadcasted_iota(jnp.int32, (BLOCK, BLOCK), 0)
    ki = lax.broadcasted_iota(jnp.int32, (BLOCK, BLOCK), 1)
    mask_cur = ki <= qi
    scale = HEAD_DIM ** -0.5
    for i in range(nblk):
        rows = slice(i * BLOCK, (i + 1) * BLOCK)
        if d >= nblk:
            kpr, vpr, prow, has_prev = kp_ref, vp_ref, rows, step >= d // nblk
        elif i == 0:
            kpr, vpr, prow, has_prev = kp_ref, vp_ref, slice(0, BLOCK), step > 0
        else:
            kpr, vpr, prow, has_prev = kc_ref, vc_ref, slice((i - 1) * BLOCK, i * BLOCK), None
        mask_prev = ki >= (qi if has_prev is None else qi + jnp.where(has_prev, 0, BLOCK))
        mask = jnp.concatenate([mask_prev, mask_cur], axis=1)
        for h in range(DIL_HEADS):
            cols = slice(h * HEAD_DIM, (h + 1) * HEAD_DIM)
            keys = jnp.concatenate([kpr[prow, cols], kc_ref[rows, cols]], axis=0)
            vals = jnp.concatenate([vpr[prow, cols], vc_ref[rows, cols]], axis=0)
            s = lax.dot_general(q_ref[rows, cols], keys, NT_DIMS, preferred_element_type=F32)
            s = jnp.where(mask, s * scale, NEG_INF)
            m = jnp.max(s, axis=-1, keepdims=True)
            p = jnp.exp(s - m).astype(BF16)
            v_aug = jnp.concatenate([vals, jnp.ones_like(vals)], axis=1)
            o_aug = jnp.dot(p, v_aug, preferred_element_type=F32)
            den = o_aug[:, HEAD_DIM:]
            o_ref[rows, cols] = o_aug[:, :HEAD_DIM] / den
            lse_ref[rows, cols] = m + jnp.log(den)


def _dilated_attention(q, k, v, d, batch, seq, *, nblk):
    gw = DIL_GROUP_WIDTH
    rows = nblk * BLOCK
    assert seq % rows == 0 and (d == 1 or d % nblk == 0)
    steps = seq // rows
    cur = pl.BlockSpec((rows, gw), lambda b, s: (b * steps + s, 0))
    if d >= nblk:
        back = d // nblk
        prev = pl.BlockSpec((rows, gw), lambda b, s: (b * steps + jnp.maximum(s - back, 0), 0))
    else:
        prev = pl.BlockSpec((BLOCK, gw),
                            lambda b, s: (b * steps * nblk + jnp.maximum(s * nblk - 1, 0), 0))
    out = jax.ShapeDtypeStruct((batch * seq, gw), F32)
    return pl.pallas_call(
        functools.partial(_dil_body, d=d, nblk=nblk),
        grid=(batch, steps),
        in_specs=[cur, cur, cur, prev, prev],
        out_specs=[cur, cur],
        out_shape=[out, out],
        compiler_params=_params("arbitrary", "arbitrary"),
        name=f"dil_attn_d{d}",
    )(q, k, v, k, v)


def _dil_merge_body(*refs):
    n = len(DILATIONS)
    o_refs, l_refs, out_ref, bufs = refs[:n], refs[n:2 * n], refs[2 * n], refs[2 * n + 1:]
    tm = out_ref.shape[0]
    outs, lses = [], []
    k = 0
    for d, o_ref, l_ref in zip(DILATIONS, o_refs, l_refs):
        if d == 1:
            outs.append(o_ref[...])
            lses.append(l_ref[...])
            continue
        span = BLOCK * d
        for src, buf in ((o_ref, bufs[k]), (l_ref, bufs[k + 1])):
            for s in range(tm // span):
                for r in range(d):
                    buf[pl.ds(s * span + r, BLOCK, stride=d), :] = (
                        src[s * span + r * BLOCK:s * span + (r + 1) * BLOCK, :])
        outs.append(bufs[k][...])
        lses.append(bufs[k + 1][...])
        k += 2
    m = functools.reduce(jnp.maximum, lses)
    es = [jnp.exp(l - m) for l in lses]
    num = sum(e * o for e, o in zip(es, outs))
    out_ref[...] = (num / sum(es)).astype(out_ref.dtype)


def _dil_merge(outs, lses):
    t, w = outs[0].shape
    tm = PERM_TILE
    assert t % tm == 0
    spec = pl.BlockSpec((tm, HEAD_DIM), lambda i, h: (i, h))
    n_buf = 2 * sum(1 for d in DILATIONS if d != 1)
    return pl.pallas_call(
        _dil_merge_body,
        grid=(t // tm, w // HEAD_DIM),
        in_specs=[spec] * (2 * len(DILATIONS)),
        out_specs=spec,
        out_shape=jax.ShapeDtypeStruct((t, w), BF16),
        scratch_shapes=[pltpu.VMEM((tm, HEAD_DIM), F32)] * n_buf,
        compiler_params=_params("arbitrary", "arbitrary"),
        name="dil_merge",
    )(*outs, *lses)


def _mem_body(q0, q1, q2, q3, kv_ref, o_ref):
    scale = MEM_HEAD_DIM ** -0.5
    for h, q_ref in enumerate((q0, q1, q2, q3)):
        cols = slice(h * MEM_HEAD_DIM, (h + 1) * MEM_HEAD_DIM)
        k = kv_ref[:, cols]
        v = kv_ref[:, MEM_WIDTH + h * MEM_HEAD_DIM:MEM_WIDTH + (h + 1) * MEM_HEAD_DIM]
        s = lax.dot_general(q_ref[...], k, NT_DIMS, preferred_element_type=F32) * scale
        m = jnp.max(s, axis=-1, keepdims=True)
        p = jnp.exp(s - m)
        den = jnp.sum(p, axis=-1, keepdims=True)
        o = jnp.dot(p.astype(BF16), v, preferred_element_type=F32) / den
        o_ref[:, cols] = o.astype(o_ref.dtype)


def _mem_attention(proj, mkv, batch, seq, *, tm):
    mem_len = mkv.shape[0] // batch
    assert seq % tm == 0
    steps = seq // tm
    q_specs = [pl.BlockSpec((tm, MEM_HEAD_DIM),
                            lambda b, i, h=h: (b * steps + i, OFF_MQ // MEM_HEAD_DIM + h))
               for h in range(MEM_HEADS)]
    return pl.pallas_call(
        _mem_body,
        grid=(batch, steps),
        in_specs=q_specs + [pl.BlockSpec((mem_len, 2 * MEM_WIDTH), lambda b, i: (b, 0))],
        out_specs=pl.BlockSpec((tm, MEM_WIDTH), lambda b, i: (b * steps + i, 0)),
        out_shape=jax.ShapeDtypeStruct((batch * seq, MEM_WIDTH), BF16),
        compiler_params=_params("arbitrary", "arbitrary"),
        name="mem_attn",
    )(proj, proj, proj, proj, mkv)


def _merge_body(a_ref, b_ref, m_ref, wa_ref, wb_ref, wm_ref, g0_ref, g1_ref, g2_ref, o_ref,
                wa_bf, wb_bf, wm_bf):
    @pl.when(pl.program_id(1) == 0)
    def _():
        wa_bf[...] = wa_ref[...].astype(BF16)
        wb_bf[...] = wb_ref[...].astype(BF16)
        wm_bf[...] = wm_ref[...].astype(BF16)

    ya = jnp.dot(a_ref[...], wa_bf[...], preferred_element_type=F32)
    yb = jnp.dot(b_ref[...], wb_bf[...], preferred_element_type=F32)
    ym = jnp.dot(m_ref[...], wm_bf[...], preferred_element_type=F32)
    o = (g0_ref[...].astype(F32) * ya + g1_ref[...].astype(F32) * yb
         + g2_ref[...].astype(F32) * ym)
    o_ref[...] = o.astype(o_ref.dtype)


def _merge(a_out, b_out, m_out, w_a, w_b, w_m, gates, *, tm, tn):
    t = a_out.shape[0]
    d = w_a.shape[1]
    assert t % tm == 0 and d % tn == 0
    nj = d // tn
    rows = lambda w: pl.BlockSpec((tm, w), lambda j, i: (i, 0))
    wcol = lambda k: pl.BlockSpec((k, tn), lambda j, i: (0, j))
    gate = lambda br: pl.BlockSpec((tm, tn), lambda j, i, br=br: (i, br * nj + j))
    return pl.pallas_call(
        _merge_body,
        grid=(nj, t // tm),
        in_specs=[rows(GM_WIDTH), rows(DIL_GROUP_WIDTH), rows(MEM_WIDTH),
                  wcol(GM_WIDTH), wcol(DIL_GROUP_WIDTH), wcol(MEM_WIDTH),
                  gate(0), gate(1), gate(2)],
        out_specs=pl.BlockSpec((tm, tn), lambda j, i: (i, j)),
        out_shape=jax.ShapeDtypeStruct((t, d), BF16),
        scratch_shapes=[pltpu.VMEM((GM_WIDTH, tn), BF16), pltpu.VMEM((DIL_GROUP_WIDTH, tn), BF16),
                        pltpu.VMEM((MEM_WIDTH, tn), BF16)],
        compiler_params=_params("arbitrary", "arbitrary"),
        name="branch_merge",
    )(a_out, b_out, m_out, w_a, w_b, w_m, gates, gates, gates)


def _sorting_network(n):
    pairs = []
    p = 1
    while p < n:
        k = p
        while k >= 1:
            for j in range(k % p, n - k, 2 * k):
                for i in range(min(k, n - j - k)):
                    if (i + j) // (2 * p) == (i + j + k) // (2 * p):
                        pairs.append((i + j, i + j + k))
            k //= 2
        p *= 2
    return pairs


SUBLANES = 8


def _top_values(x, k):
    n_groups = x.shape[0] // SUBLANES
    assert n_groups >= k
    g = [x[SUBLANES * j:SUBLANES * (j + 1), :] for j in range(n_groups)]
    for i, j in _sorting_network(n_groups):
        g[i], g[j] = jnp.maximum(g[i], g[j]), jnp.minimum(g[i], g[j])
    sub = lax.broadcasted_iota(jnp.int32, g[0].shape, 0)
    vals = []
    for r in range(k):
        m = jnp.max(g[0], axis=0, keepdims=True)
        vals.append(m)
        first = jnp.min(jnp.where(g[0] == m, sub, SUBLANES), axis=0, keepdims=True)
        hit = sub == first
        for j in range(k - 1 - r):
            g[j] = jnp.where(hit, g[j + 1], g[j])
    return jnp.concatenate(vals, axis=0)


def _staircase(a1, a2, combine, fill):
    r8 = lax.broadcasted_iota(jnp.int32, (8, a1.shape[1]), 0)
    r16 = lax.broadcasted_iota(jnp.int32, a1.shape, 0)
    lo8 = lambda x: x[0:8]
    return jnp.concatenate([
        combine(a1[0:1], a2),
        combine(a1[1:2], lo8(a2)),
        jnp.where(r8 < 5, combine(a1[2:3], lo8(a2)), fill),
        jnp.where(r8 < 4, combine(a1[3:4], lo8(a2)), fill),
        jnp.where(r16 >= 4, combine(a1, a2[0:1]), fill),
        jnp.where(r8 >= 4, combine(lo8(a1), a2[1:2]), fill),
        jnp.where(r8 == 4, combine(lo8(a1), a2[2:3]), fill),
    ], axis=0)


def _peer_route_body(q_ref, keys_ref, g_ref, cnt_ref, e1_ref, rank_ref, e2_ref):
    tm = q_ref.shape[0]
    k = PEER_TOPK
    for h in range(PEER_HEADS):
        sts, tops = [], []
        for p in range(2):
            hp = 2 * h + p
            qs = q_ref[:, hp * PEER_HALF:(hp + 1) * PEER_HALF]
            st = lax.dot_general(keys_ref[hp], qs, NT_DIMS, preferred_element_type=F32,
                                 precision=lax.Precision.HIGHEST)
            sts.append(st)
            tops.append(_top_values(st, k))
        a1, a2 = tops
        m1, m2 = a1[0:1], a2[0:1]
        cand = _staircase(a1, a2, jnp.add, -jnp.inf)
        prob = _staircase(jnp.exp(a1 - m1), jnp.exp(a2 - m2), jnp.multiply, 0.0)
        x = cand
        left = jnp.full((1, tm), float(k), F32)
        tau = jnp.full((1, tm), jnp.inf, F32)
        for _ in range(k):
            m = jnp.max(x, axis=0, keepdims=True)
            eq = x == m
            tau = jnp.where(left > 0, m, tau)
            left = left - jnp.sum(jnp.where(eq, 1.0, 0.0), axis=0, keepdims=True)
            x = jnp.where(eq, -jnp.inf, x)
        z = jnp.sum(jnp.where(cand >= tau, prob, 0.0), axis=0, keepdims=True)
        cnt = jnp.zeros((PEER_KEYS, tm), F32)
        rank = jnp.zeros((PEER_KEYS, tm), F32)
        for b in range(k):
            cnt = cnt + jnp.where(sts[0] + a2[b:b + 1] >= tau, 1.0, 0.0)
            rank = rank + jnp.where(a2[b:b + 1] > sts[1], 1.0, 0.0)
        cnt_ref[h] = cnt
        e1_ref[h] = jnp.exp(sts[0] - m1) / z
        rank_ref[h] = rank.astype(BF16)
        e2_ref[h] = jnp.exp(sts[1] - m2).astype(BF16)

    def tile16(row_f32):
        packed = jnp.broadcast_to(row_f32, (16, tm)).astype(BF16)
        return jnp.concatenate([packed] * (PEER_KEYS // 16), axis=0)

    def build(i1, carry):
        gate = None
        for h in range(PEER_HEADS):
            cnt = tile16(cnt_ref[h, pl.ds(i1, 1), :])
            e1 = tile16(e1_ref[h, pl.ds(i1, 1), :])
            sel = jnp.where(rank_ref[h] < cnt, e1 * e2_ref[h], jnp.zeros((), BF16))
            gate = sel if gate is None else gate + sel
        g_ref[pl.ds(pl.multiple_of(i1 * PEER_KEYS, PEER_KEYS), PEER_KEYS), :] = gate
        return carry

    lax.fori_loop(0, PEER_KEYS, build, 0)


def _peer_route(q, sub_keys, *, tm):
    t = q.shape[0]
    assert t % tm == 0
    hp = 2 * PEER_HEADS
    keys = sub_keys.reshape(hp, PEER_KEYS, PEER_HALF)
    n_exp = PEER_KEYS * PEER_KEYS
    per_head = lambda dt: pltpu.VMEM((PEER_HEADS, PEER_KEYS, tm), dt)
    return pl.pallas_call(
        _peer_route_body,
        grid=(t // tm,),
        in_specs=[pl.BlockSpec((tm, hp * PEER_HALF), lambda i: (i, 0)),
                  pl.BlockSpec((hp, PEER_KEYS, PEER_HALF), lambda i: (0, 0, 0))],
        out_specs=pl.BlockSpec((n_exp, tm), lambda i: (0, i)),
        out_shape=jax.ShapeDtypeStruct((n_exp, t), BF16),
        scratch_shapes=[per_head(F32), per_head(F32), per_head(BF16), per_head(BF16)],
        compiler_params=_params("arbitrary"),
        name="peer_route",
    )(q, keys)


def _peer_dense_body(xt_ref, u_ref, v_ref, g_ref, o_ref):
    @pl.when(pl.program_id(1) == 0)
    def _():
        o_ref[...] = jnp.zeros_like(o_ref)

    act_t = jnp.dot(u_ref[...], xt_ref[...], preferred_element_type=F32)
    inner = GELU_C * (act_t + 0.044715 * (act_t ** 3))
    cdf = 0.5 * (1.0 + jnp.tanh(inner.astype(BF16)))
    w_t = act_t.astype(BF16) * cdf * g_ref[...]
    o_ref[...] += jnp.dot(w_t.T, v_ref[...], preferred_element_type=F32)


def _peer_dense(h_t, u, v, gate_t, *, tm, te):
    d, t = h_t.shape
    n_exp = u.shape[0]
    assert t % tm == 0 and n_exp % te == 0
    once = dict(pipeline_mode=pl.Buffered(1))
    return pl.pallas_call(
        _peer_dense_body,
        grid=(t // tm, n_exp // te),
        in_specs=[pl.BlockSpec((d, tm), lambda i, e: (0, i), **once),
                  pl.BlockSpec((te, d), lambda i, e: (e, 0)),
                  pl.BlockSpec((te, d), lambda i, e: (e, 0)),
                  pl.BlockSpec((te, tm), lambda i, e: (e, i))],
        out_specs=pl.BlockSpec((tm, d), lambda i, e: (i, 0), **once),
        out_shape=jax.ShapeDtypeStruct((t, d), F32),
        compiler_params=_params("arbitrary", "arbitrary"),
        name="peer_dense",
    )(h_t, u, v, gate_t)


MM_TM, MM_TN = 1024, 512
ROW_TM = 256
DIL_BLOCKS_PER_STEP = 4
PEER_TM, PEER_TE = 1024, 512
ROUTE_TM = 256


def kernel(x, mem, positions, w_in, gm_ln_g, gm_ln_b, gm_w_s, gm_b_s, w_mem_kv, w_gate, b_gate,
           w_br_gmlp, w_br_dil, w_br_mem, w_out, ln_mix_g, ln_mix_b, peer_w_query, peer_sub_keys,
           peer_u, peer_v, ln_ffn_g, ln_ffn_b):
    batch, seq, d = x.shape
    t = batch * seq
    assert w_in.shape[0] == DEPTH and w_in.shape[2] == IN_WIDTH
    x2 = x.reshape(t, d)
    x_bf = x2.astype(BF16)
    mem_bf = mem.reshape(-1, d).astype(BF16)

    n_exp = PEER_KEYS * PEER_KEYS
    proj, v_bf = _matmul(x_bf, w_in.reshape(d, IN_WIDTH), out_dtype=BF16, tm=MM_TM, tn=MM_TN,
                         side=peer_v.reshape(n_exp, d), name="in_proj")
    gates, u_bf = _matmul(x_bf, w_gate.reshape(d, 3 * d), bias=b_gate.reshape(3 * d),
                          act="sigmoid", out_dtype=BF16, tm=MM_TM, tn=MM_TN,
                          side=peer_u.reshape(n_exp, d), name="gate_proj")
    mkv = _matmul(mem_bf, w_mem_kv.reshape(d, 2 * MEM_WIDTH), out_dtype=BF16, tm=MM_TM, tn=MM_TN,
                  name="mem_kv_proj")

    a_out = _gmlp(proj, gm_ln_g.reshape(-1), gm_ln_b.reshape(-1), gm_w_s.reshape(GM_GROUPS, BLOCK, BLOCK),
                  gm_b_s.reshape(GM_GROUPS, BLOCK), tm=512)

    cos, sin = _rope_table(positions, tm=1024)
    outs, lses = [], []
    for g, dil in enumerate(DILATIONS):
        qg, kg, vg = _rope_perm(proj, cos, sin, g, dil)
        o, lse = _dilated_attention(qg, kg, vg, dil, batch, seq, nblk=DIL_BLOCKS_PER_STEP)
        outs.append(o)
        lses.append(lse)
    b_out = _dil_merge(outs, lses)

    m_out = _mem_attention(proj, mkv, batch, seq, tm=512)

    merged = _merge(a_out, b_out, m_out, w_br_gmlp.reshape(GM_WIDTH, d),
                    w_br_dil.reshape(DIL_GROUP_WIDTH, d), w_br_mem.reshape(MEM_WIDTH, d), gates,
                    tm=MM_TM, tn=MM_TN)
    z1 = _matmul(merged, w_out.reshape(d, d), res=x2, alpha=ALPHA, out_dtype=F32, tm=MM_TM,
                 tn=MM_TN, name="out_proj")
    h1, h1_bf, h1_t = _layer_norm(z1, ln_mix_g.reshape(-1), ln_mix_b.reshape(-1), want_t=True,
                                  tm=ROW_TM, name="ln_mix")

    hq = PEER_HEADS * 2 * PEER_HALF
    q = _matmul(h1_bf, peer_w_query.reshape(d, hq), out_dtype=F32, tm=MM_TM, tn=MM_TN,
                name="peer_query")
    gate_t = _peer_route(q, peer_sub_keys, tm=ROUTE_TM)
    y = _peer_dense(h1_t, u_bf, v_bf, gate_t, tm=PEER_TM, te=PEER_TE)
    (out,) = _layer_norm(h1, ln_ffn_g.reshape(-1), ln_ffn_b.reshape(-1), add=y, alpha=ALPHA,
                         tm=ROW_TM, name="ln_ffn")
    return out.reshape(batch, seq, d)
```

```python
import functools

import numpy as np
import jax
import jax.numpy as jnp
from jax import lax
from jax.experimental import pallas as pl
from jax.experimental.pallas import tpu as pltpu

F32 = jnp.float32
BF16 = jnp.bfloat16

HEAD_DIM = 128
BLOCK = 128
GM_GROUPS = 12
GM_WIDTH = GM_GROUPS * 128
DILATIONS = (1, 4, 16)
DIL_HEADS = 4
DIL_GROUP_WIDTH = DIL_HEADS * HEAD_DIM
DIL_WIDTH = len(DILATIONS) * DIL_GROUP_WIDTH
MEM_HEADS = 4
MEM_HEAD_DIM = 256
MEM_WIDTH = MEM_HEADS * MEM_HEAD_DIM
PEER_HEADS = 8
PEER_KEYS = 128
PEER_TOPK = 16
PEER_HALF = 128
ROPE_THETA = 10000.0
LN_EPS = 1e-5
DEPTH = 1
ALPHA = (2 * DEPTH) ** 0.25
NEG_INF = -1e30

OFF_GU, OFF_GV = 0, GM_WIDTH
OFF_DQ = 2 * GM_WIDTH
OFF_DK = OFF_DQ + DIL_WIDTH
OFF_DV = OFF_DK + DIL_WIDTH
OFF_MQ = OFF_DV + DIL_WIDTH
IN_WIDTH = OFF_MQ + MEM_WIDTH

V7X_VMEM_BYTES = 64 * 1024 * 1024
VMEM_LIMIT = V7X_VMEM_BYTES - 8 * 1024 * 1024

NT_DIMS = (((1,), (1,)), ((), ()))
SIDE_ROWS = 128
GELU_C = float(np.float32(np.sqrt(2.0 / np.pi)))


def _params(*sem):
    return pltpu.CompilerParams(dimension_semantics=sem, vmem_limit_bytes=VMEM_LIMIT)


def _mm_body(*refs, act, alpha, has_bias, has_res, has_side):
    x_ref, w_ref = refs[0], refs[1]
    k = 2
    b_ref = r_ref = s_ref = None
    if has_bias:
        b_ref = refs[k]
        k += 1
    if has_res:
        r_ref = refs[k]
        k += 1
    if has_side:
        s_ref = refs[k]
        k += 1
    o_ref = refs[k]
    k += 1
    if has_side:
        refs[k][...] = s_ref[...].astype(BF16)
        k += 1
    wb_ref = refs[k]

    @pl.when(pl.program_id(1) == 0)
    def _():
        wb_ref[...] = w_ref[...].astype(BF16)

    acc = jnp.dot(x_ref[...], wb_ref[...], preferred_element_type=F32)
    if has_bias:
        acc = acc + b_ref[...]
    if act == "sigmoid":
        acc = 0.5 * jnp.tanh(0.5 * acc.astype(BF16)) + 0.5
    if has_res:
        acc = alpha * r_ref[...] + acc
    o_ref[...] = acc.astype(o_ref.dtype)


def _matmul(x, w, *, out_dtype, tm, tn, bias=None, res=None, alpha=1.0, act=None, side=None, name):
    m, k = x.shape
    n = w.shape[1]
    tm, tn = min(tm, m), min(tn, n)
    assert m % tm == 0 and n % tn == 0
    steps_i = m // tm
    in_specs = [pl.BlockSpec((tm, k), lambda j, i: (i, 0)),
                pl.BlockSpec((k, tn), lambda j, i: (0, j))]
    args = [x, w]
    if bias is not None:
        in_specs.append(pl.BlockSpec((1, tn), lambda j, i: (0, j)))
        args.append(bias.reshape(1, n))
    if res is not None:
        in_specs.append(pl.BlockSpec((tm, tn), lambda j, i: (i, j)))
        args.append(res)
    out_specs = [pl.BlockSpec((tm, tn), lambda j, i: (i, j))]
    out_shape = [jax.ShapeDtypeStruct((m, n), out_dtype)]
    if side is not None:
        rows, width = side.shape
        side_rows = SIDE_ROWS
        while rows // side_rows > (n // tn) * steps_i:
            side_rows *= 2
        n_blocks = rows // side_rows
        assert rows % side_rows == 0
        spec = pl.BlockSpec((side_rows, width),
                            lambda j, i: (jnp.minimum(j * steps_i + i, n_blocks - 1), 0))
        in_specs.append(spec)
        args.append(side)
        out_specs.append(spec)
        out_shape.append(jax.ShapeDtypeStruct(side.shape, BF16))
    body = functools.partial(_mm_body, act=act, alpha=alpha, has_bias=bias is not None,
                             has_res=res is not None, has_side=side is not None)
    outs = pl.pallas_call(
        body,
        grid=(n // tn, steps_i),
        in_specs=in_specs,
        out_specs=out_specs,
        out_shape=out_shape,
        scratch_shapes=[pltpu.VMEM((k, tn), BF16)],
        compiler_params=_params("arbitrary", "arbitrary"),
        name=name,
    )(*args)
    return outs if side is not None else outs[0]


def _ln_rows(z, g, b):
    mu = jnp.mean(z, axis=-1, keepdims=True)
    zc = z - mu
    var = jnp.mean(zc * zc, axis=-1, keepdims=True)
    return zc * lax.rsqrt(var + LN_EPS) * g + b


def _ln_body(*refs, alpha, has_add, want_t):
    k = 0
    z = refs[k][...]
    k += 1
    if has_add:
        z = alpha * z + refs[k][...]
        k += 1
    g_ref, b_ref = refs[k], refs[k + 1]
    k += 2
    h = _ln_rows(z, g_ref[...], b_ref[...])
    refs[k][...] = h
    if want_t:
        refs[k + 1][...] = h.astype(BF16)
        refs[k + 2][...] = h.T.astype(BF16)


def _layer_norm(z, g, b, *, add=None, alpha=1.0, want_t=False, tm, name):
    t, d = z.shape
    tm = min(tm, t)
    assert t % tm == 0
    row = pl.BlockSpec((tm, d), lambda i: (i, 0))
    vec = pl.BlockSpec((1, d), lambda i: (0, 0))
    in_specs, args = [row], [z]
    if add is not None:
        in_specs.append(row)
        args.append(add)
    in_specs += [vec, vec]
    args += [g.reshape(1, d), b.reshape(1, d)]
    out_specs, out_shape = [row], [jax.ShapeDtypeStruct((t, d), F32)]
    if want_t:
        out_specs += [row, pl.BlockSpec((d, tm), lambda i: (0, i))]
        out_shape += [jax.ShapeDtypeStruct((t, d), BF16), jax.ShapeDtypeStruct((d, t), BF16)]
    return pl.pallas_call(
        functools.partial(_ln_body, alpha=alpha, has_add=add is not None, want_t=want_t),
        grid=(t // tm,),
        in_specs=in_specs,
        out_specs=out_specs,
        out_shape=out_shape,
        compiler_params=_params("arbitrary"),
        name=name,
    )(*args)


def _gmlp_body(gu_ref, gv_ref, g_ref, b_ref, ws_ref, bs_ref, o_ref):
    tm = gu_ref.shape[0]
    u = jax.nn.gelu(gu_ref[...].astype(F32))
    v = _ln_rows(jax.nn.gelu(gv_ref[...].astype(F32)), g_ref[...], b_ref[...]).astype(BF16)
    causal = (lax.broadcasted_iota(jnp.int32, (BLOCK, BLOCK), 0)
              >= lax.broadcasted_iota(jnp.int32, (BLOCK, BLOCK), 1))
    for g in range(GM_GROUPS):
        cols = slice(g * 128, (g + 1) * 128)
        w = jnp.where(causal, ws_ref[g], 0.0).astype(BF16)
        for c in range(tm // BLOCK):
            rows = slice(c * BLOCK, (c + 1) * BLOCK)
            mixed = jnp.dot(w, v[rows, cols], preferred_element_type=F32) + bs_ref[g]
            o_ref[rows, cols] = (u[rows, cols] * mixed).astype(o_ref.dtype)


def _gmlp(proj, ln_g, ln_b, w_s, b_s, *, tm):
    t = proj.shape[0]
    assert t % tm == 0 and tm % BLOCK == 0
    bs_full = jnp.broadcast_to(b_s[:, :, None], (GM_GROUPS, BLOCK, 128))
    return pl.pallas_call(
        _gmlp_body,
        grid=(t // tm,),
        in_specs=[pl.BlockSpec((tm, GM_WIDTH), lambda i: (i, OFF_GU // GM_WIDTH)),
                  pl.BlockSpec((tm, GM_WIDTH), lambda i: (i, OFF_GV // GM_WIDTH)),
                  pl.BlockSpec((1, GM_WIDTH), lambda i: (0, 0)),
                  pl.BlockSpec((1, GM_WIDTH), lambda i: (0, 0)),
                  pl.BlockSpec((GM_GROUPS, BLOCK, BLOCK), lambda i: (0, 0, 0)),
                  pl.BlockSpec((GM_GROUPS, BLOCK, 128), lambda i: (0, 0, 0))],
        out_specs=pl.BlockSpec((tm, GM_WIDTH), lambda i: (i, 0)),
        out_shape=jax.ShapeDtypeStruct((t, GM_WIDTH), BF16),
        compiler_params=_params("arbitrary"),
        name="gmlp",
    )(proj, proj, ln_g.reshape(1, GM_WIDTH), ln_b.reshape(1, GM_WIDTH), w_s, bs_full)


PERM_TILE = BLOCK * max(DILATIONS)


def _rope_table_body(pos_ref, inv_ref, cos_ref, sin_ref):
    half = HEAD_DIM // 2
    ang = pos_ref[...] * inv_ref[...]
    lane = lax.broadcasted_iota(jnp.int32, ang.shape, 1)
    sin = jnp.sin(ang)
    cos_ref[...] = jnp.cos(ang)
    sin_ref[...] = jnp.where(lane < half, -sin, sin)


def _rope_table(positions, *, tm):
    t = positions.size
    assert t % tm == 0
    half = HEAD_DIM // 2
    inv_freq = ROPE_THETA ** (-jnp.arange(half, dtype=F32) / half)
    inv = jnp.concatenate([inv_freq, inv_freq]).reshape(1, HEAD_DIM)
    pos = positions.astype(F32).reshape(t, 1)
    out = jax.ShapeDtypeStruct((t, HEAD_DIM), F32)
    return pl.pallas_call(
        _rope_table_body,
        grid=(t // tm,),
        in_specs=[pl.BlockSpec((tm, 1), lambda i: (i, 0)),
                  pl.BlockSpec((1, HEAD_DIM), lambda i: (0, 0))],
        out_specs=[pl.BlockSpec((tm, HEAD_DIM), lambda i: (i, 0))] * 2,
        out_shape=[out, out],
        compiler_params=_params("arbitrary"),
        name="rope_table",
    )(pos, inv)


def _rope_perm_body(q_ref, k_ref, v_ref, cos_ref, sin_ref, qo_ref, ko_ref, vo_ref, buf_ref, *, d):
    tm = q_ref.shape[0]
    half = HEAD_DIM // 2
    span = BLOCK * d
    for h in range(DIL_HEADS):
        cols = slice(h * HEAD_DIM, (h + 1) * HEAD_DIM)
        for src, dst, rotate in ((q_ref, qo_ref, True), (k_ref, ko_ref, True), (v_ref, vo_ref, False)):
            t = src[:, cols].astype(F32)
            if rotate:
                swapped = jnp.concatenate([t[:, half:], t[:, :half]], axis=1)
                t = t * cos_ref[...] + swapped * sin_ref[...]
            if d == 1:
                dst[:, cols] = t.astype(dst.dtype)
                continue
            buf_ref[...] = t
            for n in range(tm // span):
                for r in range(d):
                    rows = buf_ref[pl.ds(n * span + r, BLOCK, stride=d), :]
                    dst[n * span + r * BLOCK:n * span + (r + 1) * BLOCK, cols] = rows.astype(dst.dtype)


def _rope_perm(proj, cos, sin, group, d):
    t = proj.shape[0]
    tm, gw = PERM_TILE, DIL_GROUP_WIDTH
    assert t % tm == 0 and tm % (BLOCK * d) == 0
    col = lambda off: pl.BlockSpec((tm, gw), lambda i: (i, off // gw + group))
    tab = pl.BlockSpec((tm, HEAD_DIM), lambda i: (i, 0))
    out = jax.ShapeDtypeStruct((t, gw), BF16)
    return pl.pallas_call(
        functools.partial(_rope_perm_body, d=d),
        grid=(t // tm,),
        in_specs=[col(OFF_DQ), col(OFF_DK), col(OFF_DV), tab, tab],
        out_specs=[pl.BlockSpec((tm, gw), lambda i: (i, 0))] * 3,
        out_shape=[out, out, out],
        scratch_shapes=[pltpu.VMEM((tm, HEAD_DIM), F32)],
        compiler_params=_params("arbitrary"),
        name=f"rope_perm_d{d}",
    )(proj, proj, proj, cos, sin)


def _dil_body(q_ref, kc_ref, vc_ref, kp_ref, vp_ref, o_ref, lse_ref, *, d, nblk):
    step = pl.program_id(1)
    qi = lax.broadcasted_iota(jnp.int32, (BLOCK, BLOCK), 0)
    ki = lax.broadcasted_iota(jnp.int32, (BLOCK, BLOCK), 1)
    mask_cur = ki <= qi
    scale = HEAD_DIM ** -0.5
    for i in range(nblk):
        rows = slice(i * BLOCK, (i + 1) * BLOCK)
        if d >= nblk:
            kpr, vpr, prow, has_prev = kp_ref, vp_ref, rows, step >= d // nblk
        elif i == 0:
            kpr, vpr, prow, has_prev = kp_ref, vp_ref, slice(0, BLOCK), step > 0
        else:
            kpr, vpr, prow, has_prev = kc_ref, vc_ref, slice((i - 1) * BLOCK, i * BLOCK), None
        mask_prev = ki >= (qi if has_prev is None else qi + jnp.where(has_prev, 0, BLOCK))
        mask = jnp.concatenate([mask_prev, mask_cur], axis=1)
        for h in range(DIL_HEADS):
            cols = slice(h * HEAD_DIM, (h + 1) * HEAD_DIM)
            keys = jnp.concatenate([kpr[prow, cols], kc_ref[rows, cols]], axis=0)
            vals = jnp.concatenate([vpr[prow, cols], vc_ref[rows, cols]], axis=0)
            s = lax.dot_general(q_ref[rows, cols], keys, NT_DIMS, preferred_element_type=F32)
            s = jnp.where(mask, s * scale, NEG_INF)
            m = jnp.max(s, axis=-1, keepdims=True)
            p = jnp.exp(s - m).astype(BF16)
            v_aug = jnp.concatenate([vals, jnp.ones_like(vals)], axis=1)
            o_aug = jnp.dot(p, v_aug, preferred_element_type=F32)
            den = o_aug[:, HEAD_DIM:]
            o_ref[rows, cols] = o_aug[:, :HEAD_DIM] / den
            lse_ref[rows, cols] = m + jnp.log(den)


def _dilated_attention(q, k, v, d, batch, seq, *, nblk):
    gw = DIL_GROUP_WIDTH
    rows = nblk * BLOCK
    assert seq % rows == 0 and (d == 1 or d % nblk == 0)
    steps = seq // rows
    cur = pl.BlockSpec((rows, gw), lambda b, s: (b * steps + s, 0))
    if d >= nblk:
        back = d // nblk
        prev = pl.BlockSpec((rows, gw), lambda b, s: (b * steps + jnp.maximum(s - back, 0), 0))
    else:
        prev = pl.BlockSpec((BLOCK, gw),
                            lambda b, s: (b * steps * nblk + jnp.maximum(s * nblk - 1, 0), 0))
    out = jax.ShapeDtypeStruct((batch * seq, gw), F32)
    return pl.pallas_call(
        functools.partial(_dil_body, d=d, nblk=nblk),
        grid=(batch, steps),
        in_specs=[cur, cur, cur, prev, prev],
        out_specs=[cur, cur],
        out_shape=[out, out],
        compiler_params=_params("arbitrary", "arbitrary"),
        name=f"dil_attn_d{d}",
    )(q, k, v, k, v)


def _dil_merge_body(*refs):
    n = len(DILATIONS)
    o_refs, l_refs, out_ref, bufs = refs[:n], refs[n:2 * n], refs[2 * n], refs[2 * n + 1:]
    tm = out_ref.shape[0]
    outs, lses = [], []
    k = 0
    for d, o_ref, l_ref in zip(DILATIONS, o_refs, l_refs):
        if d == 1:
            outs.append(o_ref[...])
            lses.append(l_ref[...])
            continue
        span = BLOCK * d
        for src, buf in ((o_ref, bufs[k]), (l_ref, bufs[k + 1])):
            for s in range(tm // span):
                for r in range(d):
                    buf[pl.ds(s * span + r, BLOCK, stride=d), :] = (
                        src[s * span + r * BLOCK:s * span + (r + 1) * BLOCK, :])
        outs.append(bufs[k][...])
        lses.append(bufs[k + 1][...])
        k += 2
    m = functools.reduce(jnp.maximum, lses)
    es = [jnp.exp(l - m) for l in lses]
    num = sum(e * o for e, o in zip(es, outs))
    out_ref[...] = (num / sum(es)).astype(out_ref.dtype)


def _dil_merge(outs, lses):
    t, w = outs[0].shape
    tm = PERM_TILE
    assert t % tm == 0
    spec = pl.BlockSpec((tm, HEAD_DIM), lambda i, h: (i, h))
    n_buf = 2 * sum(1 for d in DILATIONS if d != 1)
    return pl.pallas_call(
        _dil_merge_body,
        grid=(t // tm, w // HEAD_DIM),
        in_specs=[spec] * (2 * len(DILATIONS)),
        out_specs=spec,
        out_shape=jax.ShapeDtypeStruct((t, w), BF16),
        scratch_shapes=[pltpu.VMEM((tm, HEAD_DIM), F32)] * n_buf,
        compiler_params=_params("arbitrary", "arbitrary"),
        name="dil_merge",
    )(*outs, *lses)


def _mem_body(q0, q1, q2, q3, kv_ref, o_ref):
    scale = MEM_HEAD_DIM ** -0.5
    for h, q_ref in enumerate((q0, q1, q2, q3)):
        cols = slice(h * MEM_HEAD_DIM, (h + 1) * MEM_HEAD_DIM)
        k = kv_ref[:, cols]
        v = kv_ref[:, MEM_WIDTH + h * MEM_HEAD_DIM:MEM_WIDTH + (h + 1) * MEM_HEAD_DIM]
        s = lax.dot_general(q_ref[...], k, NT_DIMS, preferred_element_type=F32) * scale
        m = jnp.max(s, axis=-1, keepdims=True)
        p = jnp.exp(s - m)
        den = jnp.sum(p, axis=-1, keepdims=True)
        o = jnp.dot(p.astype(BF16), v, preferred_element_type=F32) / den
        o_ref[:, cols] = o.astype(o_ref.dtype)


def _mem_attention(proj, mkv, batch, seq, *, tm):
    mem_len = mkv.shape[0] // batch
    assert seq % tm == 0
    steps = seq // tm
    q_specs = [pl.BlockSpec((tm, MEM_HEAD_DIM),
                            lambda b, i, h=h: (b * steps + i, OFF_MQ // MEM_HEAD_DIM + h))
               for h in range(MEM_HEADS)]
    return pl.pallas_call(
        _mem_body,
        grid=(batch, steps),
        in_specs=q_specs + [pl.BlockSpec((mem_len, 2 * MEM_WIDTH), lambda b, i: (b, 0))],
        out_specs=pl.BlockSpec((tm, MEM_WIDTH), lambda b, i: (b * steps + i, 0)),
        out_shape=jax.ShapeDtypeStruct((batch * seq, MEM_WIDTH), BF16),
        compiler_params=_params("arbitrary", "arbitrary"),
        name="mem_attn",
    )(proj, proj, proj, proj, mkv)


def _merge_body(a_ref, b_ref, m_ref, wa_ref, wb_ref, wm_ref, g0_ref, g1_ref, g2_ref, o_ref,
                wa_bf, wb_bf, wm_bf):
    @pl.when(pl.program_id(1) == 0)
    def _():
        wa_bf[...] = wa_ref[...].astype(BF16)
        wb_bf[...] = wb_ref[...].astype(BF16)
        wm_bf[...] = wm_ref[...].astype(BF16)

    ya = jnp.dot(a_ref[...], wa_bf[...], preferred_element_type=F32)
    yb = jnp.dot(b_ref[...], wb_bf[...], preferred_element_type=F32)
    ym = jnp.dot(m_ref[...], wm_bf[...], preferred_element_type=F32)
    o_ref[...] = (g0_ref[...] * ya.astype(BF16) + g1_ref[...] * yb.astype(BF16)
                  + g2_ref[...] * ym.astype(BF16))


def _merge(a_out, b_out, m_out, w_a, w_b, w_m, gates, *, tm, tn):
    t = a_out.shape[0]
    d = w_a.shape[1]
    assert t % tm == 0 and d % tn == 0
    nj = d // tn
    rows = lambda w: pl.BlockSpec((tm, w), lambda j, i: (i, 0))
    wcol = lambda k: pl.BlockSpec((k, tn), lambda j, i: (0, j))
    gate = lambda br: pl.BlockSpec((tm, tn), lambda j, i, br=br: (i, br * nj + j))
    return pl.pallas_call(
        _merge_body,
        grid=(nj, t // tm),
        in_specs=[rows(GM_WIDTH), rows(DIL_GROUP_WIDTH), rows(MEM_WIDTH),
                  wcol(GM_WIDTH), wcol(DIL_GROUP_WIDTH), wcol(MEM_WIDTH),
                  gate(0), gate(1), gate(2)],
        out_specs=pl.BlockSpec((tm, tn), lambda j, i: (i, j)),
        out_shape=jax.ShapeDtypeStruct((t, d), BF16),
        scratch_shapes=[pltpu.VMEM((GM_WIDTH, tn), BF16), pltpu.VMEM((DIL_GROUP_WIDTH, tn), BF16),
                        pltpu.VMEM((MEM_WIDTH, tn), BF16)],
        compiler_params=_params("arbitrary", "arbitrary"),
        name="branch_merge",
    )(a_out, b_out, m_out, w_a, w_b, w_m, gates, gates, gates)


def _sorting_network(n):
    pairs = []
    p = 1
    while p < n:
        k = p
        while k >= 1:
            for j in range(k % p, n - k, 2 * k):
                for i in range(min(k, n - j - k)):
                    if (i + j) // (2 * p) == (i + j + k) // (2 * p):
                        pairs.append((i + j, i + j + k))
            k //= 2
        p *= 2
    return pairs


SUBLANES = 8


def _top_values(x, k):
    n_groups = x.shape[0] // SUBLANES
    assert n_groups >= k
    g = [x[SUBLANES * j:SUBLANES * (j + 1), :] for j in range(n_groups)]
    for i, j in _sorting_network(n_groups):
        g[i], g[j] = jnp.maximum(g[i], g[j]), jnp.minimum(g[i], g[j])
    sub = lax.broadcasted_iota(jnp.int32, g[0].shape, 0)
    vals = []
    for r in range(k):
        m = jnp.max(g[0], axis=0, keepdims=True)
        vals.append(m)
        first = jnp.min(jnp.where(g[0] == m, sub, SUBLANES), axis=0, keepdims=True)
        hit = sub == first
        for j in range(k - 1 - r):
            g[j] = jnp.where(hit, g[j + 1], g[j])
    return jnp.concatenate(vals, axis=0)


def _staircase(a1, a2, combine, fill):
    r8 = lax.broadcasted_iota(jnp.int32, (8, a1.shape[1]), 0)
    r16 = lax.broadcasted_iota(jnp.int32, a1.shape, 0)
    lo8 = lambda x: x[0:8]
    return jnp.concatenate([
        combine(a1[0:1], a2),
        combine(a1[1:2], lo8(a2)),
        jnp.where(r8 < 5, combine(a1[2:3], lo8(a2)), fill),
        jnp.where(r8 < 4, combine(a1[3:4], lo8(a2)), fill),
        jnp.where(r16 >= 4, combine(a1, a2[0:1]), fill),
        jnp.where(r8 >= 4, combine(lo8(a1), a2[1:2]), fill),
        jnp.where(r8 == 4, combine(lo8(a1), a2[2:3]), fill),
    ], axis=0)


def _peer_route_body(q_ref, keys_ref, g_ref, cnt_ref, e1_ref, rank_ref, e2_ref):
    tm = q_ref.shape[0]
    k = PEER_TOPK
    for h in range(PEER_HEADS):
        sts, tops = [], []
        for p in range(2):
            hp = 2 * h + p
            qs = q_ref[:, hp * PEER_HALF:(hp + 1) * PEER_HALF]
            st = lax.dot_general(keys_ref[hp], qs, NT_DIMS, preferred_element_type=F32,
                                 precision=lax.Precision.HIGHEST)
            sts.append(st)
            tops.append(_top_values(st, k))
        a1, a2 = tops
        m1, m2 = a1[0:1], a2[0:1]
        cand = _staircase(a1, a2, jnp.add, -jnp.inf)
        prob = _staircase(jnp.exp(a1 - m1), jnp.exp(a2 - m2), jnp.multiply, 0.0)
        x = cand
        left = jnp.full((1, tm), float(k), F32)
        tau = jnp.full((1, tm), jnp.inf, F32)
        for _ in range(k):
            m = jnp.max(x, axis=0, keepdims=True)
            eq = x == m
            tau = jnp.where(left > 0, m, tau)
            left = left - jnp.sum(jnp.where(eq, 1.0, 0.0), axis=0, keepdims=True)
            x = jnp.where(eq, -jnp.inf, x)
        z = jnp.sum(jnp.where(cand >= tau, prob, 0.0), axis=0, keepdims=True)
        cnt = jnp.zeros((PEER_KEYS, tm), F32)
        rank = jnp.zeros((PEER_KEYS, tm), F32)
        for b in range(k):
            cnt = cnt + jnp.where(sts[0] + a2[b:b + 1] >= tau, 1.0, 0.0)
            rank = rank + jnp.where(a2[b:b + 1] > sts[1], 1.0, 0.0)
        cnt_ref[h] = cnt
        e1_ref[h] = jnp.exp(sts[0] - m1) / z
        rank_ref[h] = rank.astype(BF16)
        e2_ref[h] = jnp.exp(sts[1] - m2).astype(BF16)

    def tile16(row_f32):
        packed = jnp.broadcast_to(row_f32, (16, tm)).astype(BF16)
        return jnp.concatenate([packed] * (PEER_KEYS // 16), axis=0)

    def build(j, carry):
        base = pl.multiple_of(j * SUBLANES, SUBLANES)
        cnts = [cnt_ref[h, pl.ds(base, SUBLANES), :] for h in range(PEER_HEADS)]
        e1s = [e1_ref[h, pl.ds(base, SUBLANES), :] for h in range(PEER_HEADS)]
        for s in range(SUBLANES):
            gate = None
            for h in range(PEER_HEADS):
                cnt = tile16(cnts[h][s:s + 1])
                e1 = tile16(e1s[h][s:s + 1])
                sel = jnp.where(rank_ref[h] < cnt, e1 * e2_ref[h], jnp.zeros((), BF16))
                gate = sel if gate is None else gate + sel
            g_ref[pl.ds(pl.multiple_of((base + s) * PEER_KEYS, PEER_KEYS), PEER_KEYS), :] = gate
        return carry

    lax.fori_loop(0, PEER_KEYS // SUBLANES, build, 0)


def _peer_route(q, sub_keys, *, tm):
    t = q.shape[0]
    assert t % tm == 0
    hp = 2 * PEER_HEADS
    keys = sub_keys.reshape(hp, PEER_KEYS, PEER_HALF)
    n_exp = PEER_KEYS * PEER_KEYS
    per_head = lambda dt: pltpu.VMEM((PEER_HEADS, PEER_KEYS, tm), dt)
    return pl.pallas_call(
        _peer_route_body,
        grid=(t // tm,),
        in_specs=[pl.BlockSpec((tm, hp * PEER_HALF), lambda i: (i, 0)),
                  pl.BlockSpec((hp, PEER_KEYS, PEER_HALF), lambda i: (0, 0, 0))],
        out_specs=pl.BlockSpec((n_exp, tm), lambda i: (0, i)),
        out_shape=jax.ShapeDtypeStruct((n_exp, t), BF16),
        scratch_shapes=[per_head(F32), per_head(F32), per_head(BF16), per_head(BF16)],
        compiler_params=_params("arbitrary"),
        name="peer_route",
    )(q, keys)


def _peer_dense_body(xt_ref, u_ref, v_ref, g_ref, o_ref):
    @pl.when(pl.program_id(1) == 0)
    def _():
        o_ref[...] = jnp.zeros_like(o_ref)

    act_t = jnp.dot(u_ref[...], xt_ref[...], preferred_element_type=F32)
    inner = GELU_C * (act_t + 0.044715 * (act_t ** 3))
    cdf = 0.5 * (1.0 + jnp.tanh(inner.astype(BF16)))
    w_t = act_t.astype(BF16) * cdf * g_ref[...]
    o_ref[...] += jnp.dot(w_t.T, v_ref[...], preferred_element_type=F32)


def _peer_dense(h_t, u, v, gate_t, *, tm, te):
    d, t = h_t.shape
    n_exp = u.shape[0]
    assert t % tm == 0 and n_exp % te == 0
    once = dict(pipeline_mode=pl.Buffered(1))
    return pl.pallas_call(
        _peer_dense_body,
        grid=(t // tm, n_exp // te),
        in_specs=[pl.BlockSpec((d, tm), lambda i, e: (0, i), **once),
                  pl.BlockSpec((te, d), lambda i, e: (e, 0)),
                  pl.BlockSpec((te, d), lambda i, e: (e, 0)),
                  pl.BlockSpec((te, tm), lambda i, e: (e, i))],
        out_specs=pl.BlockSpec((tm, d), lambda i, e: (i, 0), **once),
        out_shape=jax.ShapeDtypeStruct((t, d), F32),
        compiler_params=_params("arbitrary", "arbitrary"),
        name="peer_dense",
    )(h_t, u, v, gate_t)


MM_TM, MM_TN = 1024, 512
ROW_TM = 256
DIL_BLOCKS_PER_STEP = 4
PEER_TM, PEER_TE = 1024, 512
ROUTE_TM = 256


def kernel(x, mem, positions, w_in, gm_ln_g, gm_ln_b, gm_w_s, gm_b_s, w_mem_kv, w_gate, b_gate,
           w_br_gmlp, w_br_dil, w_br_mem, w_out, ln_mix_g, ln_mix_b, peer_w_query, peer_sub_keys,
           peer_u, peer_v, ln_ffn_g, ln_ffn_b):
    batch, seq, d = x.shape
    t = batch * seq
    assert w_in.shape[0] == DEPTH and w_in.shape[2] == IN_WIDTH
    x2 = x.reshape(t, d)
    x_bf = x2.astype(BF16)
    mem_bf = mem.reshape(-1, d).astype(BF16)

    n_exp = PEER_KEYS * PEER_KEYS
    proj, v_bf = _matmul(x_bf, w_in.reshape(d, IN_WIDTH), out_dtype=BF16, tm=MM_TM, tn=MM_TN,
                         side=peer_v.reshape(n_exp, d), name="in_proj")
    gates, u_bf = _matmul(x_bf, w_gate.reshape(d, 3 * d), bias=b_gate.reshape(3 * d),
                          act="sigmoid", out_dtype=BF16, tm=MM_TM, tn=MM_TN,
                          side=peer_u.reshape(n_exp, d), name="gate_proj")
    mkv = _matmul(mem_bf, w_mem_kv.reshape(d, 2 * MEM_WIDTH), out_dtype=BF16, tm=MM_TM, tn=MM_TN,
                  name="mem_kv_proj")

    a_out = _gmlp(proj, gm_ln_g.reshape(-1), gm_ln_b.reshape(-1), gm_w_s.reshape(GM_GROUPS, BLOCK, BLOCK),
                  gm_b_s.reshape(GM_GROUPS, BLOCK), tm=512)

    cos, sin = _rope_table(positions, tm=1024)
    outs, lses = [], []
    for g, dil in enumerate(DILATIONS):
        qg, kg, vg = _rope_perm(proj, cos, sin, g, dil)
        o, lse = _dilated_attention(qg, kg, vg, dil, batch, seq, nblk=DIL_BLOCKS_PER_STEP)
        outs.append(o)
        lses.append(lse)
    b_out = _dil_merge(outs, lses)

    m_out = _mem_attention(proj, mkv, batch, seq, tm=512)

    merged = _merge(a_out, b_out, m_out, w_br_gmlp.reshape(GM_WIDTH, d),
                    w_br_dil.reshape(DIL_GROUP_WIDTH, d), w_br_mem.reshape(MEM_WIDTH, d), gates,
                    tm=MM_TM, tn=MM_TN)
    z1 = _matmul(merged, w_out.reshape(d, d), res=x2, alpha=ALPHA, out_dtype=F32, tm=MM_TM,
                 tn=MM_TN, name="out_proj")
    h1, h1_bf, h1_t = _layer_norm(z1, ln_mix_g.reshape(-1), ln_mix_b.reshape(-1), want_t=True,
                                  tm=ROW_TM, name="ln_mix")

    hq = PEER_HEADS * 2 * PEER_HALF
    q = _matmul(h1_bf, peer_w_query.reshape(d, hq), out_dtype=F32, tm=MM_TM, tn=MM_TN,
                name="peer_query")
    gate_t = _peer_route(q, peer_sub_keys, tm=ROUTE_TM)
    y = _peer_dense(h1_t, u_bf, v_bf, gate_t, tm=PEER_TM, te=PEER_TE)
    (out,) = _layer_norm(h1, ln_ffn_g.reshape(-1), ln_ffn_b.reshape(-1), add=y, alpha=ALPHA,
                         tm=ROW_TM, name="ln_ffn")
    return out.reshape(batch, seq, d)
```

```python
import functools

import numpy as np
import jax
import jax.numpy as jnp
from jax import lax
from jax.experimental import pallas as pl
from jax.experimental.pallas import tpu as pltpu

F32 = jnp.float32
BF16 = jnp.bfloat16

HEAD_DIM = 128
BLOCK = 128
GM_GROUPS = 12
GM_WIDTH = GM_GROUPS * 128
DILATIONS = (1, 4, 16)
DIL_HEADS = 4
DIL_GROUP_WIDTH = DIL_HEADS * HEAD_DIM
DIL_WIDTH = len(DILATIONS) * DIL_GROUP_WIDTH
MEM_HEADS = 4
MEM_HEAD_DIM = 256
MEM_WIDTH = MEM_HEADS * MEM_HEAD_DIM
PEER_HEADS = 8
PEER_KEYS = 128
PEER_TOPK = 16
PEER_HALF = 128
ROPE_THETA = 10000.0
LN_EPS = 1e-5
DEPTH = 1
ALPHA = (2 * DEPTH) ** 0.25
NEG_INF = -1e30

OFF_GU, OFF_GV = 0, GM_WIDTH
OFF_DQ = 2 * GM_WIDTH
OFF_DK = OFF_DQ + DIL_WIDTH
OFF_DV = OFF_DK + DIL_WIDTH
OFF_MQ = OFF_DV + DIL_WIDTH
IN_WIDTH = OFF_MQ + MEM_WIDTH

V7X_VMEM_BYTES = 64 * 1024 * 1024
VMEM_LIMIT = V7X_VMEM_BYTES - 8 * 1024 * 1024

NT_DIMS = (((1,), (1,)), ((), ()))
SIDE_ROWS = 128
GELU_C = float(np.float32(np.sqrt(2.0 / np.pi)))


def _params(*sem):
    return pltpu.CompilerParams(dimension_semantics=sem, vmem_limit_bytes=VMEM_LIMIT)


def _mm_body(*refs, act, alpha, has_bias, has_res, has_side):
    x_ref, w_ref = refs[0], refs[1]
    k = 2
    b_ref = r_ref = s_ref = None
    if has_bias:
        b_ref = refs[k]
        k += 1
    if has_res:
        r_ref = refs[k]
        k += 1
    if has_side:
        s_ref = refs[k]
        k += 1
    o_ref = refs[k]
    k += 1
    if has_side:
        refs[k][...] = s_ref[...].astype(BF16)
        k += 1
    wb_ref = refs[k]

    @pl.when(pl.program_id(1) == 0)
    def _():
        wb_ref[...] = w_ref[...].astype(BF16)

    acc = jnp.dot(x_ref[...], wb_ref[...], preferred_element_type=F32)
    if has_bias:
        acc = acc + b_ref[...]
    if act == "sigmoid":
        acc = 0.5 * jnp.tanh(0.5 * acc.astype(BF16)) + 0.5
    if has_res:
        acc = alpha * r_ref[...] + acc
    o_ref[...] = acc.astype(o_ref.dtype)


def _matmul(x, w, *, out_dtype, tm, tn, bias=None, res=None, alpha=1.0, act=None, side=None, name):
    m, k = x.shape
    n = w.shape[1]
    tm, tn = min(tm, m), min(tn, n)
    assert m % tm == 0 and n % tn == 0
    steps_i = m // tm
    in_specs = [pl.BlockSpec((tm, k), lambda j, i: (i, 0)),
                pl.BlockSpec((k, tn), lambda j, i: (0, j))]
    args = [x, w]
    if bias is not None:
        in_specs.append(pl.BlockSpec((1, tn), lambda j, i: (0, j)))
        args.append(bias.reshape(1, n))
    if res is not None:
        in_specs.append(pl.BlockSpec((tm, tn), lambda j, i: (i, j)))
        args.append(res)
    out_specs = [pl.BlockSpec((tm, tn), lambda j, i: (i, j))]
    out_shape = [jax.ShapeDtypeStruct((m, n), out_dtype)]
    if side is not None:
        rows, width = side.shape
        side_rows = SIDE_ROWS
        while rows // side_rows > (n // tn) * steps_i:
            side_rows *= 2
        n_blocks = rows // side_rows
        assert rows % side_rows == 0
        spec = pl.BlockSpec((side_rows, width),
                            lambda j, i: (jnp.minimum(j * steps_i + i, n_blocks - 1), 0))
        in_specs.append(spec)
        args.append(side)
        out_specs.append(spec)
        out_shape.append(jax.ShapeDtypeStruct(side.shape, BF16))
    body = functools.partial(_mm_body, act=act, alpha=alpha, has_bias=bias is not None,
                             has_res=res is not None, has_side=side is not None)
    outs = pl.pallas_call(
        body,
        grid=(n // tn, steps_i),
        in_specs=in_specs,
        out_specs=out_specs,
        out_shape=out_shape,
        scratch_shapes=[pltpu.VMEM((k, tn), BF16)],
        compiler_params=_params("arbitrary", "arbitrary"),
        name=name,
    )(*args)
    return outs if side is not None else outs[0]


def _ln_rows(z, g, b):
    mu = jnp.mean(z, axis=-1, keepdims=True)
    zc = z - mu
    var = jnp.mean(zc * zc, axis=-1, keepdims=True)
    return zc * lax.rsqrt(var + LN_EPS) * g + b


def _ln_body(*refs, alpha, has_add, want_t):
    k = 0
    z = refs[k][...]
    k += 1
    if has_add:
        z = alpha * z + refs[k][...]
        k += 1
    g_ref, b_ref = refs[k], refs[k + 1]
    k += 2
    h = _ln_rows(z, g_ref[...], b_ref[...])
    refs[k][...] = h
    if want_t:
        refs[k + 1][...] = h.astype(BF16)
        refs[k + 2][...] = h.T.astype(BF16)


def _layer_norm(z, g, b, *, add=None, alpha=1.0, want_t=False, tm, name):
    t, d = z.shape
    tm = min(tm, t)
    assert t % tm == 0
    row = pl.BlockSpec((tm, d), lambda i: (i, 0))
    vec = pl.BlockSpec((1, d), lambda i: (0, 0))
    in_specs, args = [row], [z]
    if add is not None:
        in_specs.append(row)
        args.append(add)
    in_specs += [vec, vec]
    args += [g.reshape(1, d), b.reshape(1, d)]
    out_specs, out_shape = [row], [jax.ShapeDtypeStruct((t, d), F32)]
    if want_t:
        out_specs += [row, pl.BlockSpec((d, tm), lambda i: (0, i))]
        out_shape += [jax.ShapeDtypeStruct((t, d), BF16), jax.ShapeDtypeStruct((d, t), BF16)]
    return pl.pallas_call(
        functools.partial(_ln_body, alpha=alpha, has_add=add is not None, want_t=want_t),
        grid=(t // tm,),
        in_specs=in_specs,
        out_specs=out_specs,
        out_shape=out_shape,
        compiler_params=_params("arbitrary"),
        name=name,
    )(*args)


def _gmlp_body(gu_ref, gv_ref, g_ref, b_ref, ws_ref, bs_ref, o_ref):
    tm = gu_ref.shape[0]
    u = jax.nn.gelu(gu_ref[...].astype(F32))
    v = _ln_rows(jax.nn.gelu(gv_ref[...].astype(F32)), g_ref[...], b_ref[...]).astype(BF16)
    causal = (lax.broadcasted_iota(jnp.int32, (BLOCK, BLOCK), 0)
              >= lax.broadcasted_iota(jnp.int32, (BLOCK, BLOCK), 1))
    for g in range(GM_GROUPS):
        cols = slice(g * 128, (g + 1) * 128)
        w = jnp.where(causal, ws_ref[g], 0.0).astype(BF16)
        for c in range(tm // BLOCK):
            rows = slice(c * BLOCK, (c + 1) * BLOCK)
            mixed = jnp.dot(w, v[rows, cols], preferred_element_type=F32) + bs_ref[g]
            o_ref[rows, cols] = (u[rows, cols] * mixed).astype(o_ref.dtype)


def _gmlp(proj, ln_g, ln_b, w_s, b_s, *, tm):
    t = proj.shape[0]
    assert t % tm == 0 and tm % BLOCK == 0
    bs_full = jnp.broadcast_to(b_s[:, :, None], (GM_GROUPS, BLOCK, 128))
    return pl.pallas_call(
        _gmlp_body,
        grid=(t // tm,),
        in_specs=[pl.BlockSpec((tm, GM_WIDTH), lambda i: (i, OFF_GU // GM_WIDTH)),
                  pl.BlockSpec((tm, GM_WIDTH), lambda i: (i, OFF_GV // GM_WIDTH)),
                  pl.BlockSpec((1, GM_WIDTH), lambda i: (0, 0)),
                  pl.BlockSpec((1, GM_WIDTH), lambda i: (0, 0)),
                  pl.BlockSpec((GM_GROUPS, BLOCK, BLOCK), lambda i: (0, 0, 0)),
                  pl.BlockSpec((GM_GROUPS, BLOCK, 128), lambda i: (0, 0, 0))],
        out_specs=pl.BlockSpec((tm, GM_WIDTH), lambda i: (i, 0)),
        out_shape=jax.ShapeDtypeStruct((t, GM_WIDTH), BF16),
        compiler_params=_params("arbitrary"),
        name="gmlp",
    )(proj, proj, ln_g.reshape(1, GM_WIDTH), ln_b.reshape(1, GM_WIDTH), w_s, bs_full)


PERM_TILE = BLOCK * max(DILATIONS)


def _rope_table_body(pos_ref, inv_ref, cos_ref, sin_ref):
    half = HEAD_DIM // 2
    ang = pos_ref[...] * inv_ref[...]
    lane = lax.broadcasted_iota(jnp.int32, ang.shape, 1)
    sin = jnp.sin(ang)
    cos_ref[...] = jnp.cos(ang)
    sin_ref[...] = jnp.where(lane < half, -sin, sin)


def _rope_table(positions, *, tm):
    t = positions.size
    assert t % tm == 0
    half = HEAD_DIM // 2
    inv_freq = ROPE_THETA ** (-jnp.arange(half, dtype=F32) / half)
    inv = jnp.concatenate([inv_freq, inv_freq]).reshape(1, HEAD_DIM)
    pos = positions.astype(F32).reshape(t, 1)
    out = jax.ShapeDtypeStruct((t, HEAD_DIM), F32)
    return pl.pallas_call(
        _rope_table_body,
        grid=(t // tm,),
        in_specs=[pl.BlockSpec((tm, 1), lambda i: (i, 0)),
                  pl.BlockSpec((1, HEAD_DIM), lambda i: (0, 0))],
        out_specs=[pl.BlockSpec((tm, HEAD_DIM), lambda i: (i, 0))] * 2,
        out_shape=[out, out],
        compiler_params=_params("arbitrary"),
        name="rope_table",
    )(pos, inv)


def _rope_perm_body(q_ref, k_ref, v_ref, cos_ref, sin_ref, qo_ref, ko_ref, vo_ref, buf_ref, *, d):
    tm = q_ref.shape[0]
    half = HEAD_DIM // 2
    span = BLOCK * d
    for h in range(DIL_HEADS):
        cols = slice(h * HEAD_DIM, (h + 1) * HEAD_DIM)
        for src, dst, rotate in ((q_ref, qo_ref, True), (k_ref, ko_ref, True), (v_ref, vo_ref, False)):
            t = src[:, cols].astype(F32)
            if rotate:
                swapped = jnp.concatenate([t[:, half:], t[:, :half]], axis=1)
                t = t * cos_ref[...] + swapped * sin_ref[...]
            if d == 1:
                dst[:, cols] = t.astype(dst.dtype)
                continue
            buf_ref[...] = t
            for n in range(tm // span):
                for r in range(d):
                    rows = buf_ref[pl.ds(n * span + r, BLOCK, stride=d), :]
                    dst[n * span + r * BLOCK:n * span + (r + 1) * BLOCK, cols] = rows.astype(dst.dtype)


def _rope_perm(proj, cos, sin, group, d):
    t = proj.shape[0]
    tm, gw = PERM_TILE, DIL_GROUP_WIDTH
    assert t % tm == 0 and tm % (BLOCK * d) == 0
    col = lambda off: pl.BlockSpec((tm, gw), lambda i: (i, off // gw + group))
    tab = pl.BlockSpec((tm, HEAD_DIM), lambda i: (i, 0))
    out = jax.ShapeDtypeStruct((t, gw), BF16)
    return pl.pallas_call(
        functools.partial(_rope_perm_body, d=d),
        grid=(t // tm,),
        in_specs=[col(OFF_DQ), col(OFF_DK), col(OFF_DV), tab, tab],
        out_specs=[pl.BlockSpec((tm, gw), lambda i: (i, 0))] * 3,
        out_shape=[out, out, out],
        scratch_shapes=[pltpu.VMEM((tm, HEAD_DIM), F32)],
        compiler_params=_params("arbitrary"),
        name=f"rope_perm_d{d}",
    )(proj, proj, proj, cos, sin)


def _dil_body(q_ref, kc_ref, vc_ref, kp_ref, vp_ref, o_ref, lse_ref, *, d, nblk):
    step = pl.program_id(1)
    qi = lax.broadcasted_iota(jnp.int32, (BLOCK, BLOCK), 0)
    ki = lax.broadcasted_iota(jnp.int32, (BLOCK, BLOCK), 1)
    mask_cur = ki <= qi
    scale = HEAD_DIM ** -0.5
    for i in range(nblk):
        rows = slice(i * BLOCK, (i + 1) * BLOCK)
        if d >= nblk:
            kpr, vpr, prow, has_prev = kp_ref, vp_ref, rows, step >= d // nblk
        elif i == 0:
            kpr, vpr, prow, has_prev = kp_ref, vp_ref, slice(0, BLOCK), step > 0
        else:
            kpr, vpr, prow, has_prev = kc_ref, vc_ref, slice((i - 1) * BLOCK, i * BLOCK), None
        mask_prev = ki >= (qi if has_prev is None else qi + jnp.where(has_prev, 0, BLOCK))
        mask = jnp.concatenate([mask_prev, mask_cur], axis=1)
        for h in range(DIL_HEADS):
            cols = slice(h * HEAD_DIM, (h + 1) * HEAD_DIM)
            keys = jnp.concatenate([kpr[prow, cols], kc_ref[rows, cols]], axis=0)
            vals = jnp.concatenate([vpr[prow, cols], vc_ref[rows, cols]], axis=0)
            s = lax.dot_general(q_ref[rows, cols], keys, NT_DIMS, preferred_element_type=F32)
            s = jnp.where(mask, s * scale, NEG_INF)
            m = jnp.max(s, axis=-1, keepdims=True)
            p = jnp.exp(s - m).astype(BF16)
            v_aug = jnp.concatenate([vals, jnp.ones_like(vals)], axis=1)
            o_aug = jnp.dot(p, v_aug, preferred_element_type=F32)
            den = o_aug[:, HEAD_DIM:]
            o_ref[rows, cols] = o_aug[:, :HEAD_DIM] / den
            lse_ref[rows, cols] = m + jnp.log(den)


def _dilated_attention(q, k, v, d, batch, seq, *, nblk):
    gw = DIL_GROUP_WIDTH
    rows = nblk * BLOCK
    assert seq % rows == 0 and (d == 1 or d % nblk == 0)
    steps = seq // rows
    cur = pl.BlockSpec((rows, gw), lambda b, s: (b * steps + s, 0))
    if d >= nblk:
        back = d // nblk
        prev = pl.BlockSpec((rows, gw), lambda b, s: (b * steps + jnp.maximum(s - back, 0), 0))
    else:
        prev = pl.BlockSpec((BLOCK, gw),
                            lambda b, s: (b * steps * nblk + jnp.maximum(s * nblk - 1, 0), 0))
    out = jax.ShapeDtypeStruct((batch * seq, gw), F32)
    return pl.pallas_call(
        functools.partial(_dil_body, d=d, nblk=nblk),
        grid=(batch, steps),
        in_specs=[cur, cur, cur, prev, prev],
        out_specs=[cur, cur],
        out_shape=[out, out],
        compiler_params=_params("arbitrary", "arbitrary"),
        name=f"dil_attn_d{d}",
    )(q, k, v, k, v)


def _dil_merge_body(*refs):
    n = len(DILATIONS)
    o_refs, l_refs, out_ref, bufs = refs[:n], refs[n:2 * n], refs[2 * n], refs[2 * n + 1:]
    tm = out_ref.shape[0]
    outs, lses = [], []
    k = 0
    for d, o_ref, l_ref in zip(DILATIONS, o_refs, l_refs):
        if d == 1:
            outs.append(o_ref[...])
            lses.append(l_ref[...])
            continue
        span = BLOCK * d
        for src, buf in ((o_ref, bufs[k]), (l_ref, bufs[k + 1])):
            for s in range(tm // span):
                for r in range(d):
                    buf[pl.ds(s * span + r, BLOCK, stride=d), :] = (
                        src[s * span + r * BLOCK:s * span + (r + 1) * BLOCK, :])
        outs.append(bufs[k][...])
        lses.append(bufs[k + 1][...])
        k += 2
    m = functools.reduce(jnp.maximum, lses)
    es = [jnp.exp(l - m) for l in lses]
    num = sum(e * o for e, o in zip(es, outs))
    out_ref[...] = (num / sum(es)).astype(out_ref.dtype)


def _dil_merge(outs, lses):
    t, w = outs[0].shape
    tm = PERM_TILE
    assert t % tm == 0
    spec = pl.BlockSpec((tm, HEAD_DIM), lambda i, h: (i, h))
    n_buf = 2 * sum(1 for d in DILATIONS if d != 1)
    return pl.pallas_call(
        _dil_merge_body,
        grid=(t // tm, w // HEAD_DIM),
        in_specs=[spec] * (2 * len(DILATIONS)),
        out_specs=spec,
        out_shape=jax.ShapeDtypeStruct((t, w), BF16),
        scratch_shapes=[pltpu.VMEM((tm, HEAD_DIM), F32)] * n_buf,
        compiler_params=_params("arbitrary", "arbitrary"),
        name="dil_merge",
    )(*outs, *lses)


def _mem_body(q0, q1, q2, q3, kv_ref, o_ref):
    scale = MEM_HEAD_DIM ** -0.5
    for h, q_ref in enumerate((q0, q1, q2, q3)):
        cols = slice(h * MEM_HEAD_DIM, (h + 1) * MEM_HEAD_DIM)
        k = kv_ref[:, cols]
        v = kv_ref[:, MEM_WIDTH + h * MEM_HEAD_DIM:MEM_WIDTH + (h + 1) * MEM_HEAD_DIM]
        s = lax.dot_general(q_ref[...], k, NT_DIMS, preferred_element_type=F32) * scale
        m = jnp.max(s, axis=-1, keepdims=True)
        p = jnp.exp(s - m)
        den = jnp.sum(p, axis=-1, keepdims=True)
        o = jnp.dot(p.astype(BF16), v, preferred_element_type=F32) / den
        o_ref[:, cols] = o.astype(o_ref.dtype)


def _mem_attention(proj, mkv, batch, seq, *, tm):
    mem_len = mkv.shape[0] // batch
    assert seq % tm == 0
    steps = seq // tm
    q_specs = [pl.BlockSpec((tm, MEM_HEAD_DIM),
                            lambda b, i, h=h: (b * steps + i, OFF_MQ // MEM_HEAD_DIM + h))
               for h in range(MEM_HEADS)]
    return pl.pallas_call(
        _mem_body,
        grid=(batch, steps),
        in_specs=q_specs + [pl.BlockSpec((mem_len, 2 * MEM_WIDTH), lambda b, i: (b, 0))],
        out_specs=pl.BlockSpec((tm, MEM_WIDTH), lambda b, i: (b * steps + i, 0)),
        out_shape=jax.ShapeDtypeStruct((batch * seq, MEM_WIDTH), BF16),
        compiler_params=_params("arbitrary", "arbitrary"),
        name="mem_attn",
    )(proj, proj, proj, proj, mkv)


def _merge_body(a_ref, b_ref, m_ref, wa_ref, wb_ref, wm_ref, g0_ref, g1_ref, g2_ref, o_ref,
                wa_bf, wb_bf, wm_bf):
    @pl.when(pl.program_id(1) == 0)
    def _():
        wa_bf[...] = wa_ref[...].astype(BF16)
        wb_bf[...] = wb_ref[...].astype(BF16)
        wm_bf[...] = wm_ref[...].astype(BF16)

    ya = jnp.dot(a_ref[...], wa_bf[...], preferred_element_type=F32)
    yb = jnp.dot(b_ref[...], wb_bf[...], preferred_element_type=F32)
    ym = jnp.dot(m_ref[...], wm_bf[...], preferred_element_type=F32)
    o_ref[...] = (g0_ref[...] * ya.astype(BF16) + g1_ref[...] * yb.astype(BF16)
                  + g2_ref[...] * ym.astype(BF16))


def _merge(a_out, b_out, m_out, w_a, w_b, w_m, gates, *, tm, tn):
    t = a_out.shape[0]
    d = w_a.shape[1]
    assert t % tm == 0 and d % tn == 0
    nj = d // tn
    rows = lambda w: pl.BlockSpec((tm, w), lambda j, i: (i, 0))
    wcol = lambda k: pl.BlockSpec((k, tn), lambda j, i: (0, j))
    gate = lambda br: pl.BlockSpec((tm, tn), lambda j, i, br=br: (i, br * nj + j))
    return pl.pallas_call(
        _merge_body,
        grid=(nj, t // tm),
        in_specs=[rows(GM_WIDTH), rows(DIL_GROUP_WIDTH), rows(MEM_WIDTH),
                  wcol(GM_WIDTH), wcol(DIL_GROUP_WIDTH), wcol(MEM_WIDTH),
                  gate(0), gate(1), gate(2)],
        out_specs=pl.BlockSpec((tm, tn), lambda j, i: (i, j)),
        out_shape=jax.ShapeDtypeStruct((t, d), BF16),
        scratch_shapes=[pltpu.VMEM((GM_WIDTH, tn), BF16), pltpu.VMEM((DIL_GROUP_WIDTH, tn), BF16),
                        pltpu.VMEM((MEM_WIDTH, tn), BF16)],
        compiler_params=_params("arbitrary", "arbitrary"),
        name="branch_merge",
    )(a_out, b_out, m_out, w_a, w_b, w_m, gates, gates, gates)


def _sorting_network(n):
    pairs = []
    p = 1
    while p < n:
        k = p
        while k >= 1:
            for j in range(k % p, n - k, 2 * k):
                for i in range(min(k, n - j - k)):
                    if (i + j) // (2 * p) == (i + j + k) // (2 * p):
                        pairs.append((i + j, i + j + k))
            k //= 2
        p *= 2
    return pairs


SUBLANES = 8


def _top_values(x, k):
    n_groups = x.shape[0] // SUBLANES
    assert n_groups >= k
    g = [x[SUBLANES * j:SUBLANES * (j + 1), :] for j in range(n_groups)]
    for i, j in _sorting_network(n_groups):
        g[i], g[j] = jnp.maximum(g[i], g[j]), jnp.minimum(g[i], g[j])
    sub = lax.broadcasted_iota(jnp.int32, g[0].shape, 0)
    vals = []
    for r in range(k):
        m = jnp.max(g[0], axis=0, keepdims=True)
        vals.append(m)
        first = jnp.min(jnp.where(g[0] == m, sub, SUBLANES), axis=0, keepdims=True)
        hit = sub == first
        for j in range(k - 1 - r):
            g[j] = jnp.where(hit, g[j + 1], g[j])
    return jnp.concatenate(vals, axis=0)


def _staircase(a1, a2, combine, fill):
    r8 = lax.broadcasted_iota(jnp.int32, (8, a1.shape[1]), 0)
    r16 = lax.broadcasted_iota(jnp.int32, a1.shape, 0)
    lo8 = lambda x: x[0:8]
    return jnp.concatenate([
        combine(a1[0:1], a2),
        combine(a1[1:2], lo8(a2)),
        jnp.where(r8 < 5, combine(a1[2:3], lo8(a2)), fill),
        jnp.where(r8 < 4, combine(a1[3:4], lo8(a2)), fill),
        jnp.where(r16 >= 4, combine(a1, a2[0:1]), fill),
        jnp.where(r8 >= 4, combine(lo8(a1), a2[1:2]), fill),
        jnp.where(r8 == 4, combine(lo8(a1), a2[2:3]), fill),
    ], axis=0)


def _peer_route_body(q_ref, keys_ref, g_ref, cnt_ref, e1_ref, rank_ref, e2_ref):
    tm = q_ref.shape[0]
    k = PEER_TOPK
    for h in range(PEER_HEADS):
        sts, tops = [], []
        for p in range(2):
            hp = 2 * h + p
            qs = q_ref[:, hp * PEER_HALF:(hp + 1) * PEER_HALF]
            st = lax.dot_general(keys_ref[hp], qs, NT_DIMS, preferred_element_type=F32,
                                 precision=lax.Precision.HIGHEST)
            sts.append(st)
            tops.append(_top_values(st, k))
        a1, a2 = tops
        m1, m2 = a1[0:1], a2[0:1]
        cand = _staircase(a1, a2, jnp.add, -jnp.inf)
        prob = _staircase(jnp.exp(a1 - m1), jnp.exp(a2 - m2), jnp.multiply, 0.0)
        x = cand
        left = jnp.full((1, tm), float(k), F32)
        tau = jnp.full((1, tm), jnp.inf, F32)
        for _ in range(k):
            m = jnp.max(x, axis=0, keepdims=True)
            eq = x == m
            tau = jnp.where(left > 0, m, tau)
            left = left - jnp.sum(jnp.where(eq, 1.0, 0.0), axis=0, keepdims=True)
            x = jnp.where(eq, -jnp.inf, x)
        z = jnp.sum(jnp.where(cand >= tau, prob, 0.0), axis=0, keepdims=True)
        cnt = jnp.zeros((PEER_KEYS, tm), F32)
        rank = jnp.zeros((PEER_KEYS, tm), F32)
        for b in range(k):
            cnt = cnt + jnp.where(sts[0] + a2[b:b + 1] >= tau, 1.0, 0.0)
            rank = rank + jnp.where(a2[b:b + 1] > sts[1], 1.0, 0.0)
        cnt_ref[h] = cnt
        e1_ref[h] = jnp.exp(sts[0] - m1) / z
        rank_ref[h] = rank.astype(BF16)
        e2_ref[h] = jnp.exp(sts[1] - m2).astype(BF16)

    def tile16(row_f32):
        packed = jnp.broadcast_to(row_f32, (16, tm)).astype(BF16)
        return jnp.concatenate([packed] * (PEER_KEYS // 16), axis=0)

    def build(i1, carry):
        gate = None
        for h in range(PEER_HEADS):
            cnt = tile16(cnt_ref[h, pl.ds(i1, 1), :])
            e1 = tile16(e1_ref[h, pl.ds(i1, 1), :])
            sel = jnp.where(rank_ref[h] < cnt, e1 * e2_ref[h], jnp.zeros((), BF16))
            gate = sel if gate is None else gate + sel
        g_ref[pl.ds(pl.multiple_of(i1 * PEER_KEYS, PEER_KEYS), PEER_KEYS), :] = gate
        return carry

    lax.fori_loop(0, PEER_KEYS, build, 0)


def _peer_route(q, sub_keys, *, tm):
    t = q.shape[0]
    assert t % tm == 0
    hp = 2 * PEER_HEADS
    keys = sub_keys.reshape(hp, PEER_KEYS, PEER_HALF)
    n_exp = PEER_KEYS * PEER_KEYS
    per_head = lambda dt: pltpu.VMEM((PEER_HEADS, PEER_KEYS, tm), dt)
    return pl.pallas_call(
        _peer_route_body,
        grid=(t // tm,),
        in_specs=[pl.BlockSpec((tm, hp * PEER_HALF), lambda i: (i, 0)),
                  pl.BlockSpec((hp, PEER_KEYS, PEER_HALF), lambda i: (0, 0, 0))],
        out_specs=pl.BlockSpec((n_exp, tm), lambda i: (0, i)),
        out_shape=jax.ShapeDtypeStruct((n_exp, t), BF16),
        scratch_shapes=[per_head(F32), per_head(F32), per_head(BF16), per_head(BF16)],
        compiler_params=_params("arbitrary"),
        name="peer_route",
    )(q, keys)


def _peer_dense_body(xt_ref, u_ref, v_ref, g_ref, o_ref):
    @pl.when(pl.program_id(1) == 0)
    def _():
        o_ref[...] = jnp.zeros_like(o_ref)

    act_t = jnp.dot(u_ref[...], xt_ref[...], preferred_element_type=F32)
    inner = GELU_C * (act_t + 0.044715 * (act_t ** 3))
    cdf = 0.5 * (1.0 + jnp.tanh(inner.astype(BF16)))
    w_t = act_t.astype(BF16) * cdf * g_ref[...]
    o_ref[...] += jnp.dot(w_t.T, v_ref[...], preferred_element_type=F32)


def _peer_dense(h_t, u, v, gate_t, *, tm, te):
    d, t = h_t.shape
    n_exp = u.shape[0]
    assert t % tm == 0 and n_exp % te == 0
    once = dict(pipeline_mode=pl.Buffered(1))
    return pl.pallas_call(
        _peer_dense_body,
        grid=(t // tm, n_exp // te),
        in_specs=[pl.BlockSpec((d, tm), lambda i, e: (0, i), **once),
                  pl.BlockSpec((te, d), lambda i, e: (e, 0)),
                  pl.BlockSpec((te, d), lambda i, e: (e, 0)),
                  pl.BlockSpec((te, tm), lambda i, e: (e, i))],
        out_specs=pl.BlockSpec((tm, d), lambda i, e: (i, 0), **once),
        out_shape=jax.ShapeDtypeStruct((t, d), F32),
        compiler_params=_params("arbitrary", "arbitrary"),
        name="peer_dense",
    )(h_t, u, v, gate_t)


MM_TM, MM_TN = 1024, 512
ROW_TM = 256
DIL_BLOCKS_PER_STEP = 4
PEER_TM, PEER_TE = 1024, 512
ROUTE_TM = 512


def kernel(x, mem, positions, w_in, gm_ln_g, gm_ln_b, gm_w_s, gm_b_s, w_mem_kv, w_gate, b_gate,
           w_br_gmlp, w_br_dil, w_br_mem, w_out, ln_mix_g, ln_mix_b, peer_w_query, peer_sub_keys,
           peer_u, peer_v, ln_ffn_g, ln_ffn_b):
    batch, seq, d = x.shape
    t = batch * seq
    assert w_in.shape[0] == DEPTH and w_in.shape[2] == IN_WIDTH
    x2 = x.reshape(t, d)
    x_bf = x2.astype(BF16)
    mem_bf = mem.reshape(-1, d).astype(BF16)

    n_exp = PEER_KEYS * PEER_KEYS
    proj, v_bf = _matmul(x_bf, w_in.reshape(d, IN_WIDTH), out_dtype=BF16, tm=MM_TM, tn=MM_TN,
                         side=peer_v.reshape(n_exp, d), name="in_proj")
    gates, u_bf = _matmul(x_bf, w_gate.reshape(d, 3 * d), bias=b_gate.reshape(3 * d),
                          act="sigmoid", out_dtype=BF16, tm=MM_TM, tn=MM_TN,
                          side=peer_u.reshape(n_exp, d), name="gate_proj")
    mkv = _matmul(mem_bf, w_mem_kv.reshape(d, 2 * MEM_WIDTH), out_dtype=BF16, tm=MM_TM, tn=MM_TN,
                  name="mem_kv_proj")

    a_out = _gmlp(proj, gm_ln_g.reshape(-1), gm_ln_b.reshape(-1), gm_w_s.reshape(GM_GROUPS, BLOCK, BLOCK),
                  gm_b_s.reshape(GM_GROUPS, BLOCK), tm=512)

    cos, sin = _rope_table(positions, tm=1024)
    outs, lses = [], []
    for g, dil in enumerate(DILATIONS):
        qg, kg, vg = _rope_perm(proj, cos, sin, g, dil)
        o, lse = _dilated_attention(qg, kg, vg, dil, batch, seq, nblk=DIL_BLOCKS_PER_STEP)
        outs.append(o)
        lses.append(lse)
    b_out = _dil_merge(outs, lses)

    m_out = _mem_attention(proj, mkv, batch, seq, tm=512)

    merged = _merge(a_out, b_out, m_out, w_br_gmlp.reshape(GM_WIDTH, d),
                    w_br_dil.reshape(DIL_GROUP_WIDTH, d), w_br_mem.reshape(MEM_WIDTH, d), gates,
                    tm=MM_TM, tn=MM_TN)
    z1 = _matmul(merged, w_out.reshape(d, d), res=x2, alpha=ALPHA, out_dtype=F32, tm=MM_TM,
                 tn=MM_TN, name="out_proj")
    h1, h1_bf, h1_t = _layer_norm(z1, ln_mix_g.reshape(-1), ln_mix_b.reshape(-1), want_t=True,
                                  tm=ROW_TM, name="ln_mix")

    hq = PEER_HEADS * 2 * PEER_HALF
    q = _matmul(h1_bf, peer_w_query.reshape(d, hq), out_dtype=F32, tm=MM_TM, tn=MM_TN,
                name="peer_query")
    gate_t = _peer_route(q, peer_sub_keys, tm=ROUTE_TM)
    y = _peer_dense(h1_t, u_bf, v_bf, gate_t, tm=PEER_TM, te=PEER_TE)
    (out,) = _layer_norm(h1, ln_ffn_g.reshape(-1), ln_ffn_b.reshape(-1), add=y, alpha=ALPHA,
                         tm=ROW_TM, name="ln_ffn")
    return out.reshape(batch, seq, d)
```

```python
import functools

import numpy as np
import jax
import jax.numpy as jnp
from jax import lax
from jax.experimental import pallas as pl
from jax.experimental.pallas import tpu as pltpu

F32 = jnp.float32
BF16 = jnp.bfloat16

HEAD_DIM = 128
BLOCK = 128
GM_GROUPS = 12
GM_WIDTH = GM_GROUPS * 128
DILATIONS = (1, 4, 16)
DIL_HEADS = 4
DIL_GROUP_WIDTH = DIL_HEADS * HEAD_DIM
DIL_WIDTH = len(DILATIONS) * DIL_GROUP_WIDTH
MEM_HEADS = 4
MEM_HEAD_DIM = 256
MEM_WIDTH = MEM_HEADS * MEM_HEAD_DIM
PEER_HEADS = 8
PEER_KEYS = 128
PEER_TOPK = 16
PEER_HALF = 128
ROPE_THETA = 10000.0
LN_EPS = 1e-5
DEPTH = 1
ALPHA = (2 * DEPTH) ** 0.25
NEG_INF = -1e30

OFF_GU, OFF_GV = 0, GM_WIDTH
OFF_DQ = 2 * GM_WIDTH
OFF_DK = OFF_DQ + DIL_WIDTH
OFF_DV = OFF_DK + DIL_WIDTH
OFF_MQ = OFF_DV + DIL_WIDTH
IN_WIDTH = OFF_MQ + MEM_WIDTH

V7X_VMEM_BYTES = 64 * 1024 * 1024
VMEM_LIMIT = V7X_VMEM_BYTES - 8 * 1024 * 1024

NT_DIMS = (((1,), (1,)), ((), ()))
SIDE_ROWS = 128
GELU_C = float(np.float32(np.sqrt(2.0 / np.pi)))


def _params(*sem):
    return pltpu.CompilerParams(dimension_semantics=sem, vmem_limit_bytes=VMEM_LIMIT)


def _mm_body(*refs, act, alpha, has_bias, has_res, has_side):
    x_ref, w_ref = refs[0], refs[1]
    k = 2
    b_ref = r_ref = s_ref = None
    if has_bias:
        b_ref = refs[k]
        k += 1
    if has_res:
        r_ref = refs[k]
        k += 1
    if has_side:
        s_ref = refs[k]
        k += 1
    o_ref = refs[k]
    k += 1
    if has_side:
        refs[k][...] = s_ref[...].astype(BF16)
        k += 1
    wb_ref = refs[k]

    @pl.when(pl.program_id(1) == 0)
    def _():
        wb_ref[...] = w_ref[...].astype(BF16)

    acc = jnp.dot(x_ref[...], wb_ref[...], preferred_element_type=F32)
    if has_bias:
        acc = acc + b_ref[...]
    if act == "sigmoid":
        acc = 0.5 * jnp.tanh(0.5 * acc.astype(BF16)) + 0.5
    if has_res:
        acc = alpha * r_ref[...] + acc
    o_ref[...] = acc.astype(o_ref.dtype)


def _matmul(x, w, *, out_dtype, tm, tn, bias=None, res=None, alpha=1.0, act=None, side=None, name):
    m, k = x.shape
    n = w.shape[1]
    tm, tn = min(tm, m), min(tn, n)
    assert m % tm == 0 and n % tn == 0
    steps_i = m // tm
    in_specs = [pl.BlockSpec((tm, k), lambda j, i: (i, 0)),
                pl.BlockSpec((k, tn), lambda j, i: (0, j))]
    args = [x, w]
    if bias is not None:
        in_specs.append(pl.BlockSpec((1, tn), lambda j, i: (0, j)))
        args.append(bias.reshape(1, n))
    if res is not None:
        in_specs.append(pl.BlockSpec((tm, tn), lambda j, i: (i, j)))
        args.append(res)
    out_specs = [pl.BlockSpec((tm, tn), lambda j, i: (i, j))]
    out_shape = [jax.ShapeDtypeStruct((m, n), out_dtype)]
    if side is not None:
        rows, width = side.shape
        side_rows = SIDE_ROWS
        while rows // side_rows > (n // tn) * steps_i:
            side_rows *= 2
        n_blocks = rows // side_rows
        assert rows % side_rows == 0
        spec = pl.BlockSpec((side_rows, width),
                            lambda j, i: (jnp.minimum(j * steps_i + i, n_blocks - 1), 0))
        in_specs.append(spec)
        args.append(side)
        out_specs.append(spec)
        out_shape.append(jax.ShapeDtypeStruct(side.shape, BF16))
    body = functools.partial(_mm_body, act=act, alpha=alpha, has_bias=bias is not None,
                             has_res=res is not None, has_side=side is not None)
    outs = pl.pallas_call(
        body,
        grid=(n // tn, steps_i),
        in_specs=in_specs,
        out_specs=out_specs,
        out_shape=out_shape,
        scratch_shapes=[pltpu.VMEM((k, tn), BF16)],
        compiler_params=_params("arbitrary", "arbitrary"),
        name=name,
    )(*args)
    return outs if side is not None else outs[0]


def _ln_rows(z, g, b):
    mu = jnp.mean(z, axis=-1, keepdims=True)
    zc = z - mu
    var = jnp.mean(zc * zc, axis=-1, keepdims=True)
    return zc * lax.rsqrt(var + LN_EPS) * g + b


def _ln_body(*refs, alpha, has_add, want_t):
    k = 0
    z = refs[k][...]
    k += 1
    if has_add:
        z = alpha * z + refs[k][...]
        k += 1
    g_ref, b_ref = refs[k], refs[k + 1]
    k += 2
    h = _ln_rows(z, g_ref[...], b_ref[...])
    refs[k][...] = h
    if want_t:
        refs[k + 1][...] = h.astype(BF16)
        refs[k + 2][...] = h.T.astype(BF16)


def _layer_norm(z, g, b, *, add=None, alpha=1.0, want_t=False, tm, name):
    t, d = z.shape
    tm = min(tm, t)
    assert t % tm == 0
    row = pl.BlockSpec((tm, d), lambda i: (i, 0))
    vec = pl.BlockSpec((1, d), lambda i: (0, 0))
    in_specs, args = [row], [z]
    if add is not None:
        in_specs.append(row)
        args.append(add)
    in_specs += [vec, vec]
    args += [g.reshape(1, d), b.reshape(1, d)]
    out_specs, out_shape = [row], [jax.ShapeDtypeStruct((t, d), F32)]
    if want_t:
        out_specs += [row, pl.BlockSpec((d, tm), lambda i: (0, i))]
        out_shape += [jax.ShapeDtypeStruct((t, d), BF16), jax.ShapeDtypeStruct((d, t), BF16)]
    return pl.pallas_call(
        functools.partial(_ln_body, alpha=alpha, has_add=add is not None, want_t=want_t),
        grid=(t // tm,),
        in_specs=in_specs,
        out_specs=out_specs,
        out_shape=out_shape,
        compiler_params=_params("arbitrary"),
        name=name,
    )(*args)


def _gmlp_body(gu_ref, gv_ref, g_ref, b_ref, ws_ref, bs_ref, o_ref):
    tm = gu_ref.shape[0]
    u = jax.nn.gelu(gu_ref[...].astype(F32))
    v = _ln_rows(jax.nn.gelu(gv_ref[...].astype(F32)), g_ref[...], b_ref[...]).astype(BF16)
    causal = (lax.broadcasted_iota(jnp.int32, (BLOCK, BLOCK), 0)
              >= lax.broadcasted_iota(jnp.int32, (BLOCK, BLOCK), 1))
    for g in range(GM_GROUPS):
        cols = slice(g * 128, (g + 1) * 128)
        w = jnp.where(causal, ws_ref[g], 0.0).astype(BF16)
        for c in range(tm // BLOCK):
            rows = slice(c * BLOCK, (c + 1) * BLOCK)
            mixed = jnp.dot(w, v[rows, cols], preferred_element_type=F32) + bs_ref[g]
            o_ref[rows, cols] = (u[rows, cols] * mixed).astype(o_ref.dtype)


def _gmlp(proj, ln_g, ln_b, w_s, b_s, *, tm):
    t = proj.shape[0]
    assert t % tm == 0 and tm % BLOCK == 0
    bs_full = jnp.broadcast_to(b_s[:, :, None], (GM_GROUPS, BLOCK, 128))
    return pl.pallas_call(
        _gmlp_body,
        grid=(t // tm,),
        in_specs=[pl.BlockSpec((tm, GM_WIDTH), lambda i: (i, OFF_GU // GM_WIDTH)),
                  pl.BlockSpec((tm, GM_WIDTH), lambda i: (i, OFF_GV // GM_WIDTH)),
                  pl.BlockSpec((1, GM_WIDTH), lambda i: (0, 0)),
                  pl.BlockSpec((1, GM_WIDTH), lambda i: (0, 0)),
                  pl.BlockSpec((GM_GROUPS, BLOCK, BLOCK), lambda i: (0, 0, 0)),
                  pl.BlockSpec((GM_GROUPS, BLOCK, 128), lambda i: (0, 0, 0))],
        out_specs=pl.BlockSpec((tm, GM_WIDTH), lambda i: (i, 0)),
        out_shape=jax.ShapeDtypeStruct((t, GM_WIDTH), BF16),
        compiler_params=_params("arbitrary"),
        name="gmlp",
    )(proj, proj, ln_g.reshape(1, GM_WIDTH), ln_b.reshape(1, GM_WIDTH), w_s, bs_full)


PERM_TILE = BLOCK * max(DILATIONS)


def _rope_table_body(pos_ref, inv_ref, cos_ref, sin_ref):
    half = HEAD_DIM // 2
    ang = pos_ref[...] * inv_ref[...]
    lane = lax.broadcasted_iota(jnp.int32, ang.shape, 1)
    sin = jnp.sin(ang)
    cos_ref[...] = jnp.cos(ang)
    sin_ref[...] = jnp.where(lane < half, -sin, sin)


def _rope_table(positions, *, tm):
    t = positions.size
    assert t % tm == 0
    half = HEAD_DIM // 2
    inv_freq = ROPE_THETA ** (-jnp.arange(half, dtype=F32) / half)
    inv = jnp.concatenate([inv_freq, inv_freq]).reshape(1, HEAD_DIM)
    pos = positions.astype(F32).reshape(t, 1)
    out = jax.ShapeDtypeStruct((t, HEAD_DIM), F32)
    return pl.pallas_call(
        _rope_table_body,
        grid=(t // tm,),
        in_specs=[pl.BlockSpec((tm, 1), lambda i: (i, 0)),
                  pl.BlockSpec((1, HEAD_DIM), lambda i: (0, 0))],
        out_specs=[pl.BlockSpec((tm, HEAD_DIM), lambda i: (i, 0))] * 2,
        out_shape=[out, out],
        compiler_params=_params("arbitrary"),
        name="rope_table",
    )(pos, inv)


def _rope_perm_body(q_ref, k_ref, v_ref, cos_ref, sin_ref, qo_ref, ko_ref, vo_ref, buf_ref, *, d):
    tm = q_ref.shape[0]
    half = HEAD_DIM // 2
    span = BLOCK * d
    for h in range(DIL_HEADS):
        cols = slice(h * HEAD_DIM, (h + 1) * HEAD_DIM)
        for src, dst, rotate in ((q_ref, qo_ref, True), (k_ref, ko_ref, True), (v_ref, vo_ref, False)):
            t = src[:, cols].astype(F32)
            if rotate:
                swapped = jnp.concatenate([t[:, half:], t[:, :half]], axis=1)
                t = t * cos_ref[...] + swapped * sin_ref[...]
            if d == 1:
                dst[:, cols] = t.astype(dst.dtype)
                continue
            buf_ref[...] = t
            for n in range(tm // span):
                for r in range(d):
                    rows = buf_ref[pl.ds(n * span + r, BLOCK, stride=d), :]
                    dst[n * span + r * BLOCK:n * span + (r + 1) * BLOCK, cols] = rows.astype(dst.dtype)


def _rope_perm(proj, cos, sin, group, d):
    t = proj.shape[0]
    tm, gw = PERM_TILE, DIL_GROUP_WIDTH
    assert t % tm == 0 and tm % (BLOCK * d) == 0
    col = lambda off: pl.BlockSpec((tm, gw), lambda i: (i, off // gw + group))
    tab = pl.BlockSpec((tm, HEAD_DIM), lambda i: (i, 0))
    out = jax.ShapeDtypeStruct((t, gw), BF16)
    return pl.pallas_call(
        functools.partial(_rope_perm_body, d=d),
        grid=(t // tm,),
        in_specs=[col(OFF_DQ), col(OFF_DK), col(OFF_DV), tab, tab],
        out_specs=[pl.BlockSpec((tm, gw), lambda i: (i, 0))] * 3,
        out_shape=[out, out, out],
        scratch_shapes=[pltpu.VMEM((tm, HEAD_DIM), F32)],
        compiler_params=_params("arbitrary"),
        name=f"rope_perm_d{d}",
    )(proj, proj, proj, cos, sin)


def _dil_body(q_ref, kc_ref, vc_ref, kp_ref, vp_ref, o_ref, lse_ref, *, d, nblk):
    step = pl.program_id(1)
    qi = lax.broadcasted_iota(jnp.int32, (BLOCK, BLOCK), 0)
    ki = lax.broadcasted_iota(jnp.int32, (BLOCK, BLOCK), 1)
    mask_cur = ki <= qi
    scale = HEAD_DIM ** -0.5
    for i in range(nblk):
        rows = slice(i * BLOCK, (i + 1) * BLOCK)
        if d >= nblk:
            kpr, vpr, prow, has_prev = kp_ref, vp_ref, rows, step >= d // nblk
        elif i == 0:
            kpr, vpr, prow, has_prev = kp_ref, vp_ref, slice(0, BLOCK), step > 0
        else:
            kpr, vpr, prow, has_prev = kc_ref, vc_ref, slice((i - 1) * BLOCK, i * BLOCK), None
        mask_prev = ki >= (qi if has_prev is None else qi + jnp.where(has_prev, 0, BLOCK))
        mask = jnp.concatenate([mask_prev, mask_cur], axis=1)
        for h in range(DIL_HEADS):
            cols = slice(h * HEAD_DIM, (h + 1) * HEAD_DIM)
            keys = jnp.concatenate([kpr[prow, cols], kc_ref[rows, cols]], axis=0)
            vals = jnp.concatenate([vpr[prow, cols], vc_ref[rows, cols]], axis=0)
            s = lax.dot_general(q_ref[rows, cols], keys, NT_DIMS, preferred_element_type=F32)
            s = jnp.where(mask, s * scale, NEG_INF)
            m = jnp.max(s, axis=-1, keepdims=True)
            p = jnp.exp(s - m).astype(BF16)
            v_aug = jnp.concatenate([vals, jnp.ones_like(vals)], axis=1)
            o_aug = jnp.dot(p, v_aug, preferred_element_type=F32)
            den = o_aug[:, HEAD_DIM:]
            o_ref[rows, cols] = o_aug[:, :HEAD_DIM] / den
            lse_ref[rows, cols] = m + jnp.log(den)


def _dilated_attention(q, k, v, d, batch, seq, *, nblk):
    gw = DIL_GROUP_WIDTH
    rows = nblk * BLOCK
    assert seq % rows == 0 and (d == 1 or d % nblk == 0)
    steps = seq // rows
    cur = pl.BlockSpec((rows, gw), lambda b, s: (b * steps + s, 0))
    if d >= nblk:
        back = d // nblk
        prev = pl.BlockSpec((rows, gw), lambda b, s: (b * steps + jnp.maximum(s - back, 0), 0))
    else:
        prev = pl.BlockSpec((BLOCK, gw),
                            lambda b, s: (b * steps * nblk + jnp.maximum(s * nblk - 1, 0), 0))
    out = jax.ShapeDtypeStruct((batch * seq, gw), F32)
    return pl.pallas_call(
        functools.partial(_dil_body, d=d, nblk=nblk),
        grid=(batch, steps),
        in_specs=[cur, cur, cur, prev, prev],
        out_specs=[cur, cur],
        out_shape=[out, out],
        compiler_params=_params("arbitrary", "arbitrary"),
        name=f"dil_attn_d{d}",
    )(q, k, v, k, v)


def _dil_merge_body(*refs):
    n = len(DILATIONS)
    o_refs, l_refs, out_ref, bufs = refs[:n], refs[n:2 * n], refs[2 * n], refs[2 * n + 1:]
    tm = out_ref.shape[0]
    outs, lses = [], []
    k = 0
    for d, o_ref, l_ref in zip(DILATIONS, o_refs, l_refs):
        if d == 1:
            outs.append(o_ref[...])
            lses.append(l_ref[...])
            continue
        span = BLOCK * d
        for src, buf in ((o_ref, bufs[k]), (l_ref, bufs[k + 1])):
            for s in range(tm // span):
                for r in range(d):
                    buf[pl.ds(s * span + r, BLOCK, stride=d), :] = (
                        src[s * span + r * BLOCK:s * span + (r + 1) * BLOCK, :])
        outs.append(bufs[k][...])
        lses.append(bufs[k + 1][...])
        k += 2
    m = functools.reduce(jnp.maximum, lses)
    es = [jnp.exp(l - m) for l in lses]
    num = sum(e * o for e, o in zip(es, outs))
    out_ref[...] = (num / sum(es)).astype(out_ref.dtype)


def _dil_merge(outs, lses):
    t, w = outs[0].shape
    tm = PERM_TILE
    assert t % tm == 0
    spec = pl.BlockSpec((tm, HEAD_DIM), lambda i, h: (i, h))
    n_buf = 2 * sum(1 for d in DILATIONS if d != 1)
    return pl.pallas_call(
        _dil_merge_body,
        grid=(t // tm, w // HEAD_DIM),
        in_specs=[spec] * (2 * len(DILATIONS)),
        out_specs=spec,
        out_shape=jax.ShapeDtypeStruct((t, w), BF16),
        scratch_shapes=[pltpu.VMEM((tm, HEAD_DIM), F32)] * n_buf,
        compiler_params=_params("arbitrary", "arbitrary"),
        name="dil_merge",
    )(*outs, *lses)


def _mem_body(q0, q1, q2, q3, kv_ref, o_ref):
    scale = MEM_HEAD_DIM ** -0.5
    for h, q_ref in enumerate((q0, q1, q2, q3)):
        cols = slice(h * MEM_HEAD_DIM, (h + 1) * MEM_HEAD_DIM)
        k = kv_ref[:, cols]
        v = kv_ref[:, MEM_WIDTH + h * MEM_HEAD_DIM:MEM_WIDTH + (h + 1) * MEM_HEAD_DIM]
        s = lax.dot_general(q_ref[...], k, NT_DIMS, preferred_element_type=F32) * scale
        m = jnp.max(s, axis=-1, keepdims=True)
        p = jnp.exp(s - m)
        den = jnp.sum(p, axis=-1, keepdims=True)
        o = jnp.dot(p.astype(BF16), v, preferred_element_type=F32) / den
        o_ref[:, cols] = o.astype(o_ref.dtype)


def _mem_attention(proj, mkv, batch, seq, *, tm):
    mem_len = mkv.shape[0] // batch
    assert seq % tm == 0
    steps = seq // tm
    q_specs = [pl.BlockSpec((tm, MEM_HEAD_DIM),
                            lambda b, i, h=h: (b * steps + i, OFF_MQ // MEM_HEAD_DIM + h))
               for h in range(MEM_HEADS)]
    return pl.pallas_call(
        _mem_body,
        grid=(batch, steps),
        in_specs=q_specs + [pl.BlockSpec((mem_len, 2 * MEM_WIDTH), lambda b, i: (b, 0))],
        out_specs=pl.BlockSpec((tm, MEM_WIDTH), lambda b, i: (b * steps + i, 0)),
        out_shape=jax.ShapeDtypeStruct((batch * seq, MEM_WIDTH), BF16),
        compiler_params=_params("arbitrary", "arbitrary"),
        name="mem_attn",
    )(proj, proj, proj, proj, mkv)


def _merge_body(a_ref, b_ref, m_ref, wa_ref, wb_ref, wm_ref, g0_ref, g1_ref, g2_ref, o_ref,
                wa_bf, wb_bf, wm_bf):
    @pl.when(pl.program_id(1) == 0)
    def _():
        wa_bf[...] = wa_ref[...].astype(BF16)
        wb_bf[...] = wb_ref[...].astype(BF16)
        wm_bf[...] = wm_ref[...].astype(BF16)

    ya = jnp.dot(a_ref[...], wa_bf[...], preferred_element_type=F32)
    yb = jnp.dot(b_ref[...], wb_bf[...], preferred_element_type=F32)
    ym = jnp.dot(m_ref[...], wm_bf[...], preferred_element_type=F32)
    o_ref[...] = (g0_ref[...] * ya.astype(BF16) + g1_ref[...] * yb.astype(BF16)
                  + g2_ref[...] * ym.astype(BF16))


def _merge(a_out, b_out, m_out, w_a, w_b, w_m, gates, *, tm, tn):
    t = a_out.shape[0]
    d = w_a.shape[1]
    assert t % tm == 0 and d % tn == 0
    nj = d // tn
    rows = lambda w: pl.BlockSpec((tm, w), lambda j, i: (i, 0))
    wcol = lambda k: pl.BlockSpec((k, tn), lambda j, i: (0, j))
    gate = lambda br: pl.BlockSpec((tm, tn), lambda j, i, br=br: (i, br * nj + j))
    return pl.pallas_call(
        _merge_body,
        grid=(nj, t // tm),
        in_specs=[rows(GM_WIDTH), rows(DIL_GROUP_WIDTH), rows(MEM_WIDTH),
                  wcol(GM_WIDTH), wcol(DIL_GROUP_WIDTH), wcol(MEM_WIDTH),
                  gate(0), gate(1), gate(2)],
        out_specs=pl.BlockSpec((tm, tn), lambda j, i: (i, j)),
        out_shape=jax.ShapeDtypeStruct((t, d), BF16),
        scratch_shapes=[pltpu.VMEM((GM_WIDTH, tn), BF16), pltpu.VMEM((DIL_GROUP_WIDTH, tn), BF16),
                        pltpu.VMEM((MEM_WIDTH, tn), BF16)],
        compiler_params=_params("arbitrary", "arbitrary"),
        name="branch_merge",
    )(a_out, b_out, m_out, w_a, w_b, w_m, gates, gates, gates)


def _sorting_network(n):
    pairs = []
    p = 1
    while p < n:
        k = p
        while k >= 1:
            for j in range(k % p, n - k, 2 * k):
                for i in range(min(k, n - j - k)):
                    if (i + j) // (2 * p) == (i + j + k) // (2 * p):
                        pairs.append((i + j, i + j + k))
            k //= 2
        p *= 2
    return pairs


SUBLANES = 8


def _top_values(x, k):
    n_groups = x.shape[0] // SUBLANES
    assert n_groups >= k
    g = [x[SUBLANES * j:SUBLANES * (j + 1), :] for j in range(n_groups)]
    for i, j in _sorting_network(n_groups):
        g[i], g[j] = jnp.maximum(g[i], g[j]), jnp.minimum(g[i], g[j])
    sub = lax.broadcasted_iota(jnp.int32, g[0].shape, 0)
    vals = []
    for r in range(k):
        m = jnp.max(g[0], axis=0, keepdims=True)
        vals.append(m)
        first = jnp.min(jnp.where(g[0] == m, sub, SUBLANES), axis=0, keepdims=True)
        hit = sub == first
        for j in range(k - 1 - r):
            g[j] = jnp.where(hit, g[j + 1], g[j])
    return jnp.concatenate(vals, axis=0)


def _staircase(a1, a2, combine, fill):
    r8 = lax.broadcasted_iota(jnp.int32, (8, a1.shape[1]), 0)
    r16 = lax.broadcasted_iota(jnp.int32, a1.shape, 0)
    lo8 = lambda x: x[0:8]
    return jnp.concatenate([
        combine(a1[0:1], a2),
        combine(a1[1:2], lo8(a2)),
        jnp.where(r8 < 5, combine(a1[2:3], lo8(a2)), fill),
        jnp.where(r8 < 4, combine(a1[3:4], lo8(a2)), fill),
        jnp.where(r16 >= 4, combine(a1, a2[0:1]), fill),
        jnp.where(r8 >= 4, combine(lo8(a1), a2[1:2]), fill),
        jnp.where(r8 == 4, combine(lo8(a1), a2[2:3]), fill),
    ], axis=0)


def _peer_route_body(q_ref, keys_ref, g_ref, cnt_ref, e1_ref, rank_ref, e2_ref):
    tm = q_ref.shape[0]
    k = PEER_TOPK
    for h in range(PEER_HEADS):
        sts, tops = [], []
        for p in range(2):
            hp = 2 * h + p
            qs = q_ref[:, hp * PEER_HALF:(hp + 1) * PEER_HALF]
            st = lax.dot_general(keys_ref[hp], qs, NT_DIMS, preferred_element_type=F32,
                                 precision=lax.Precision.HIGHEST)
            sts.append(st)
            tops.append(_top_values(st, k))
        a1, a2 = tops
        m1, m2 = a1[0:1], a2[0:1]
        cand = _staircase(a1, a2, jnp.add, -jnp.inf)
        prob = _staircase(jnp.exp(a1 - m1), jnp.exp(a2 - m2), jnp.multiply, 0.0)
        x = cand
        left = jnp.full((1, tm), float(k), F32)
        tau = jnp.full((1, tm), jnp.inf, F32)
        for _ in range(k):
            m = jnp.max(x, axis=0, keepdims=True)
            eq = x == m
            tau = jnp.where(left > 0, m, tau)
            left = left - jnp.sum(jnp.where(eq, 1.0, 0.0), axis=0, keepdims=True)
            x = jnp.where(eq, -jnp.inf, x)
        z = jnp.sum(jnp.where(cand >= tau, prob, 0.0), axis=0, keepdims=True)
        cnt = jnp.zeros((PEER_KEYS, tm), F32)
        rank = jnp.zeros((PEER_KEYS, tm), F32)
        for b in range(k):
            cnt = cnt + jnp.where(sts[0] + a2[b:b + 1] >= tau, 1.0, 0.0)
            rank = rank + jnp.where(a2[b:b + 1] > sts[1], 1.0, 0.0)
        cnt_ref[h] = cnt
        e1_ref[h] = jnp.exp(sts[0] - m1) / z
        rank_ref[h] = rank.astype(BF16)
        e2_ref[h] = jnp.exp(sts[1] - m2).astype(BF16)

    def tile16(row_f32):
        packed = jnp.broadcast_to(row_f32, (16, tm)).astype(BF16)
        return jnp.concatenate([packed] * (PEER_KEYS // 16), axis=0)

    def build(i1, carry):
        gate = None
        for h in range(PEER_HEADS):
            cnt = tile16(cnt_ref[h, pl.ds(i1, 1), :])
            e1 = tile16(e1_ref[h, pl.ds(i1, 1), :])
            sel = jnp.where(rank_ref[h] < cnt, e1 * e2_ref[h], jnp.zeros((), BF16))
            gate = sel if gate is None else gate + sel
        g_ref[pl.ds(pl.multiple_of(i1 * PEER_KEYS, PEER_KEYS), PEER_KEYS), :] = gate
        return carry

    lax.fori_loop(0, PEER_KEYS, build, 0)


def _peer_route(q, sub_keys, *, tm):
    t = q.shape[0]
    assert t % tm == 0
    hp = 2 * PEER_HEADS
    keys = sub_keys.reshape(hp, PEER_KEYS, PEER_HALF)
    n_exp = PEER_KEYS * PEER_KEYS
    per_head = lambda dt: pltpu.VMEM((PEER_HEADS, PEER_KEYS, tm), dt)
    return pl.pallas_call(
        _peer_route_body,
        grid=(t // tm,),
        in_specs=[pl.BlockSpec((tm, hp * PEER_HALF), lambda i: (i, 0)),
                  pl.BlockSpec((hp, PEER_KEYS, PEER_HALF), lambda i: (0, 0, 0))],
        out_specs=pl.BlockSpec((n_exp, tm), lambda i: (0, i)),
        out_shape=jax.ShapeDtypeStruct((n_exp, t), BF16),
        scratch_shapes=[per_head(F32), per_head(F32), per_head(BF16), per_head(BF16)],
        compiler_params=_params("arbitrary"),
        name="peer_route",
    )(q, keys)


def _peer_dense_body(xt_ref, u_ref, v_ref, g_ref, o_ref):
    @pl.when(pl.program_id(1) == 0)
    def _():
        o_ref[...] = jnp.zeros_like(o_ref)

    act_t = jnp.dot(u_ref[...], xt_ref[...], preferred_element_type=F32)
    inner = GELU_C * (act_t + 0.044715 * (act_t ** 3))
    cdf = 0.5 * (1.0 + jnp.tanh(inner.astype(BF16)))
    w_t = act_t.astype(BF16) * cdf * g_ref[...]
    o_ref[...] += jnp.dot(w_t.T, v_ref[...], preferred_element_type=F32)


def _peer_dense(h_t, u, v, gate_t, *, tm, te):
    d, t = h_t.shape
    n_exp = u.shape[0]
    assert t % tm == 0 and n_exp % te == 0
    once = dict(pipeline_mode=pl.Buffered(1))
    return pl.pallas_call(
        _peer_dense_body,
        grid=(t // tm, n_exp // te),
        in_specs=[pl.BlockSpec((d, tm), lambda i, e: (0, i), **once),
                  pl.BlockSpec((te, d), lambda i, e: (e, 0)),
                  pl.BlockSpec((te, d), lambda i, e: (e, 0)),
                  pl.BlockSpec((te, tm), lambda i, e: (e, i))],
        out_specs=pl.BlockSpec((tm, d), lambda i, e: (i, 0), **once),
        out_shape=jax.ShapeDtypeStruct((t, d), F32),
        compiler_params=_params("arbitrary", "arbitrary"),
        name="peer_dense",
    )(h_t, u, v, gate_t)


MM_TM, MM_TN = 1024, 512
ROW_TM = 256
DIL_BLOCKS_PER_STEP = 4
PEER_TM, PEER_TE = 1024, 512
ROUTE_TM = 256


def kernel(x, mem, positions, w_in, gm_ln_g, gm_ln_b, gm_w_s, gm_b_s, w_mem_kv, w_gate, b_gate,
           w_br_gmlp, w_br_dil, w_br_mem, w_out, ln_mix_g, ln_mix_b, peer_w_query, peer_sub_keys,
           peer_u, peer_v, ln_ffn_g, ln_ffn_b):
    batch, seq, d = x.shape
    t = batch * seq
    assert w_in.shape[0] == DEPTH and w_in.shape[2] == IN_WIDTH
    x2 = x.reshape(t, d)
    x_bf = x2.astype(BF16)
    mem_bf = mem.reshape(-1, d).astype(BF16)

    n_exp = PEER_KEYS * PEER_KEYS
    proj, v_bf = _matmul(x_bf, w_in.reshape(d, IN_WIDTH), out_dtype=BF16, tm=MM_TM, tn=MM_TN,
                         side=peer_v.reshape(n_exp, d), name="in_proj")
    gates, u_bf = _matmul(x_bf, w_gate.reshape(d, 3 * d), bias=b_gate.reshape(3 * d),
                          act="sigmoid", out_dtype=BF16, tm=MM_TM, tn=MM_TN,
                          side=peer_u.reshape(n_exp, d), name="gate_proj")
    mkv = _matmul(mem_bf, w_mem_kv.reshape(d, 2 * MEM_WIDTH), out_dtype=BF16, tm=MM_TM, tn=MM_TN,
                  name="mem_kv_proj")

    a_out = _gmlp(proj, gm_ln_g.reshape(-1), gm_ln_b.reshape(-1), gm_w_s.reshape(GM_GROUPS, BLOCK, BLOCK),
                  gm_b_s.reshape(GM_GROUPS, BLOCK), tm=512)

    cos, sin = _rope_table(positions, tm=1024)
    outs, lses = [], []
    for g, dil in enumerate(DILATIONS):
        qg, kg, vg = _rope_perm(proj, cos, sin, g, dil)
        o, lse = _dilated_attention(qg, kg, vg, dil, batch, seq, nblk=DIL_BLOCKS_PER_STEP)
        outs.append(o)
        lses.append(lse)
    b_out = _dil_merge(outs, lses)

    m_out = _mem_attention(proj, mkv, batch, seq, tm=512)

    merged = _merge(a_out, b_out, m_out, w_br_gmlp.reshape(GM_WIDTH, d),
                    w_br_dil.reshape(DIL_GROUP_WIDTH, d), w_br_mem.reshape(MEM_WIDTH, d), gates,
                    tm=MM_TM, tn=MM_TN)
    z1 = _matmul(merged, w_out.reshape(d, d), res=x2, alpha=ALPHA, out_dtype=F32, tm=MM_TM,
                 tn=MM_TN, name="out_proj")
    h1, h1_bf, h1_t = _layer_norm(z1, ln_mix_g.reshape(-1), ln_mix_b.reshape(-1), want_t=True,
                                  tm=ROW_TM, name="ln_mix")

    hq = PEER_HEADS * 2 * PEER_HALF
    q = _matmul(h1_bf, peer_w_query.reshape(d, hq), out_dtype=F32, tm=MM_TM, tn=MM_TN,
                name="peer_query")
    gate_t = _peer_route(q, peer_sub_keys, tm=ROUTE_TM)
    y = _peer_dense(h1_t, u_bf, v_bf, gate_t, tm=PEER_TM, te=PEER_TE)
    (out,) = _layer_norm(h1, ln_ffn_g.reshape(-1), ln_ffn_b.reshape(-1), add=y, alpha=ALPHA,
                         tm=ROW_TM, name="ln_ffn")
    return out.reshape(batch, seq, d)
```

```python
import functools

import numpy as np
import jax
import jax.numpy as jnp
from jax import lax
from jax.experimental import pallas as pl
from jax.experimental.pallas import tpu as pltpu

F32 = jnp.float32
BF16 = jnp.bfloat16

HEAD_DIM = 128
BLOCK = 128
GM_GROUPS = 12
GM_WIDTH = GM_GROUPS * 128
DILATIONS = (1, 4, 16)
DIL_HEADS = 4
DIL_GROUP_WIDTH = DIL_HEADS * HEAD_DIM
DIL_WIDTH = len(DILATIONS) * DIL_GROUP_WIDTH
MEM_HEADS = 4
MEM_HEAD_DIM = 256
MEM_WIDTH = MEM_HEADS * MEM_HEAD_DIM
PEER_HEADS = 8
PEER_KEYS = 128
PEER_TOPK = 16
PEER_HALF = 128
ROPE_THETA = 10000.0
LN_EPS = 1e-5
DEPTH = 1
ALPHA = (2 * DEPTH) ** 0.25
NEG_INF = -1e30

OFF_GU, OFF_GV = 0, GM_WIDTH
OFF_DQ = 2 * GM_WIDTH
OFF_DK = OFF_DQ + DIL_WIDTH
OFF_DV = OFF_DK + DIL_WIDTH
OFF_MQ = OFF_DV + DIL_WIDTH
IN_WIDTH = OFF_MQ + MEM_WIDTH

V7X_VMEM_BYTES = 64 * 1024 * 1024
VMEM_LIMIT = V7X_VMEM_BYTES - 8 * 1024 * 1024

NT_DIMS = (((1,), (1,)), ((), ()))
SIDE_ROWS = 128
GELU_C = float(np.float32(np.sqrt(2.0 / np.pi)))


def _params(*sem):
    return pltpu.CompilerParams(dimension_semantics=sem, vmem_limit_bytes=VMEM_LIMIT)


def _mm_body(*refs, act, alpha, has_bias, has_res, has_side):
    x_ref, w_ref = refs[0], refs[1]
    k = 2
    b_ref = r_ref = s_ref = None
    if has_bias:
        b_ref = refs[k]
        k += 1
    if has_res:
        r_ref = refs[k]
        k += 1
    if has_side:
        s_ref = refs[k]
        k += 1
    o_ref = refs[k]
    k += 1
    if has_side:
        refs[k][...] = s_ref[...].astype(BF16)
        k += 1
    wb_ref = refs[k]

    @pl.when(pl.program_id(1) == 0)
    def _():
        wb_ref[...] = w_ref[...].astype(BF16)

    acc = jnp.dot(x_ref[...], wb_ref[...], preferred_element_type=F32)
    if has_bias:
        acc = acc + b_ref[...]
    if act == "sigmoid":
        acc = 0.5 * jnp.tanh(0.5 * acc.astype(BF16)) + 0.5
    if has_res:
        acc = alpha * r_ref[...] + acc
    o_ref[...] = acc.astype(o_ref.dtype)


def _matmul(x, w, *, out_dtype, tm, tn, bias=None, res=None, alpha=1.0, act=None, side=None, name):
    m, k = x.shape
    n = w.shape[1]
    tm, tn = min(tm, m), min(tn, n)
    assert m % tm == 0 and n % tn == 0
    steps_i = m // tm
    in_specs = [pl.BlockSpec((tm, k), lambda j, i: (i, 0)),
                pl.BlockSpec((k, tn), lambda j, i: (0, j))]
    args = [x, w]
    if bias is not None:
        in_specs.append(pl.BlockSpec((1, tn), lambda j, i: (0, j)))
        args.append(bias.reshape(1, n))
    if res is not None:
        in_specs.append(pl.BlockSpec((tm, tn), lambda j, i: (i, j)))
        args.append(res)
    out_specs = [pl.BlockSpec((tm, tn), lambda j, i: (i, j))]
    out_shape = [jax.ShapeDtypeStruct((m, n), out_dtype)]
    if side is not None:
        rows, width = side.shape
        side_rows = SIDE_ROWS
        while rows // side_rows > (n // tn) * steps_i:
            side_rows *= 2
        n_blocks = rows // side_rows
        assert rows % side_rows == 0
        spec = pl.BlockSpec((side_rows, width),
                            lambda j, i: (jnp.minimum(j * steps_i + i, n_blocks - 1), 0))
        in_specs.append(spec)
        args.append(side)
        out_specs.append(spec)
        out_shape.append(jax.ShapeDtypeStruct(side.shape, BF16))
    body = functools.partial(_mm_body, act=act, alpha=alpha, has_bias=bias is not None,
                             has_res=res is not None, has_side=side is not None)
    outs = pl.pallas_call(
        body,
        grid=(n // tn, steps_i),
        in_specs=in_specs,
        out_specs=out_specs,
        out_shape=out_shape,
        scratch_shapes=[pltpu.VMEM((k, tn), BF16)],
        compiler_params=_params("arbitrary", "arbitrary"),
        name=name,
    )(*args)
    return outs if side is not None else outs[0]


def _ln_rows(z, g, b):
    mu = jnp.mean(z, axis=-1, keepdims=True)
    zc = z - mu
    var = jnp.mean(zc * zc, axis=-1, keepdims=True)
    return zc * lax.rsqrt(var + LN_EPS) * g + b


def _ln_body(*refs, alpha, has_add, want_f32, want_t):
    k = 0
    z = refs[k][...].astype(F32)
    k += 1
    if has_add:
        z = alpha * z + refs[k][...]
        k += 1
    g_ref, b_ref = refs[k], refs[k + 1]
    k += 2
    h = _ln_rows(z, g_ref[...], b_ref[...])
    if want_f32:
        refs[k][...] = h
        k += 1
    if want_t:
        refs[k][...] = h.astype(BF16)
        refs[k + 1][...] = h.T.astype(BF16)


def _layer_norm(z, g, b, *, add=None, alpha=1.0, want_f32=True, want_t=False, tm, name):
    t, d = z.shape
    tm = min(tm, t)
    assert t % tm == 0
    row = pl.BlockSpec((tm, d), lambda i: (i, 0))
    vec = pl.BlockSpec((1, d), lambda i: (0, 0))
    in_specs, args = [row], [z]
    if add is not None:
        in_specs.append(row)
        args.append(add)
    in_specs += [vec, vec]
    args += [g.reshape(1, d), b.reshape(1, d)]
    out_specs, out_shape = [], []
    if want_f32:
        out_specs += [row]
        out_shape += [jax.ShapeDtypeStruct((t, d), F32)]
    if want_t:
        out_specs += [row, pl.BlockSpec((d, tm), lambda i: (0, i))]
        out_shape += [jax.ShapeDtypeStruct((t, d), BF16), jax.ShapeDtypeStruct((d, t), BF16)]
    return pl.pallas_call(
        functools.partial(_ln_body, alpha=alpha, has_add=add is not None, want_f32=want_f32,
                          want_t=want_t),
        grid=(t // tm,),
        in_specs=in_specs,
        out_specs=out_specs,
        out_shape=out_shape,
        compiler_params=_params("arbitrary"),
        name=name,
    )(*args)


def _gmlp_body(gu_ref, gv_ref, g_ref, b_ref, ws_ref, bs_ref, o_ref):
    tm = gu_ref.shape[0]
    u = jax.nn.gelu(gu_ref[...].astype(F32))
    v = _ln_rows(jax.nn.gelu(gv_ref[...].astype(F32)), g_ref[...], b_ref[...]).astype(BF16)
    causal = (lax.broadcasted_iota(jnp.int32, (BLOCK, BLOCK), 0)
              >= lax.broadcasted_iota(jnp.int32, (BLOCK, BLOCK), 1))
    for g in range(GM_GROUPS):
        cols = slice(g * 128, (g + 1) * 128)
        w = jnp.where(causal, ws_ref[g], 0.0).astype(BF16)
        for c in range(tm // BLOCK):
            rows = slice(c * BLOCK, (c + 1) * BLOCK)
            mixed = jnp.dot(w, v[rows, cols], preferred_element_type=F32) + bs_ref[g]
            o_ref[rows, cols] = (u[rows, cols] * mixed).astype(o_ref.dtype)


def _gmlp(proj, ln_g, ln_b, w_s, b_s, *, tm):
    t = proj.shape[0]
    assert t % tm == 0 and tm % BLOCK == 0
    bs_full = jnp.broadcast_to(b_s[:, :, None], (GM_GROUPS, BLOCK, 128))
    return pl.pallas_call(
        _gmlp_body,
        grid=(t // tm,),
        in_specs=[pl.BlockSpec((tm, GM_WIDTH), lambda i: (i, OFF_GU // GM_WIDTH)),
                  pl.BlockSpec((tm, GM_WIDTH), lambda i: (i, OFF_GV // GM_WIDTH)),
                  pl.BlockSpec((1, GM_WIDTH), lambda i: (0, 0)),
                  pl.BlockSpec((1, GM_WIDTH), lambda i: (0, 0)),
                  pl.BlockSpec((GM_GROUPS, BLOCK, BLOCK), lambda i: (0, 0, 0)),
                  pl.BlockSpec((GM_GROUPS, BLOCK, 128), lambda i: (0, 0, 0))],
        out_specs=pl.BlockSpec((tm, GM_WIDTH), lambda i: (i, 0)),
        out_shape=jax.ShapeDtypeStruct((t, GM_WIDTH), BF16),
        compiler_params=_params("arbitrary"),
        name="gmlp",
    )(proj, proj, ln_g.reshape(1, GM_WIDTH), ln_b.reshape(1, GM_WIDTH), w_s, bs_full)


PERM_TILE = BLOCK * max(DILATIONS)


def _rope_table_body(pos_ref, inv_ref, cos_ref, sin_ref):
    half = HEAD_DIM // 2
    ang = pos_ref[...] * inv_ref[...]
    lane = lax.broadcasted_iota(jnp.int32, ang.shape, 1)
    sin = jnp.sin(ang)
    cos_ref[...] = jnp.cos(ang)
    sin_ref[...] = jnp.where(lane < half, -sin, sin)


def _rope_table(positions, *, tm):
    t = positions.size
    assert t % tm == 0
    half = HEAD_DIM // 2
    inv_freq = ROPE_THETA ** (-jnp.arange(half, dtype=F32) / half)
    inv = jnp.concatenate([inv_freq, inv_freq]).reshape(1, HEAD_DIM)
    pos = positions.astype(F32).reshape(t, 1)
    out = jax.ShapeDtypeStruct((t, HEAD_DIM), F32)
    return pl.pallas_call(
        _rope_table_body,
        grid=(t // tm,),
        in_specs=[pl.BlockSpec((tm, 1), lambda i: (i, 0)),
                  pl.BlockSpec((1, HEAD_DIM), lambda i: (0, 0))],
        out_specs=[pl.BlockSpec((tm, HEAD_DIM), lambda i: (i, 0))] * 2,
        out_shape=[out, out],
        compiler_params=_params("arbitrary"),
        name="rope_table",
    )(pos, inv)


def _rope_perm_body(q_ref, k_ref, v_ref, cos_ref, sin_ref, qo_ref, ko_ref, vo_ref, buf_ref, *, d):
    tm = q_ref.shape[0]
    half = HEAD_DIM // 2
    span = BLOCK * d
    for h in range(DIL_HEADS):
        cols = slice(h * HEAD_DIM, (h + 1) * HEAD_DIM)
        for src, dst, rotate in ((q_ref, qo_ref, True), (k_ref, ko_ref, True), (v_ref, vo_ref, False)):
            t = src[:, cols].astype(F32)
            if rotate:
                swapped = jnp.concatenate([t[:, half:], t[:, :half]], axis=1)
                t = t * cos_ref[...] + swapped * sin_ref[...]
            if d == 1:
                dst[:, cols] = t.astype(dst.dtype)
                continue
            buf_ref[...] = t
            for n in range(tm // span):
                for r in range(d):
                    rows = buf_ref[pl.ds(n * span + r, BLOCK, stride=d), :]
                    dst[n * span + r * BLOCK:n * span + (r + 1) * BLOCK, cols] = rows.astype(dst.dtype)


def _rope_perm(proj, cos, sin, group, d):
    t = proj.shape[0]
    tm, gw = PERM_TILE, DIL_GROUP_WIDTH
    assert t % tm == 0 and tm % (BLOCK * d) == 0
    col = lambda off: pl.BlockSpec((tm, gw), lambda i: (i, off // gw + group))
    tab = pl.BlockSpec((tm, HEAD_DIM), lambda i: (i, 0))
    out = jax.ShapeDtypeStruct((t, gw), BF16)
    return pl.pallas_call(
        functools.partial(_rope_perm_body, d=d),
        grid=(t // tm,),
        in_specs=[col(OFF_DQ), col(OFF_DK), col(OFF_DV), tab, tab],
        out_specs=[pl.BlockSpec((tm, gw), lambda i: (i, 0))] * 3,
        out_shape=[out, out, out],
        scratch_shapes=[pltpu.VMEM((tm, HEAD_DIM), F32)],
        compiler_params=_params("arbitrary"),
        name=f"rope_perm_d{d}",
    )(proj, proj, proj, cos, sin)


def _dil_body(q_ref, kc_ref, vc_ref, kp_ref, vp_ref, o_ref, lse_ref, *, d, nblk):
    step = pl.program_id(1)
    qi = lax.broadcasted_iota(jnp.int32, (BLOCK, BLOCK), 0)
    ki = lax.broadcasted_iota(jnp.int32, (BLOCK, BLOCK), 1)
    mask_cur = ki <= qi
    scale = HEAD_DIM ** -0.5
    for i in range(nblk):
        rows = slice(i * BLOCK, (i + 1) * BLOCK)
        if d >= nblk:
            kpr, vpr, prow, has_prev = kp_ref, vp_ref, rows, step >= d // nblk
        elif i == 0:
            kpr, vpr, prow, has_prev = kp_ref, vp_ref, slice(0, BLOCK), step > 0
        else:
            kpr, vpr, prow, has_prev = kc_ref, vc_ref, slice((i - 1) * BLOCK, i * BLOCK), None
        mask_prev = ki >= (qi if has_prev is None else qi + jnp.where(has_prev, 0, BLOCK))
        mask = jnp.concatenate([mask_prev, mask_cur], axis=1)
        for h in range(DIL_HEADS):
            cols = slice(h * HEAD_DIM, (h + 1) * HEAD_DIM)
            keys = jnp.concatenate([kpr[prow, cols], kc_ref[rows, cols]], axis=0)
            vals = jnp.concatenate([vpr[prow, cols], vc_ref[rows, cols]], axis=0)
            s = lax.dot_general(q_ref[rows, cols], keys, NT_DIMS, preferred_element_type=F32)
            s = jnp.where(mask, s * scale, NEG_INF)
            m = jnp.max(s, axis=-1, keepdims=True)
            p = jnp.exp(s - m).astype(BF16)
            v_aug = jnp.concatenate([vals, jnp.ones_like(vals)], axis=1)
            o_aug = jnp.dot(p, v_aug, preferred_element_type=F32)
            den = o_aug[:, HEAD_DIM:]
            o_ref[rows, cols] = o_aug[:, :HEAD_DIM] / den
            lse_ref[rows, cols] = m + jnp.log(den)


def _dilated_attention(q, k, v, d, batch, seq, *, nblk):
    gw = DIL_GROUP_WIDTH
    rows = nblk * BLOCK
    assert seq % rows == 0 and (d == 1 or d % nblk == 0)
    steps = seq // rows
    cur = pl.BlockSpec((rows, gw), lambda b, s: (b * steps + s, 0))
    if d >= nblk:
        back = d // nblk
        prev = pl.BlockSpec((rows, gw), lambda b, s: (b * steps + jnp.maximum(s - back, 0), 0))
    else:
        prev = pl.BlockSpec((BLOCK, gw),
                            lambda b, s: (b * steps * nblk + jnp.maximum(s * nblk - 1, 0), 0))
    out = jax.ShapeDtypeStruct((batch * seq, gw), F32)
    return pl.pallas_call(
        functools.partial(_dil_body, d=d, nblk=nblk),
        grid=(batch, steps),
        in_specs=[cur, cur, cur, prev, prev],
        out_specs=[cur, cur],
        out_shape=[out, out],
        compiler_params=_params("arbitrary", "arbitrary"),
        name=f"dil_attn_d{d}",
    )(q, k, v, k, v)


def _dil_merge_body(*refs):
    n = len(DILATIONS)
    o_refs, l_refs, out_ref, bufs = refs[:n], refs[n:2 * n], refs[2 * n], refs[2 * n + 1:]
    tm = out_ref.shape[0]
    outs, lses = [], []
    k = 0
    for d, o_ref, l_ref in zip(DILATIONS, o_refs, l_refs):
        if d == 1:
            outs.append(o_ref[...])
            lses.append(l_ref[...])
            continue
        span = BLOCK * d
        for src, buf in ((o_ref, bufs[k]), (l_ref, bufs[k + 1])):
            for s in range(tm // span):
                for r in range(d):
                    buf[pl.ds(s * span + r, BLOCK, stride=d), :] = (
                        src[s * span + r * BLOCK:s * span + (r + 1) * BLOCK, :])
        outs.append(bufs[k][...])
        lses.append(bufs[k + 1][...])
        k += 2
    m = functools.reduce(jnp.maximum, lses)
    es = [jnp.exp(l - m) for l in lses]
    num = sum(e * o for e, o in zip(es, outs))
    out_ref[...] = (num / sum(es)).astype(out_ref.dtype)


def _dil_merge(outs, lses):
    t, w = outs[0].shape
    tm = PERM_TILE
    assert t % tm == 0
    spec = pl.BlockSpec((tm, HEAD_DIM), lambda i, h: (i, h))
    n_buf = 2 * sum(1 for d in DILATIONS if d != 1)
    return pl.pallas_call(
        _dil_merge_body,
        grid=(t // tm, w // HEAD_DIM),
        in_specs=[spec] * (2 * len(DILATIONS)),
        out_specs=spec,
        out_shape=jax.ShapeDtypeStruct((t, w), BF16),
        scratch_shapes=[pltpu.VMEM((tm, HEAD_DIM), F32)] * n_buf,
        compiler_params=_params("arbitrary", "arbitrary"),
        name="dil_merge",
    )(*outs, *lses)


def _mem_body(q0, q1, q2, q3, kv_ref, o_ref):
    scale = MEM_HEAD_DIM ** -0.5
    for h, q_ref in enumerate((q0, q1, q2, q3)):
        cols = slice(h * MEM_HEAD_DIM, (h + 1) * MEM_HEAD_DIM)
        k = kv_ref[:, cols]
        v = kv_ref[:, MEM_WIDTH + h * MEM_HEAD_DIM:MEM_WIDTH + (h + 1) * MEM_HEAD_DIM]
        s = lax.dot_general(q_ref[...], k, NT_DIMS, preferred_element_type=F32) * scale
        m = jnp.max(s, axis=-1, keepdims=True)
        p = jnp.exp(s - m)
        den = jnp.sum(p, axis=-1, keepdims=True)
        o = jnp.dot(p.astype(BF16), v, preferred_element_type=F32) / den
        o_ref[:, cols] = o.astype(o_ref.dtype)


def _mem_attention(proj, mkv, batch, seq, *, tm):
    mem_len = mkv.shape[0] // batch
    assert seq % tm == 0
    steps = seq // tm
    q_specs = [pl.BlockSpec((tm, MEM_HEAD_DIM),
                            lambda b, i, h=h: (b * steps + i, OFF_MQ // MEM_HEAD_DIM + h))
               for h in range(MEM_HEADS)]
    return pl.pallas_call(
        _mem_body,
        grid=(batch, steps),
        in_specs=q_specs + [pl.BlockSpec((mem_len, 2 * MEM_WIDTH), lambda b, i: (b, 0))],
        out_specs=pl.BlockSpec((tm, MEM_WIDTH), lambda b, i: (b * steps + i, 0)),
        out_shape=jax.ShapeDtypeStruct((batch * seq, MEM_WIDTH), BF16),
        compiler_params=_params("arbitrary", "arbitrary"),
        name="mem_attn",
    )(proj, proj, proj, proj, mkv)


def _merge_body(a_ref, b_ref, m_ref, wa_ref, wb_ref, wm_ref, g0_ref, g1_ref, g2_ref, o_ref,
                wa_bf, wb_bf, wm_bf):
    @pl.when(pl.program_id(1) == 0)
    def _():
        wa_bf[...] = wa_ref[...].astype(BF16)
        wb_bf[...] = wb_ref[...].astype(BF16)
        wm_bf[...] = wm_ref[...].astype(BF16)

    ya = jnp.dot(a_ref[...], wa_bf[...], preferred_element_type=F32)
    yb = jnp.dot(b_ref[...], wb_bf[...], preferred_element_type=F32)
    ym = jnp.dot(m_ref[...], wm_bf[...], preferred_element_type=F32)
    o_ref[...] = (g0_ref[...] * ya.astype(BF16) + g1_ref[...] * yb.astype(BF16)
                  + g2_ref[...] * ym.astype(BF16))


def _merge(a_out, b_out, m_out, w_a, w_b, w_m, gates, *, tm, tn):
    t = a_out.shape[0]
    d = w_a.shape[1]
    assert t % tm == 0 and d % tn == 0
    nj = d // tn
    rows = lambda w: pl.BlockSpec((tm, w), lambda j, i: (i, 0))
    wcol = lambda k: pl.BlockSpec((k, tn), lambda j, i: (0, j))
    gate = lambda br: pl.BlockSpec((tm, tn), lambda j, i, br=br: (i, br * nj + j))
    return pl.pallas_call(
        _merge_body,
        grid=(nj, t // tm),
        in_specs=[rows(GM_WIDTH), rows(DIL_GROUP_WIDTH), rows(MEM_WIDTH),
                  wcol(GM_WIDTH), wcol(DIL_GROUP_WIDTH), wcol(MEM_WIDTH),
                  gate(0), gate(1), gate(2)],
        out_specs=pl.BlockSpec((tm, tn), lambda j, i: (i, j)),
        out_shape=jax.ShapeDtypeStruct((t, d), BF16),
        scratch_shapes=[pltpu.VMEM((GM_WIDTH, tn), BF16), pltpu.VMEM((DIL_GROUP_WIDTH, tn), BF16),
                        pltpu.VMEM((MEM_WIDTH, tn), BF16)],
        compiler_params=_params("arbitrary", "arbitrary"),
        name="branch_merge",
    )(a_out, b_out, m_out, w_a, w_b, w_m, gates, gates, gates)


def _sorting_network(n):
    pairs = []
    p = 1
    while p < n:
        k = p
        while k >= 1:
            for j in range(k % p, n - k, 2 * k):
                for i in range(min(k, n - j - k)):
                    if (i + j) // (2 * p) == (i + j + k) // (2 * p):
                        pairs.append((i + j, i + j + k))
            k //= 2
        p *= 2
    return pairs


SUBLANES = 8


def _top_values(x, k):
    n_groups = x.shape[0] // SUBLANES
    assert n_groups >= k
    g = [x[SUBLANES * j:SUBLANES * (j + 1), :] for j in range(n_groups)]
    for i, j in _sorting_network(n_groups):
        g[i], g[j] = jnp.maximum(g[i], g[j]), jnp.minimum(g[i], g[j])
    sub = lax.broadcasted_iota(jnp.int32, g[0].shape, 0)
    vals = []
    for r in range(k):
        m = jnp.max(g[0], axis=0, keepdims=True)
        vals.append(m)
        first = jnp.min(jnp.where(g[0] == m, sub, SUBLANES), axis=0, keepdims=True)
        hit = sub == first
        for j in range(k - 1 - r):
            g[j] = jnp.where(hit, g[j + 1], g[j])
    return jnp.concatenate(vals, axis=0)


def _staircase(a1, a2, combine, fill):
    r8 = lax.broadcasted_iota(jnp.int32, (8, a1.shape[1]), 0)
    r16 = lax.broadcasted_iota(jnp.int32, a1.shape, 0)
    lo8 = lambda x: x[0:8]
    return jnp.concatenate([
        combine(a1[0:1], a2),
        combine(a1[1:2], lo8(a2)),
        jnp.where(r8 < 5, combine(a1[2:3], lo8(a2)), fill),
        jnp.where(r8 < 4, combine(a1[3:4], lo8(a2)), fill),
        jnp.where(r16 >= 4, combine(a1, a2[0:1]), fill),
        jnp.where(r8 >= 4, combine(lo8(a1), a2[1:2]), fill),
        jnp.where(r8 == 4, combine(lo8(a1), a2[2:3]), fill),
    ], axis=0)


def _peer_route_body(q_ref, keys_ref, g_ref, cnt_ref, e1_ref, rank_ref, e2_ref):
    tm = q_ref.shape[0]
    k = PEER_TOPK
    for h in range(PEER_HEADS):
        sts, tops = [], []
        for p in range(2):
            hp = 2 * h + p
            qs = q_ref[:, hp * PEER_HALF:(hp + 1) * PEER_HALF]
            st = lax.dot_general(keys_ref[hp], qs, NT_DIMS, preferred_element_type=F32,
                                 precision=lax.Precision.HIGHEST)
            sts.append(st)
            tops.append(_top_values(st, k))
        a1, a2 = tops
        m1, m2 = a1[0:1], a2[0:1]
        cand = _staircase(a1, a2, jnp.add, -jnp.inf)
        prob = _staircase(jnp.exp(a1 - m1), jnp.exp(a2 - m2), jnp.multiply, 0.0)
        x = cand
        left = jnp.full((1, tm), float(k), F32)
        tau = jnp.full((1, tm), jnp.inf, F32)
        for _ in range(k):
            m = jnp.max(x, axis=0, keepdims=True)
            eq = x == m
            tau = jnp.where(left > 0, m, tau)
            left = left - jnp.sum(jnp.where(eq, 1.0, 0.0), axis=0, keepdims=True)
            x = jnp.where(eq, -jnp.inf, x)
        z = jnp.sum(jnp.where(cand >= tau, prob, 0.0), axis=0, keepdims=True)
        cnt = jnp.zeros((PEER_KEYS, tm), F32)
        rank = jnp.zeros((PEER_KEYS, tm), F32)
        for b in range(k):
            cnt = cnt + jnp.where(sts[0] + a2[b:b + 1] >= tau, 1.0, 0.0)
            rank = rank + jnp.where(a2[b:b + 1] > sts[1], 1.0, 0.0)
        cnt_ref[h] = cnt
        e1_ref[h] = jnp.exp(sts[0] - m1) / z
        rank_ref[h] = rank.astype(BF16)
        e2_ref[h] = jnp.exp(sts[1] - m2).astype(BF16)

    def tile16(row_f32):
        packed = jnp.broadcast_to(row_f32, (16, tm)).astype(BF16)
        return jnp.concatenate([packed] * (PEER_KEYS // 16), axis=0)

    def build(i1, carry):
        gate = None
        for h in range(PEER_HEADS):
            cnt = tile16(cnt_ref[h, pl.ds(i1, 1), :])
            e1 = tile16(e1_ref[h, pl.ds(i1, 1), :])
            sel = jnp.where(rank_ref[h] < cnt, e1 * e2_ref[h], jnp.zeros((), BF16))
            gate = sel if gate is None else gate + sel
        g_ref[pl.ds(pl.multiple_of(i1 * PEER_KEYS, PEER_KEYS), PEER_KEYS), :] = gate
        return carry

    lax.fori_loop(0, PEER_KEYS, build, 0)


def _peer_route(q, sub_keys, *, tm):
    t = q.shape[0]
    assert t % tm == 0
    hp = 2 * PEER_HEADS
    keys = sub_keys.reshape(hp, PEER_KEYS, PEER_HALF)
    n_exp = PEER_KEYS * PEER_KEYS
    per_head = lambda dt: pltpu.VMEM((PEER_HEADS, PEER_KEYS, tm), dt)
    return pl.pallas_call(
        _peer_route_body,
        grid=(t // tm,),
        in_specs=[pl.BlockSpec((tm, hp * PEER_HALF), lambda i: (i, 0)),
                  pl.BlockSpec((hp, PEER_KEYS, PEER_HALF), lambda i: (0, 0, 0))],
        out_specs=pl.BlockSpec((n_exp, tm), lambda i: (0, i)),
        out_shape=jax.ShapeDtypeStruct((n_exp, t), BF16),
        scratch_shapes=[per_head(F32), per_head(F32), per_head(BF16), per_head(BF16)],
        compiler_params=_params("arbitrary"),
        name="peer_route",
    )(q, keys)


def _peer_dense_body(xt_ref, u_ref, v_ref, g_ref, o_ref):
    @pl.when(pl.program_id(1) == 0)
    def _():
        o_ref[...] = jnp.zeros_like(o_ref)

    act_t = jnp.dot(u_ref[...], xt_ref[...], preferred_element_type=F32)
    inner = GELU_C * (act_t + 0.044715 * (act_t ** 3))
    cdf = 0.5 * (1.0 + jnp.tanh(inner.astype(BF16)))
    w_t = act_t.astype(BF16) * cdf * g_ref[...]
    o_ref[...] += jnp.dot(w_t.T, v_ref[...], preferred_element_type=F32)


def _peer_dense(h_t, u, v, gate_t, *, tm, te):
    d, t = h_t.shape
    n_exp = u.shape[0]
    assert t % tm == 0 and n_exp % te == 0
    once = dict(pipeline_mode=pl.Buffered(1))
    return pl.pallas_call(
        _peer_dense_body,
        grid=(t // tm, n_exp // te),
        in_specs=[pl.BlockSpec((d, tm), lambda i, e: (0, i), **once),
                  pl.BlockSpec((te, d), lambda i, e: (e, 0)),
                  pl.BlockSpec((te, d), lambda i, e: (e, 0)),
                  pl.BlockSpec((te, tm), lambda i, e: (e, i))],
        out_specs=pl.BlockSpec((tm, d), lambda i, e: (i, 0), **once),
        out_shape=jax.ShapeDtypeStruct((t, d), F32),
        compiler_params=_params("arbitrary", "arbitrary"),
        name="peer_dense",
    )(h_t, u, v, gate_t)


MM_TM, MM_TN = 1024, 512
ROW_TM = 256
DIL_BLOCKS_PER_STEP = 4
PEER_TM, PEER_TE = 1024, 512
ROUTE_TM = 256


def kernel(x, mem, positions, w_in, gm_ln_g, gm_ln_b, gm_w_s, gm_b_s, w_mem_kv, w_gate, b_gate,
           w_br_gmlp, w_br_dil, w_br_mem, w_out, ln_mix_g, ln_mix_b, peer_w_query, peer_sub_keys,
           peer_u, peer_v, ln_ffn_g, ln_ffn_b):
    batch, seq, d = x.shape
    t = batch * seq
    assert w_in.shape[0] == DEPTH and w_in.shape[2] == IN_WIDTH
    x2 = x.reshape(t, d)
    x_bf = x2.astype(BF16)
    mem_bf = mem.reshape(-1, d).astype(BF16)

    n_exp = PEER_KEYS * PEER_KEYS
    proj, v_bf = _matmul(x_bf, w_in.reshape(d, IN_WIDTH), out_dtype=BF16, tm=MM_TM, tn=MM_TN,
                         side=peer_v.reshape(n_exp, d), name="in_proj")
    gates, u_bf = _matmul(x_bf, w_gate.reshape(d, 3 * d), bias=b_gate.reshape(3 * d),
                          act="sigmoid", out_dtype=BF16, tm=MM_TM, tn=MM_TN,
                          side=peer_u.reshape(n_exp, d), name="gate_proj")
    mkv = _matmul(mem_bf, w_mem_kv.reshape(d, 2 * MEM_WIDTH), out_dtype=BF16, tm=MM_TM, tn=MM_TN,
                  name="mem_kv_proj")

    a_out = _gmlp(proj, gm_ln_g.reshape(-1), gm_ln_b.reshape(-1), gm_w_s.reshape(GM_GROUPS, BLOCK, BLOCK),
                  gm_b_s.reshape(GM_GROUPS, BLOCK), tm=512)

    cos, sin = _rope_table(positions, tm=1024)
    outs, lses = [], []
    for g, dil in enumerate(DILATIONS):
        qg, kg, vg = _rope_perm(proj, cos, sin, g, dil)
        o, lse = _dilated_attention(qg, kg, vg, dil, batch, seq, nblk=DIL_BLOCKS_PER_STEP)
        outs.append(o)
        lses.append(lse)
    b_out = _dil_merge(outs, lses)

    m_out = _mem_attention(proj, mkv, batch, seq, tm=512)

    merged = _merge(a_out, b_out, m_out, w_br_gmlp.reshape(GM_WIDTH, d),
                    w_br_dil.reshape(DIL_GROUP_WIDTH, d), w_br_mem.reshape(MEM_WIDTH, d), gates,
                    tm=MM_TM, tn=MM_TN)
    z1 = _matmul(merged, w_out.reshape(d, d), res=x2, alpha=ALPHA, out_dtype=F32, tm=MM_TM,
                 tn=MM_TN, name="out_proj")
    h1_bf, h1_t = _layer_norm(z1, ln_mix_g.reshape(-1), ln_mix_b.reshape(-1), want_f32=False,
                              want_t=True, tm=ROW_TM, name="ln_mix")

    hq = PEER_HEADS * 2 * PEER_HALF
    q = _matmul(h1_bf, peer_w_query.reshape(d, hq), out_dtype=F32, tm=MM_TM, tn=MM_TN,
                name="peer_query")
    gate_t = _peer_route(q, peer_sub_keys, tm=ROUTE_TM)
    y = _peer_dense(h1_t, u_bf, v_bf, gate_t, tm=PEER_TM, te=PEER_TE)
    (out,) = _layer_norm(h1_bf, ln_ffn_g.reshape(-1), ln_ffn_b.reshape(-1), add=y, alpha=ALPHA,
                         tm=ROW_TM, name="ln_ffn")
    return out.reshape(batch, seq, d)
```

```python
import functools

import numpy as np
import jax
import jax.numpy as jnp
from jax import lax
from jax.experimental import pallas as pl
from jax.experimental.pallas import tpu as pltpu

F32 = jnp.float32
BF16 = jnp.bfloat16

HEAD_DIM = 128
BLOCK = 128
GM_GROUPS = 12
GM_WIDTH = GM_GROUPS * 128
DILATIONS = (1, 4, 16)
DIL_HEADS = 4
DIL_GROUP_WIDTH = DIL_HEADS * HEAD_DIM
DIL_WIDTH = len(DILATIONS) * DIL_GROUP_WIDTH
MEM_HEADS = 4
MEM_HEAD_DIM = 256
MEM_WIDTH = MEM_HEADS * MEM_HEAD_DIM
PEER_HEADS = 8
PEER_KEYS = 128
PEER_TOPK = 16
PEER_HALF = 128
ROPE_THETA = 10000.0
LN_EPS = 1e-5
DEPTH = 1
ALPHA = (2 * DEPTH) ** 0.25
NEG_INF = -1e30

OFF_GU, OFF_GV = 0, GM_WIDTH
OFF_DQ = 2 * GM_WIDTH
OFF_DK = OFF_DQ + DIL_WIDTH
OFF_DV = OFF_DK + DIL_WIDTH
OFF_MQ = OFF_DV + DIL_WIDTH
IN_WIDTH = OFF_MQ + MEM_WIDTH

V7X_VMEM_BYTES = 64 * 1024 * 1024
VMEM_LIMIT = V7X_VMEM_BYTES - 8 * 1024 * 1024

NT_DIMS = (((1,), (1,)), ((), ()))
SIDE_ROWS = 128
GELU_C = float(np.float32(np.sqrt(2.0 / np.pi)))


def _params(*sem):
    return pltpu.CompilerParams(dimension_semantics=sem, vmem_limit_bytes=VMEM_LIMIT)


def _mm_body(*refs, act, alpha, has_bias, has_res, has_side):
    x_ref, w_ref = refs[0], refs[1]
    k = 2
    b_ref = r_ref = s_ref = None
    if has_bias:
        b_ref = refs[k]
        k += 1
    if has_res:
        r_ref = refs[k]
        k += 1
    if has_side:
        s_ref = refs[k]
        k += 1
    o_ref = refs[k]
    k += 1
    if has_side:
        refs[k][...] = s_ref[...].astype(BF16)
        k += 1
    wb_ref = refs[k]

    @pl.when(pl.program_id(1) == 0)
    def _():
        wb_ref[...] = w_ref[...].astype(BF16)

    acc = jnp.dot(x_ref[...], wb_ref[...], preferred_element_type=F32)
    if has_bias:
        acc = acc + b_ref[...]
    if act == "sigmoid":
        acc = 0.5 * jnp.tanh(0.5 * acc.astype(BF16)) + 0.5
    if has_res:
        acc = alpha * r_ref[...] + acc
    o_ref[...] = acc.astype(o_ref.dtype)


def _matmul(x, w, *, out_dtype, tm, tn, bias=None, res=None, alpha=1.0, act=None, side=None, name):
    m, k = x.shape
    n = w.shape[1]
    tm, tn = min(tm, m), min(tn, n)
    assert m % tm == 0 and n % tn == 0
    steps_i = m // tm
    in_specs = [pl.BlockSpec((tm, k), lambda j, i: (i, 0)),
                pl.BlockSpec((k, tn), lambda j, i: (0, j))]
    args = [x, w]
    if bias is not None:
        in_specs.append(pl.BlockSpec((1, tn), lambda j, i: (0, j)))
        args.append(bias.reshape(1, n))
    if res is not None:
        in_specs.append(pl.BlockSpec((tm, tn), lambda j, i: (i, j)))
        args.append(res)
    out_specs = [pl.BlockSpec((tm, tn), lambda j, i: (i, j))]
    out_shape = [jax.ShapeDtypeStruct((m, n), out_dtype)]
    if side is not None:
        rows, width = side.shape
        side_rows = SIDE_ROWS
        while rows // side_rows > (n // tn) * steps_i:
            side_rows *= 2
        n_blocks = rows // side_rows
        assert rows % side_rows == 0
        spec = pl.BlockSpec((side_rows, width),
                            lambda j, i: (jnp.minimum(j * steps_i + i, n_blocks - 1), 0))
        in_specs.append(spec)
        args.append(side)
        out_specs.append(spec)
        out_shape.append(jax.ShapeDtypeStruct(side.shape, BF16))
    body = functools.partial(_mm_body, act=act, alpha=alpha, has_bias=bias is not None,
                             has_res=res is not None, has_side=side is not None)
    outs = pl.pallas_call(
        body,
        grid=(n // tn, steps_i),
        in_specs=in_specs,
        out_specs=out_specs,
        out_shape=out_shape,
        scratch_shapes=[pltpu.VMEM((k, tn), BF16)],
        compiler_params=_params("arbitrary", "arbitrary"),
        name=name,
    )(*args)
    return outs if side is not None else outs[0]


def _ln_rows(z, g, b):
    mu = jnp.mean(z, axis=-1, keepdims=True)
    zc = z - mu
    var = jnp.mean(zc * zc, axis=-1, keepdims=True)
    return zc * lax.rsqrt(var + LN_EPS) * g + b


def _ln_body(*refs, alpha, has_add, want_f32, want_t):
    k = 0
    z = refs[k][...].astype(F32)
    k += 1
    if has_add:
        z = alpha * z + refs[k][...]
        k += 1
    g_ref, b_ref = refs[k], refs[k + 1]
    k += 2
    h = _ln_rows(z, g_ref[...], b_ref[...])
    if want_f32:
        refs[k][...] = h
        k += 1
    if want_t:
        refs[k][...] = h.astype(BF16)
        refs[k + 1][...] = h.T.astype(BF16)


def _layer_norm(z, g, b, *, add=None, alpha=1.0, want_f32=True, want_t=False, tm, name):
    t, d = z.shape
    tm = min(tm, t)
    assert t % tm == 0
    row = pl.BlockSpec((tm, d), lambda i: (i, 0))
    vec = pl.BlockSpec((1, d), lambda i: (0, 0))
    in_specs, args = [row], [z]
    if add is not None:
        in_specs.append(row)
        args.append(add)
    in_specs += [vec, vec]
    args += [g.reshape(1, d), b.reshape(1, d)]
    out_specs, out_shape = [], []
    if want_f32:
        out_specs += [row]
        out_shape += [jax.ShapeDtypeStruct((t, d), F32)]
    if want_t:
        out_specs += [row, pl.BlockSpec((d, tm), lambda i: (0, i))]
        out_shape += [jax.ShapeDtypeStruct((t, d), BF16), jax.ShapeDtypeStruct((d, t), BF16)]
    return pl.pallas_call(
        functools.partial(_ln_body, alpha=alpha, has_add=add is not None, want_f32=want_f32,
                          want_t=want_t),
        grid=(t // tm,),
        in_specs=in_specs,
        out_specs=out_specs,
        out_shape=out_shape,
        compiler_params=_params("arbitrary"),
        name=name,
    )(*args)


def _gmlp_body(gu_ref, gv_ref, g_ref, b_ref, ws_ref, bs_ref, o_ref):
    tm = gu_ref.shape[0]
    u = jax.nn.gelu(gu_ref[...].astype(F32))
    v = _ln_rows(jax.nn.gelu(gv_ref[...].astype(F32)), g_ref[...], b_ref[...]).astype(BF16)
    causal = (lax.broadcasted_iota(jnp.int32, (BLOCK, BLOCK), 0)
              >= lax.broadcasted_iota(jnp.int32, (BLOCK, BLOCK), 1))
    for g in range(GM_GROUPS):
        cols = slice(g * 128, (g + 1) * 128)
        w = jnp.where(causal, ws_ref[g], 0.0).astype(BF16)
        for c in range(tm // BLOCK):
            rows = slice(c * BLOCK, (c + 1) * BLOCK)
            mixed = jnp.dot(w, v[rows, cols], preferred_element_type=F32) + bs_ref[g]
            o_ref[rows, cols] = (u[rows, cols] * mixed).astype(o_ref.dtype)


def _gmlp(proj, ln_g, ln_b, w_s, b_s, *, tm):
    t = proj.shape[0]
    assert t % tm == 0 and tm % BLOCK == 0
    bs_full = jnp.broadcast_to(b_s[:, :, None], (GM_GROUPS, BLOCK, 128))
    return pl.pallas_call(
        _gmlp_body,
        grid=(t // tm,),
        in_specs=[pl.BlockSpec((tm, GM_WIDTH), lambda i: (i, OFF_GU // GM_WIDTH)),
                  pl.BlockSpec((tm, GM_WIDTH), lambda i: (i, OFF_GV // GM_WIDTH)),
                  pl.BlockSpec((1, GM_WIDTH), lambda i: (0, 0)),
                  pl.BlockSpec((1, GM_WIDTH), lambda i: (0, 0)),
                  pl.BlockSpec((GM_GROUPS, BLOCK, BLOCK), lambda i: (0, 0, 0)),
                  pl.BlockSpec((GM_GROUPS, BLOCK, 128), lambda i: (0, 0, 0))],
        out_specs=pl.BlockSpec((tm, GM_WIDTH), lambda i: (i, 0)),
        out_shape=jax.ShapeDtypeStruct((t, GM_WIDTH), BF16),
        compiler_params=_params("arbitrary"),
        name="gmlp",
    )(proj, proj, ln_g.reshape(1, GM_WIDTH), ln_b.reshape(1, GM_WIDTH), w_s, bs_full)


PERM_TILE = BLOCK * max(DILATIONS)


def _rope_table_body(pos_ref, inv_ref, cos_ref, sin_ref):
    half = HEAD_DIM // 2
    ang = pos_ref[...] * inv_ref[...]
    lane = lax.broadcasted_iota(jnp.int32, ang.shape, 1)
    sin = jnp.sin(ang)
    cos_ref[...] = jnp.cos(ang)
    sin_ref[...] = jnp.where(lane < half, -sin, sin)


def _rope_table(positions, *, tm):
    t = positions.size
    assert t % tm == 0
    half = HEAD_DIM // 2
    inv_freq = ROPE_THETA ** (-jnp.arange(half, dtype=F32) / half)
    inv = jnp.concatenate([inv_freq, inv_freq]).reshape(1, HEAD_DIM)
    pos = positions.astype(F32).reshape(t, 1)
    out = jax.ShapeDtypeStruct((t, HEAD_DIM), F32)
    return pl.pallas_call(
        _rope_table_body,
        grid=(t // tm,),
        in_specs=[pl.BlockSpec((tm, 1), lambda i: (i, 0)),
                  pl.BlockSpec((1, HEAD_DIM), lambda i: (0, 0))],
        out_specs=[pl.BlockSpec((tm, HEAD_DIM), lambda i: (i, 0))] * 2,
        out_shape=[out, out],
        compiler_params=_params("arbitrary"),
        name="rope_table",
    )(pos, inv)


def _rope_perm_body(q_ref, k_ref, v_ref, cos_ref, sin_ref, qo_ref, ko_ref, vo_ref, buf_ref, *, d):
    tm = q_ref.shape[0]
    half = HEAD_DIM // 2
    span = BLOCK * d
    for h in range(DIL_HEADS):
        cols = slice(h * HEAD_DIM, (h + 1) * HEAD_DIM)
        for src, dst, rotate in ((q_ref, qo_ref, True), (k_ref, ko_ref, True), (v_ref, vo_ref, False)):
            t = src[:, cols].astype(F32)
            if rotate:
                swapped = jnp.concatenate([t[:, half:], t[:, :half]], axis=1)
                t = t * cos_ref[...] + swapped * sin_ref[...]
            if d == 1:
                dst[:, cols] = t.astype(dst.dtype)
                continue
            buf_ref[...] = t
            for n in range(tm // span):
                for r in range(d):
                    rows = buf_ref[pl.ds(n * span + r, BLOCK, stride=d), :]
                    dst[n * span + r * BLOCK:n * span + (r + 1) * BLOCK, cols] = rows.astype(dst.dtype)


def _rope_perm(proj, cos, sin, group, d):
    t = proj.shape[0]
    tm, gw = PERM_TILE, DIL_GROUP_WIDTH
    assert t % tm == 0 and tm % (BLOCK * d) == 0
    col = lambda off: pl.BlockSpec((tm, gw), lambda i: (i, off // gw + group))
    tab = pl.BlockSpec((tm, HEAD_DIM), lambda i: (i, 0))
    out = jax.ShapeDtypeStruct((t, gw), BF16)
    return pl.pallas_call(
        functools.partial(_rope_perm_body, d=d),
        grid=(t // tm,),
        in_specs=[col(OFF_DQ), col(OFF_DK), col(OFF_DV), tab, tab],
        out_specs=[pl.BlockSpec((tm, gw), lambda i: (i, 0))] * 3,
        out_shape=[out, out, out],
        scratch_shapes=[pltpu.VMEM((tm, HEAD_DIM), F32)],
        compiler_params=_params("arbitrary"),
        name=f"rope_perm_d{d}",
    )(proj, proj, proj, cos, sin)


def _dil_body(q_ref, kc_ref, vc_ref, kp_ref, vp_ref, o_ref, lse_ref, *, d, nblk):
    step = pl.program_id(1)
    qi = lax.broadcasted_iota(jnp.int32, (BLOCK, BLOCK), 0)
    ki = lax.broadcasted_iota(jnp.int32, (BLOCK, BLOCK), 1)
    mask_cur = ki <= qi
    scale = HEAD_DIM ** -0.5
    for i in range(nblk):
        rows = slice(i * BLOCK, (i + 1) * BLOCK)
        if d >= nblk:
            kpr, vpr, prow, has_prev = kp_ref, vp_ref, rows, step >= d // nblk
        elif i == 0:
            kpr, vpr, prow, has_prev = kp_ref, vp_ref, slice(0, BLOCK), step > 0
        else:
            kpr, vpr, prow, has_prev = kc_ref, vc_ref, slice((i - 1) * BLOCK, i * BLOCK), None
        mask_prev = ki >= (qi if has_prev is None else qi + jnp.where(has_prev, 0, BLOCK))
        mask = jnp.concatenate([mask_prev, mask_cur], axis=1)
        for h in range(DIL_HEADS):
            cols = slice(h * HEAD_DIM, (h + 1) * HEAD_DIM)
            keys = jnp.concatenate([kpr[prow, cols], kc_ref[rows, cols]], axis=0)
            vals = jnp.concatenate([vpr[prow, cols], vc_ref[rows, cols]], axis=0)
            s = lax.dot_general(q_ref[rows, cols], keys, NT_DIMS, preferred_element_type=F32)
            s = jnp.where(mask, s * scale, NEG_INF)
            m = jnp.max(s, axis=-1, keepdims=True)
            p = jnp.exp(s - m).astype(BF16)
            v_aug = jnp.concatenate([vals, jnp.ones_like(vals)], axis=1)
            o_aug = jnp.dot(p, v_aug, preferred_element_type=F32)
            den = o_aug[:, HEAD_DIM:]
            o_ref[rows, cols] = (o_aug[:, :HEAD_DIM] / den).astype(o_ref.dtype)
            lse_ref[rows, cols] = m + jnp.log(den)


def _dilated_attention(q, k, v, d, batch, seq, *, nblk):
    gw = DIL_GROUP_WIDTH
    rows = nblk * BLOCK
    assert seq % rows == 0 and (d == 1 or d % nblk == 0)
    steps = seq // rows
    cur = pl.BlockSpec((rows, gw), lambda b, s: (b * steps + s, 0))
    if d >= nblk:
        back = d // nblk
        prev = pl.BlockSpec((rows, gw), lambda b, s: (b * steps + jnp.maximum(s - back, 0), 0))
    else:
        prev = pl.BlockSpec((BLOCK, gw),
                            lambda b, s: (b * steps * nblk + jnp.maximum(s * nblk - 1, 0), 0))
    out = jax.ShapeDtypeStruct((batch * seq, gw), F32)
    return pl.pallas_call(
        functools.partial(_dil_body, d=d, nblk=nblk),
        grid=(batch, steps),
        in_specs=[cur, cur, cur, prev, prev],
        out_specs=[cur, cur],
        out_shape=[jax.ShapeDtypeStruct(out.shape, BF16), out],
        compiler_params=_params("arbitrary", "arbitrary"),
        name=f"dil_attn_d{d}",
    )(q, k, v, k, v)


def _dil_merge_body(*refs):
    n = len(DILATIONS)
    o_refs, l_refs, out_ref, bufs = refs[:n], refs[n:2 * n], refs[2 * n], refs[2 * n + 1:]
    tm = out_ref.shape[0]
    outs, lses = [], []
    k = 0
    for d, o_ref, l_ref in zip(DILATIONS, o_refs, l_refs):
        if d == 1:
            outs.append(o_ref[...].astype(F32))
            lses.append(l_ref[...])
            continue
        span = BLOCK * d
        for src, buf in ((o_ref, bufs[k]), (l_ref, bufs[k + 1])):
            for s in range(tm // span):
                for r in range(d):
                    buf[pl.ds(s * span + r, BLOCK, stride=d), :] = (
                        src[s * span + r * BLOCK:s * span + (r + 1) * BLOCK, :].astype(F32))
        outs.append(bufs[k][...])
        lses.append(bufs[k + 1][...])
        k += 2
    m = functools.reduce(jnp.maximum, lses)
    es = [jnp.exp(l - m) for l in lses]
    num = sum(e * o for e, o in zip(es, outs))
    out_ref[...] = (num / sum(es)).astype(out_ref.dtype)


def _dil_merge(outs, lses):
    t, w = outs[0].shape
    tm = PERM_TILE
    assert t % tm == 0
    spec = pl.BlockSpec((tm, HEAD_DIM), lambda i, h: (i, h))
    n_buf = 2 * sum(1 for d in DILATIONS if d != 1)
    return pl.pallas_call(
        _dil_merge_body,
        grid=(t // tm, w // HEAD_DIM),
        in_specs=[spec] * (2 * len(DILATIONS)),
        out_specs=spec,
        out_shape=jax.ShapeDtypeStruct((t, w), BF16),
        scratch_shapes=[pltpu.VMEM((tm, HEAD_DIM), F32)] * n_buf,
        compiler_params=_params("arbitrary", "arbitrary"),
        name="dil_merge",
    )(*outs, *lses)


def _mem_body(q0, q1, q2, q3, kv_ref, o_ref):
    scale = MEM_HEAD_DIM ** -0.5
    for h, q_ref in enumerate((q0, q1, q2, q3)):
        cols = slice(h * MEM_HEAD_DIM, (h + 1) * MEM_HEAD_DIM)
        k = kv_ref[:, cols]
        v = kv_ref[:, MEM_WIDTH + h * MEM_HEAD_DIM:MEM_WIDTH + (h + 1) * MEM_HEAD_DIM]
        s = lax.dot_general(q_ref[...], k, NT_DIMS, preferred_element_type=F32) * scale
        m = jnp.max(s, axis=-1, keepdims=True)
        p = jnp.exp(s - m)
        den = jnp.sum(p, axis=-1, keepdims=True)
        o = jnp.dot(p.astype(BF16), v, preferred_element_type=F32) / den
        o_ref[:, cols] = o.astype(o_ref.dtype)


def _mem_attention(proj, mkv, batch, seq, *, tm):
    mem_len = mkv.shape[0] // batch
    assert seq % tm == 0
    steps = seq // tm
    q_specs = [pl.BlockSpec((tm, MEM_HEAD_DIM),
                            lambda b, i, h=h: (b * steps + i, OFF_MQ // MEM_HEAD_DIM + h))
               for h in range(MEM_HEADS)]
    return pl.pallas_call(
        _mem_body,
        grid=(batch, steps),
        in_specs=q_specs + [pl.BlockSpec((mem_len, 2 * MEM_WIDTH), lambda b, i: (b, 0))],
        out_specs=pl.BlockSpec((tm, MEM_WIDTH), lambda b, i: (b * steps + i, 0)),
        out_shape=jax.ShapeDtypeStruct((batch * seq, MEM_WIDTH), BF16),
        compiler_params=_params("arbitrary", "arbitrary"),
        name="mem_attn",
    )(proj, proj, proj, proj, mkv)


def _merge_body(a_ref, b_ref, m_ref, wa_ref, wb_ref, wm_ref, g0_ref, g1_ref, g2_ref, o_ref,
                wa_bf, wb_bf, wm_bf):
    @pl.when(pl.program_id(1) == 0)
    def _():
        wa_bf[...] = wa_ref[...].astype(BF16)
        wb_bf[...] = wb_ref[...].astype(BF16)
        wm_bf[...] = wm_ref[...].astype(BF16)

    ya = jnp.dot(a_ref[...], wa_bf[...], preferred_element_type=F32)
    yb = jnp.dot(b_ref[...], wb_bf[...], preferred_element_type=F32)
    ym = jnp.dot(m_ref[...], wm_bf[...], preferred_element_type=F32)
    o_ref[...] = (g0_ref[...] * ya.astype(BF16) + g1_ref[...] * yb.astype(BF16)
                  + g2_ref[...] * ym.astype(BF16))


def _merge(a_out, b_out, m_out, w_a, w_b, w_m, gates, *, tm, tn):
    t = a_out.shape[0]
    d = w_a.shape[1]
    assert t % tm == 0 and d % tn == 0
    nj = d // tn
    rows = lambda w: pl.BlockSpec((tm, w), lambda j, i: (i, 0))
    wcol = lambda k: pl.BlockSpec((k, tn), lambda j, i: (0, j))
    gate = lambda br: pl.BlockSpec((tm, tn), lambda j, i, br=br: (i, br * nj + j))
    return pl.pallas_call(
        _merge_body,
        grid=(nj, t // tm),
        in_specs=[rows(GM_WIDTH), rows(DIL_GROUP_WIDTH), rows(MEM_WIDTH),
                  wcol(GM_WIDTH), wcol(DIL_GROUP_WIDTH), wcol(MEM_WIDTH),
                  gate(0), gate(1), gate(2)],
        out_specs=pl.BlockSpec((tm, tn), lambda j, i: (i, j)),
        out_shape=jax.ShapeDtypeStruct((t, d), BF16),
        scratch_shapes=[pltpu.VMEM((GM_WIDTH, tn), BF16), pltpu.VMEM((DIL_GROUP_WIDTH, tn), BF16),
                        pltpu.VMEM((MEM_WIDTH, tn), BF16)],
        compiler_params=_params("arbitrary", "arbitrary"),
        name="branch_merge",
    )(a_out, b_out, m_out, w_a, w_b, w_m, gates, gates, gates)


def _sorting_network(n):
    pairs = []
    p = 1
    while p < n:
        k = p
        while k >= 1:
            for j in range(k % p, n - k, 2 * k):
                for i in range(min(k, n - j - k)):
                    if (i + j) // (2 * p) == (i + j + k) // (2 * p):
                        pairs.append((i + j, i + j + k))
            k //= 2
        p *= 2
    return pairs


SUBLANES = 8


def _top_values(x, k):
    n_groups = x.shape[0] // SUBLANES
    assert n_groups >= k
    g = [x[SUBLANES * j:SUBLANES * (j + 1), :] for j in range(n_groups)]
    for i, j in _sorting_network(n_groups):
        g[i], g[j] = jnp.maximum(g[i], g[j]), jnp.minimum(g[i], g[j])
    sub = lax.broadcasted_iota(jnp.int32, g[0].shape, 0)
    vals = []
    for r in range(k):
        m = jnp.max(g[0], axis=0, keepdims=True)
        vals.append(m)
        first = jnp.min(jnp.where(g[0] == m, sub, SUBLANES), axis=0, keepdims=True)
        hit = sub == first
        for j in range(k - 1 - r):
            g[j] = jnp.where(hit, g[j + 1], g[j])
    return jnp.concatenate(vals, axis=0)


def _staircase(a1, a2, combine, fill):
    r8 = lax.broadcasted_iota(jnp.int32, (8, a1.shape[1]), 0)
    r16 = lax.broadcasted_iota(jnp.int32, a1.shape, 0)
    lo8 = lambda x: x[0:8]
    return jnp.concatenate([
        combine(a1[0:1], a2),
        combine(a1[1:2], lo8(a2)),
        jnp.where(r8 < 5, combine(a1[2:3], lo8(a2)), fill),
        jnp.where(r8 < 4, combine(a1[3:4], lo8(a2)), fill),
        jnp.where(r16 >= 4, combine(a1, a2[0:1]), fill),
        jnp.where(r8 >= 4, combine(lo8(a1), a2[1:2]), fill),
        jnp.where(r8 == 4, combine(lo8(a1), a2[2:3]), fill),
    ], axis=0)


def _peer_route_body(q_ref, keys_ref, g_ref, cnt_ref, e1_ref, rank_ref, e2_ref):
    tm = q_ref.shape[0]
    k = PEER_TOPK
    for h in range(PEER_HEADS):
        sts, tops = [], []
        for p in range(2):
            hp = 2 * h + p
            qs = q_ref[:, hp * PEER_HALF:(hp + 1) * PEER_HALF]
            st = lax.dot_general(keys_ref[hp], qs, NT_DIMS, preferred_element_type=F32,
                                 precision=lax.Precision.HIGHEST)
            sts.append(st)
            tops.append(_top_values(st, k))
        a1, a2 = tops
        m1, m2 = a1[0:1], a2[0:1]
        cand = _staircase(a1, a2, jnp.add, -jnp.inf)
        prob = _staircase(jnp.exp(a1 - m1), jnp.exp(a2 - m2), jnp.multiply, 0.0)
        x = cand
        left = jnp.full((1, tm), float(k), F32)
        tau = jnp.full((1, tm), jnp.inf, F32)
        for _ in range(k):
            m = jnp.max(x, axis=0, keepdims=True)
            eq = x == m
            tau = jnp.where(left > 0, m, tau)
            left = left - jnp.sum(jnp.where(eq, 1.0, 0.0), axis=0, keepdims=True)
            x = jnp.where(eq, -jnp.inf, x)
        z = jnp.sum(jnp.where(cand >= tau, prob, 0.0), axis=0, keepdims=True)
        cnt = jnp.zeros((PEER_KEYS, tm), F32)
        rank = jnp.zeros((PEER_KEYS, tm), F32)
        for b in range(k):
            cnt = cnt + jnp.where(sts[0] + a2[b:b + 1] >= tau, 1.0, 0.0)
            rank = rank + jnp.where(a2[b:b + 1] > sts[1], 1.0, 0.0)
        cnt_ref[h] = cnt
        e1_ref[h] = jnp.exp(sts[0] - m1) / z
        rank_ref[h] = rank.astype(BF16)
        e2_ref[h] = jnp.exp(sts[1] - m2).astype(BF16)

    def tile16(row_f32):
        packed = jnp.broadcast_to(row_f32, (16, tm)).astype(BF16)
        return jnp.concatenate([packed] * (PEER_KEYS // 16), axis=0)

    def build(i1, carry):
        gate = None
        for h in range(PEER_HEADS):
            cnt = tile16(cnt_ref[h, pl.ds(i1, 1), :])
            e1 = tile16(e1_ref[h, pl.ds(i1, 1), :])
            sel = jnp.where(rank_ref[h] < cnt, e1 * e2_ref[h], jnp.zeros((), BF16))
            gate = sel if gate is None else gate + sel
        g_ref[pl.ds(pl.multiple_of(i1 * PEER_KEYS, PEER_KEYS), PEER_KEYS), :] = gate
        return carry

    lax.fori_loop(0, PEER_KEYS, build, 0)


def _peer_route(q, sub_keys, *, tm):
    t = q.shape[0]
    assert t % tm == 0
    hp = 2 * PEER_HEADS
    keys = sub_keys.reshape(hp, PEER_KEYS, PEER_HALF)
    n_exp = PEER_KEYS * PEER_KEYS
    per_head = lambda dt: pltpu.VMEM((PEER_HEADS, PEER_KEYS, tm), dt)
    return pl.pallas_call(
        _peer_route_body,
        grid=(t // tm,),
        in_specs=[pl.BlockSpec((tm, hp * PEER_HALF), lambda i: (i, 0)),
                  pl.BlockSpec((hp, PEER_KEYS, PEER_HALF), lambda i: (0, 0, 0))],
        out_specs=pl.BlockSpec((n_exp, tm), lambda i: (0, i)),
        out_shape=jax.ShapeDtypeStruct((n_exp, t), BF16),
        scratch_shapes=[per_head(F32), per_head(F32), per_head(BF16), per_head(BF16)],
        compiler_params=_params("arbitrary"),
        name="peer_route",
    )(q, keys)


def _peer_dense_body(xt_ref, u_ref, v_ref, g_ref, o_ref):
    @pl.when(pl.program_id(1) == 0)
    def _():
        o_ref[...] = jnp.zeros_like(o_ref)

    act_t = jnp.dot(u_ref[...], xt_ref[...], preferred_element_type=F32)
    inner = GELU_C * (act_t + 0.044715 * (act_t ** 3))
    cdf = 0.5 * (1.0 + jnp.tanh(inner.astype(BF16)))
    w_t = act_t.astype(BF16) * cdf * g_ref[...]
    o_ref[...] += jnp.dot(w_t.T, v_ref[...], preferred_element_type=F32)


def _peer_dense(h_t, u, v, gate_t, *, tm, te):
    d, t = h_t.shape
    n_exp = u.shape[0]
    assert t % tm == 0 and n_exp % te == 0
    once = dict(pipeline_mode=pl.Buffered(1))
    return pl.pallas_call(
        _peer_dense_body,
        grid=(t // tm, n_exp // te),
        in_specs=[pl.BlockSpec((d, tm), lambda i, e: (0, i), **once),
                  pl.BlockSpec((te, d), lambda i, e: (e, 0)),
                  pl.BlockSpec((te, d), lambda i, e: (e, 0)),
                  pl.BlockSpec((te, tm), lambda i, e: (e, i))],
        out_specs=pl.BlockSpec((tm, d), lambda i, e: (i, 0), **once),
        out_shape=jax.ShapeDtypeStruct((t, d), F32),
        compiler_params=_params("arbitrary", "arbitrary"),
        name="peer_dense",
    )(h_t, u, v, gate_t)


MM_TM, MM_TN = 1024, 512
ROW_TM = 256
DIL_BLOCKS_PER_STEP = 4
PEER_TM, PEER_TE = 1024, 512
ROUTE_TM = 256


def kernel(x, mem, positions, w_in, gm_ln_g, gm_ln_b, gm_w_s, gm_b_s, w_mem_kv, w_gate, b_gate,
           w_br_gmlp, w_br_dil, w_br_mem, w_out, ln_mix_g, ln_mix_b, peer_w_query, peer_sub_keys,
           peer_u, peer_v, ln_ffn_g, ln_ffn_b):
    batch, seq, d = x.shape
    t = batch * seq
    assert w_in.shape[0] == DEPTH and w_in.shape[2] == IN_WIDTH
    x2 = x.reshape(t, d)
    x_bf = x2.astype(BF16)
    mem_bf = mem.reshape(-1, d).astype(BF16)

    n_exp = PEER_KEYS * PEER_KEYS
    proj, v_bf = _matmul(x_bf, w_in.reshape(d, IN_WIDTH), out_dtype=BF16, tm=MM_TM, tn=MM_TN,
                         side=peer_v.reshape(n_exp, d), name="in_proj")
    gates, u_bf = _matmul(x_bf, w_gate.reshape(d, 3 * d), bias=b_gate.reshape(3 * d),
                          act="sigmoid", out_dtype=BF16, tm=MM_TM, tn=MM_TN,
                          side=peer_u.reshape(n_exp, d), name="gate_proj")
    mkv = _matmul(mem_bf, w_mem_kv.reshape(d, 2 * MEM_WIDTH), out_dtype=BF16, tm=MM_TM, tn=MM_TN,
                  name="mem_kv_proj")

    a_out = _gmlp(proj, gm_ln_g.reshape(-1), gm_ln_b.reshape(-1), gm_w_s.reshape(GM_GROUPS, BLOCK, BLOCK),
                  gm_b_s.reshape(GM_GROUPS, BLOCK), tm=512)

    cos, sin = _rope_table(positions, tm=1024)
    outs, lses = [], []
    for g, dil in enumerate(DILATIONS):
        qg, kg, vg = _rope_perm(proj, cos, sin, g, dil)
        o, lse = _dilated_attention(qg, kg, vg, dil, batch, seq, nblk=DIL_BLOCKS_PER_STEP)
        outs.append(o)
        lses.append(lse)
    b_out = _dil_merge(outs, lses)

    m_out = _mem_attention(proj, mkv, batch, seq, tm=512)

    merged = _merge(a_out, b_out, m_out, w_br_gmlp.reshape(GM_WIDTH, d),
                    w_br_dil.reshape(DIL_GROUP_WIDTH, d), w_br_mem.reshape(MEM_WIDTH, d), gates,
                    tm=MM_TM, tn=MM_TN)
    z1 = _matmul(merged, w_out.reshape(d, d), res=x2, alpha=ALPHA, out_dtype=F32, tm=MM_TM,
                 tn=MM_TN, name="out_proj")
    h1_bf, h1_t = _layer_norm(z1, ln_mix_g.reshape(-1), ln_mix_b.reshape(-1), want_f32=False,
                              want_t=True, tm=ROW_TM, name="ln_mix")

    hq = PEER_HEADS * 2 * PEER_HALF
    q = _matmul(h1_bf, peer_w_query.reshape(d, hq), out_dtype=F32, tm=MM_TM, tn=MM_TN,
                name="peer_query")
    gate_t = _peer_route(q, peer_sub_keys, tm=ROUTE_TM)
    y = _peer_dense(h1_t, u_bf, v_bf, gate_t, tm=PEER_TM, te=PEER_TE)
    (out,) = _layer_norm(h1_bf, ln_ffn_g.reshape(-1), ln_ffn_b.reshape(-1), add=y, alpha=ALPHA,
                         tm=ROW_TM, name="ln_ffn")
    return out.reshape(batch, seq, d)
```
